```python
import math
import functools
import jax
import jax.numpy as jnp
from jax import lax

D_MODEL = 1024
BATCH = 2
SEQ = 8192
DEPTH = 4
DEC_BATCH = 128
DEC_SEQ = 1
PAST_LEN = 2048
PAGE_SIZE = 128

HEAD_DIM = D_MODEL // 16
BRANCH_W = 4 * HEAD_DIM
N_BRANCH = 4
RWKV_HEADS = 4
RWKV_W = RWKV_HEADS * HEAD_DIM
RWKV_DECAY_LORA = 64
RWKV_A_LORA = 64
RWKV_G_LORA = 128
RWKV_P = 3 * RWKV_W + RWKV_DECAY_LORA + RWKV_A_LORA + RWKV_G_LORA
RWKV_LN_EPS = 64e-5
ATT_HEADS = 4
ATT_W = ATT_HEADS * HEAD_DIM
IDX_HEADS = 4
IDX_DIM = HEAD_DIM
ATT_P = 3 * ATT_W + IDX_HEADS * IDX_DIM + IDX_DIM + IDX_HEADS
TOPK_MAX = 256
Q_BLOCK = 128
ROPE_THETA = 500000.0
ROT_DIM = HEAD_DIM // 4
ATT_SCALE = HEAD_DIM ** -0.5
IDX_SCALE = (IDX_HEADS * IDX_DIM) ** -0.5
RET_HEADS = 4
RET_W = RET_HEADS * HEAD_DIM
RET_P = 4 * RET_W
RET_CHUNK = 128
RET_THETA = 10000.0
RET_KSCALE = HEAD_DIM ** -0.5
POOL_WINDOWS = (2, 4, 8, 16)
POOL_GROUPS = 4
POOL_GW = HEAD_DIM
POOL_W = POOL_GROUPS * POOL_GW
POOL_BUF = 15
GATE_P = N_BRANCH * D_MODEL
P_TOTAL = RWKV_P + ATT_P + RET_P + POOL_W + GATE_P
D_FF = -(-8 * D_MODEL // 768) * 256
NORM_EPS = 1e-6
GN_EPS = 1e-5

kernel_name = 'hybrid_rwkv7_dsa_retnet_pool_step'

F32 = jnp.float32


def _split(t, sizes):
    out, o = [], 0
    for s in sizes:
        out.append(t[..., o:o + s])
        o += s
    return out


def rms_norm(x, w):
    xf = x.astype(F32)
    y = xf * lax.rsqrt(jnp.mean(xf * xf, axis=-1, keepdims=True) + NORM_EPS)
    return (y * w.astype(F32)).astype(x.dtype)


def head_group_norm(x, w, b, eps):
    xf = x.astype(F32)
    xc = xf - jnp.mean(xf, axis=-1, keepdims=True)
    y = xc * lax.rsqrt(jnp.mean(xc * xc, axis=-1, keepdims=True) + eps)
    return y.reshape(*x.shape[:-2], -1) * w.astype(F32) + b.astype(F32)


def rope(x, pos, rot_dim, theta):
    half = rot_dim // 2
    inv = theta ** (-jnp.arange(half, dtype=F32) / half)
    ang = pos.astype(F32)[:, None] * inv[None, :]
    cos = jnp.cos(ang)[None, :, None, :]
    sin = jnp.sin(ang)[None, :, None, :]
    xf = x.astype(F32)
    x1, x2 = xf[..., :half], xf[..., half:rot_dim]
    out = jnp.concatenate([x1 * cos - x2 * sin, x1 * sin + x2 * cos, xf[..., rot_dim:]], axis=-1)
    return out.astype(x.dtype)


def rwkv7_mix(p, prev, wkv0, mu, w0, w_w2, a0, w_a2, w_g2, k_k, k_a, r_k, ln_w, ln_b):
    B, T, _ = p.shape
    shifted = jnp.concatenate([prev[:, None, :].astype(p.dtype), p[:, :-1]], axis=1)
    xl = p + (shifted - p) * mu
    r, k, v, wl, al, gl = _split(xl, (RWKV_W, RWKV_W, RWKV_W, RWKV_DECAY_LORA, RWKV_A_LORA, RWKV_G_LORA))
    w_log = -jax.nn.softplus(-(w0 + jnp.tanh(wl) @ w_w2).astype(F32)) - 0.5
    decay = jnp.exp(-jnp.exp(w_log))
    a = jax.nn.sigmoid((a0 + al @ w_a2).astype(F32))
    g = (jax.nn.sigmoid(gl) @ w_g2).astype(F32)
    heads = lambda t: t.reshape(B, T, RWKV_HEADS, HEAD_DIM).astype(F32)
    r, k, v, a, decay = heads(r), heads(k), heads(v), heads(a), heads(decay)
    kk = k * k_k.reshape(RWKV_HEADS, HEAD_DIM).astype(F32)
    kk = kk / jnp.sqrt(jnp.sum(kk * kk, axis=-1, keepdims=True) + 1e-12)
    k = k * (1.0 + (a - 1.0) * k_a.reshape(RWKV_HEADS, HEAD_DIM).astype(F32))

    def step(S, inp):
        r_t, w_t, k_t, v_t, kk_t, a_t = inp
        sa = jnp.einsum('bhvk,bhk->bhv', S, kk_t)
        S = (S * w_t[:, :, None, :] - sa[..., None] * (kk_t * a_t)[:, :, None, :]
             + v_t[..., None] * k_t[:, :, None, :])
        return S, jnp.einsum('bhvk,bhk->bhv', S, r_t)

    tm = lambda t: jnp.swapaxes(t, 0, 1)
    S, y = lax.scan(step, wkv0.astype(F32), (tm(r), tm(decay), tm(k), tm(v), tm(kk), tm(a)))
    y = tm(y)
    o = head_group_norm(y, ln_w, ln_b, RWKV_LN_EPS)
    bonus = (jnp.sum(r * k * r_k.astype(F32), axis=-1, keepdims=True) * v).reshape(B, T, RWKV_W)
    return ((o + bonus) * g).astype(p.dtype), S.astype(p.dtype)


def index_scores(qi, wi, keys):
    s = jnp.einsum('bqhd,bsd->bqhs', qi.astype(F32), keys.astype(F32))
    return jnp.einsum('bqhs,bqh->bqs', jax.nn.relu(s), wi.astype(F32)) * IDX_SCALE


def attend_selected(q, ks, vs, valid):
    logit = jnp.einsum('bqhd,bqkhd->bqhk', q.astype(F32), ks.astype(F32)) * ATT_SCALE
    logit = jnp.where(valid[:, :, None, :], logit, -jnp.inf)
    prob = jax.nn.softmax(logit, axis=-1)
    return jnp.einsum('bqhk,bqkhd->bqhd', prob, vs.astype(F32)).astype(q.dtype)


def dsa_prompt(q, k, v, qi, ki, wi):
    B, T = q.shape[:2]
    n_sel = min(TOPK_MAX, T // 4)
    qb = min(Q_BLOCK, T)
    nblk = T // qb
    take = jax.vmap(lambda arr, i: arr[i])
    key_pos = jnp.arange(T)

    def block(args):
        q_b, qi_b, wi_b, pos_b = args
        score = index_scores(qi_b, wi_b, ki)
        score = jnp.where(key_pos[None, None, :] <= pos_b[None, :, None], score, -jnp.inf)
        _, idx = lax.top_k(score, n_sel)
        valid = idx <= pos_b[None, :, None]
        return attend_selected(q_b, take(k, idx), take(v, idx), valid)

    blk = lambda t: t.reshape(B, nblk, qb, *t.shape[2:]).swapaxes(0, 1)
    out = lax.map(block, (blk(q), blk(qi), blk(wi), key_pos.reshape(nblk, qb)))
    return out.swapaxes(0, 1).reshape(q.shape)


def dsa_sample(q, k, v, qi, ki, wi, ck, cv, cki, page_table):
    nb, T = q.shape[:2]
    ps = ck.shape[1]
    past = page_table.shape[1] * ps
    L = past + T
    n_sel = min(TOPK_MAX, L // 4)
    past_ki = cki[page_table].reshape(nb, past, IDX_DIM)
    keys = jnp.concatenate([past_ki.astype(ki.dtype), ki], axis=1)
    qpos = past + jnp.arange(T)
    score = index_scores(qi, wi, keys)
    score = jnp.where(jnp.arange(L)[None, None, :] <= qpos[None, :, None], score, -jnp.inf)
    _, idx = lax.top_k(score, n_sel)
    in_past = (idx < past)[..., None, None]
    pidx = jnp.minimum(idx, past - 1)
    phys = jax.vmap(lambda pt, i: pt[i])(page_table, pidx // ps)
    off = pidx % ps
    take = jax.vmap(lambda arr, i: arr[i])
    nidx = jnp.clip(idx - past, 0, T - 1)
    ks = jnp.where(in_past, ck[phys, off].astype(k.dtype), take(k, nidx))
    vs = jnp.where(in_past, cv[phys, off].astype(v.dtype), take(v, nidx))
    valid = idx <= qpos[None, :, None]
    return attend_selected(q, ks, vs, valid)


def retention(q, k, v, S0, chunk):
    B, T, H, D = q.shape
    nc = T // chunk
    log_g = jnp.log1p(-jnp.exp2(-5.0 - jnp.arange(H, dtype=F32)))
    n = jnp.arange(chunk, dtype=F32)
    diff = n[:, None] - n[None, :]
    dmask = jnp.where(diff >= 0, jnp.exp(log_g[:, None, None] * jnp.maximum(diff, 0.0)), 0.0)
    q_dec = jnp.exp(log_g[:, None] * (n + 1.0)).T
    k_dec = jnp.exp(log_g[:, None] * (chunk - 1.0 - n))
    s_dec = jnp.exp(log_g * chunk)
    cs = lambda t: t.astype(F32).reshape(B, nc, chunk, H, D).swapaxes(0, 1)

    def step(S, inp):
        qc, kc, vc = inp
        att = jnp.einsum('bnhd,bmhd->bhnm', qc, kc) * dmask
        o = (jnp.einsum('bhnm,bmhd->bnhd', att, vc)
             + jnp.einsum('bnhk,bhkv->bnhv', qc, S) * q_dec[None, :, :, None])
        S = S * s_dec[None, :, None, None] + jnp.einsum('bmhk,bmhv,hm->bhkv', kc, vc, k_dec)
        return S, o

    S, o = lax.scan(step, S0.astype(F32), (cs(q), cs(k), cs(v)))
    return o.swapaxes(0, 1).reshape(B, T, H, D), S


def pool_mix(u, buf, pos0, w_pool, scale):
    B, T, _ = u.shape
    full = jnp.concatenate([buf.astype(u.dtype), u], axis=1).astype(F32)
    csum = jnp.concatenate([jnp.zeros((B, 1, POOL_W), F32), jnp.cumsum(full, axis=1)], axis=1)
    end = csum[:, POOL_BUF + 1:]
    pos = (pos0 + jnp.arange(T)).astype(F32)
    means = []
    for gi, win in enumerate(POOL_WINDOWS):
        c = slice(gi * POOL_GW, (gi + 1) * POOL_GW)
        start = csum[:, POOL_BUF + 1 - win:POOL_BUF + 1 - win + T, c]
        cnt = jnp.minimum(pos + 1.0, float(win))[None, :, None]
        means.append((end[..., c] - start) / cnt)
    d = (jnp.concatenate(means, axis=-1) - full[:, POOL_BUF:]).reshape(B, T, POOL_GROUPS, POOL_GW)
    y = jnp.einsum('btgi,gio->btgo', d, w_pool.astype(F32)).reshape(B, T, POOL_W) * scale.astype(F32)
    return y.astype(u.dtype), full[:, -POOL_BUF:].astype(u.dtype)


def trunk_layer(x, pos0, shift0, wkv0, ret0, pool0, attend, lw):
    (norm1_w, w_in, rwkv_mu, rwkv_w0, rwkv_w_w2, rwkv_a0, rwkv_w_a2, rwkv_w_g2, rwkv_k_k,
     rwkv_k_a, rwkv_r_k, rwkv_ln_w, rwkv_ln_b, q_norm_w, k_norm_w, idx_k_norm_w, ret_gn_w,
     ret_gn_b, pool_w, pool_scale, w_branch, w_out, norm2_w, w_ffn_in, w_ffn_out) = lw
    B, T, _ = x.shape
    dt = x.dtype
    pos = pos0 + jnp.arange(T)
    h = rms_norm(x, norm1_w)
    p = h @ w_in
    p_rwkv, p_att, p_ret, p_pool, p_gate = _split(p, (RWKV_P, ATT_P, RET_P, POOL_W, GATE_P))
    heads = lambda t, nh: t.reshape(B, T, nh, -1)
    o_a, wkv1 = rwkv7_mix(p_rwkv, shift0, wkv0, rwkv_mu, rwkv_w0, rwkv_w_w2, rwkv_a0, rwkv_w_a2,
                          rwkv_w_g2, rwkv_k_k, rwkv_k_a, rwkv_r_k, rwkv_ln_w, rwkv_ln_b)
    shift1 = p_rwkv[:, -1]
    q, k, v, qi, ki, wi = _split(p_att, (ATT_W, ATT_W, ATT_W, IDX_HEADS * IDX_DIM, IDX_DIM, IDX_HEADS))
    q = rope(rms_norm(heads(q, ATT_HEADS), q_norm_w), pos, ROT_DIM, ROPE_THETA)
    k = rope(rms_norm(heads(k, ATT_HEADS), k_norm_w), pos, ROT_DIM, ROPE_THETA)
    v = heads(v, ATT_HEADS)
    qi = rope(heads(qi, IDX_HEADS), pos, ROT_DIM, ROPE_THETA)
    ki = rope(rms_norm(ki, idx_k_norm_w)[:, :, None, :], pos, ROT_DIM, ROPE_THETA)[:, :, 0, :]
    o_b = attend(q, k, v, qi, ki, wi).reshape(B, T, ATT_W)
    rq, rk, rv, rg = _split(p_ret, (RET_W, RET_W, RET_W, RET_W))
    rq = rope(heads(rq, RET_HEADS), pos, HEAD_DIM, RET_THETA)
    rk = rope(heads(rk, RET_HEADS), pos, HEAD_DIM, RET_THETA) * RET_KSCALE
    o_ret, ret1 = retention(rq, rk, heads(rv, RET_HEADS), ret0, math.gcd(T, RET_CHUNK))
    o_c = (jax.nn.silu(rg.astype(F32)) * head_group_norm(o_ret, ret_gn_w, ret_gn_b, GN_EPS)).astype(dt)
    o_d, pool1 = pool_mix(p_pool, pool0, pos0, pool_w, pool_scale)
    branches = jnp.stack([o_a, o_b, o_c, o_d], axis=2)
    up = jnp.einsum('btnc,ncd->btnd', branches, w_branch)
    gates = jax.nn.sigmoid(p_gate.reshape(B, T, N_BRANCH, D_MODEL))
    x = x + jnp.sum(gates * up, axis=2) @ w_out
    g, u = _split(rms_norm(x, norm2_w) @ w_ffn_in, (D_FF, D_FF))
    x = x + (jax.nn.silu(g) * u) @ w_ffn_out
    return x, (k, v, ki, wkv1, shift1, ret1.astype(dt), pool1)


def setup_inputs(seed: int = 0) -> dict:
    key = jax.random.key(seed)
    keys = list(jax.random.split(key, 48))

    def nrm(shape, scale):
        return jax.random.normal(keys.pop(), shape, F32) * scale

    def gain(shape):
        return 1.0 + nrm(shape, 0.02)

    n_pages = PAST_LEN // PAGE_SIZE
    n_used = DEC_BATCH * n_pages
    n_pool = n_used + n_used // 4
    perm = jax.random.permutation(keys.pop(), n_pool)
    page_table = perm[:n_used].reshape(DEC_BATCH, n_pages).astype(jnp.int32)
    L = DEPTH
    return {
        'x_prompt': nrm((BATCH, SEQ, D_MODEL), 1.0),
        'x_sample': nrm((DEC_BATCH, DEC_SEQ, D_MODEL), 1.0),
        'cache_k': nrm((L, n_pool, PAGE_SIZE, ATT_HEADS, HEAD_DIM), 1.0),
        'cache_v': nrm((L, n_pool, PAGE_SIZE, ATT_HEADS, HEAD_DIM), 1.0),
        'cache_kidx': nrm((L, n_pool, PAGE_SIZE, IDX_DIM), 1.0),
        'state_wkv': nrm((L, DEC_BATCH, RWKV_HEADS, HEAD_DIM, HEAD_DIM), 0.1),
        'state_shift': nrm((L, DEC_BATCH, RWKV_P), 1.0),
        'state_ret': nrm((L, DEC_BATCH, RET_HEADS, HEAD_DIM, HEAD_DIM), 0.5),
        'state_pool': nrm((L, DEC_BATCH, POOL_BUF, POOL_W), 1.0),
        'page_table': page_table,
        'norm1_w': gain((L, D_MODEL)),
        'w_in': nrm((L, D_MODEL, P_TOTAL), D_MODEL ** -0.5),
        'rwkv_mu': jax.random.uniform(keys.pop(), (L, RWKV_P), F32),
        'rwkv_w0': jax.random.uniform(keys.pop(), (L, RWKV_W), F32, -6.0, -1.0),
        'rwkv_w_w2': nrm((L, RWKV_DECAY_LORA, RWKV_W), 0.1 * RWKV_DECAY_LORA ** -0.5),
        'rwkv_a0': nrm((L, RWKV_W), 0.1),
        'rwkv_w_a2': nrm((L, RWKV_A_LORA, RWKV_W), RWKV_A_LORA ** -0.5),
        'rwkv_w_g2': nrm((L, RWKV_G_LORA, RWKV_W), RWKV_G_LORA ** -0.5),
        'rwkv_k_k': 0.85 + nrm((L, RWKV_W), 0.02),
        'rwkv_k_a': gain((L, RWKV_W)),
        'rwkv_r_k': nrm((L, RWKV_HEADS, HEAD_DIM), 0.1),
        'rwkv_ln_w': gain((L, RWKV_W)),
        'rwkv_ln_b': nrm((L, RWKV_W), 0.02),
        'q_norm_w': gain((L, HEAD_DIM)),
        'k_norm_w': gain((L, HEAD_DIM)),
        'idx_k_norm_w': gain((L, IDX_DIM)),
        'ret_gn_w': gain((L, RET_W)),
        'ret_gn_b': nrm((L, RET_W), 0.02),
        'pool_w': nrm((L, POOL_GROUPS, POOL_GW, POOL_GW), POOL_GW ** -0.5),
        'pool_scale': gain((L, POOL_W)),
        'w_branch': nrm((L, N_BRANCH, BRANCH_W, D_MODEL), BRANCH_W ** -0.5),
        'w_out': nrm((L, D_MODEL, D_MODEL), D_MODEL ** -0.5),
        'norm2_w': gain((L, D_MODEL)),
        'w_ffn_in': nrm((L, D_MODEL, 2 * D_FF), D_MODEL ** -0.5),
        'w_ffn_out': nrm((L, D_FF, D_MODEL), D_FF ** -0.5),
    }


def reference(x_prompt, x_sample, cache_k, cache_v, cache_kidx, state_wkv, state_shift, state_ret,
              state_pool, page_table, norm1_w, w_in, rwkv_mu, rwkv_w0, rwkv_w_w2, rwkv_a0, rwkv_w_a2,
              rwkv_w_g2, rwkv_k_k, rwkv_k_a, rwkv_r_k, rwkv_ln_w, rwkv_ln_b, q_norm_w, k_norm_w,
              idx_k_norm_w, ret_gn_w, ret_gn_b, pool_w, pool_scale, w_branch, w_out, norm2_w,
              w_ffn_in, w_ffn_out):
    past_len = page_table.shape[1] * cache_k.shape[2]
    B = x_prompt.shape[0]
    dt = x_prompt.dtype
    xp, xs = x_prompt, x_sample
    prompt_new = [[] for _ in range(7)]
    sample_new = [[] for _ in range(7)]
    for l in range(DEPTH):
        lw = (norm1_w[l], w_in[l], rwkv_mu[l], rwkv_w0[l], rwkv_w_w2[l], rwkv_a0[l], rwkv_w_a2[l],
              rwkv_w_g2[l], rwkv_k_k[l], rwkv_k_a[l], rwkv_r_k[l], rwkv_ln_w[l], rwkv_ln_b[l],
              q_norm_w[l], k_norm_w[l], idx_k_norm_w[l], ret_gn_w[l], ret_gn_b[l], pool_w[l],
              pool_scale[l], w_branch[l], w_out[l], norm2_w[l], w_ffn_in[l], w_ffn_out[l])
        xp, st = trunk_layer(
            xp, 0,
            jnp.zeros((B, RWKV_P), dt),
            jnp.zeros((B, RWKV_HEADS, HEAD_DIM, HEAD_DIM), dt),
            jnp.zeros((B, RET_HEADS, HEAD_DIM, HEAD_DIM), dt),
            jnp.zeros((B, POOL_BUF, POOL_W), dt),
            dsa_prompt, lw)
        for lst, s in zip(prompt_new, st):
            lst.append(s)
        attend = functools.partial(dsa_sample, ck=cache_k[l], cv=cache_v[l], cki=cache_kidx[l],
                                   page_table=page_table)
        xs, st = trunk_layer(xs, past_len, state_shift[l], state_wkv[l], state_ret[l], state_pool[l],
                             attend, lw)
        for lst, s in zip(sample_new, st):
            lst.append(s)
    p_k, p_v, p_kidx, p_wkv, p_shift, p_ret, p_pool = [jnp.stack(a) for a in prompt_new]
    s_k, s_v, s_kidx, s_wkv, s_shift, s_ret, s_pool = [jnp.stack(a) for a in sample_new]
    return (xp, xs, p_k, p_v, p_kidx, p_wkv, p_shift, p_ret, p_pool,
            s_k, s_v, s_kidx, s_wkv, s_shift, s_ret, s_pool)
```

```python
import functools
import math

import jax
import jax.numpy as jnp
from jax import lax
from jax.experimental import pallas as pl
from jax.experimental.pallas import tpu as pltpu

F32 = jnp.float32
BF16 = jnp.bfloat16
I32 = jnp.int32

D_MODEL = 1024
DEPTH = 4
HEAD_DIM = 64
N_HEADS = 4
BRANCH_W = N_HEADS * HEAD_DIM
N_BRANCH = 4
RWKV_DECAY_LORA = 64
RWKV_A_LORA = 64
RWKV_G_LORA = 128
RWKV_P = 3 * BRANCH_W + RWKV_DECAY_LORA + RWKV_A_LORA + RWKV_G_LORA
RWKV_LN_EPS = 64e-5
ATT_P = 3 * BRANCH_W + N_HEADS * HEAD_DIM + HEAD_DIM + N_HEADS
ATT_P_PAD = 1280
TOPK_MAX = 256
ROPE_THETA = 500000.0
ROT_DIM = HEAD_DIM // 4
ATT_SCALE = HEAD_DIM ** -0.5
IDX_SCALE = (N_HEADS * HEAD_DIM) ** -0.5
RET_P = 4 * BRANCH_W
RET_CHUNK = 128
RET_THETA = 10000.0
RET_KSCALE = HEAD_DIM ** -0.5
POOL_WINDOWS = (2, 4, 8, 16)
POOL_BUF = 15
GATE_P = N_BRANCH * D_MODEL
D_FF = -(-8 * D_MODEL // 768) * 256
NORM_EPS = 1e-6
GN_EPS = 1e-5

COL_GATE = 0
COL_RWKV = GATE_P
COL_RET = COL_RWKV + RWKV_P
COL_POOL = COL_RET + RET_P
COL_ATT = COL_POOL + BRANCH_W
P_PAD = COL_ATT + ATT_P_PAD

LANES = 128
INT_MIN = -2 ** 31
NEG_BIG = -1e30
VMEM_LIMIT = 56 * 1024 * 1024


def _cparams(n_axes):
    return pltpu.CompilerParams(dimension_semantics=("arbitrary",) * n_axes,
                                vmem_limit_bytes=VMEM_LIMIT)


def _dot(a, b):
    return jnp.dot(a, b, preferred_element_type=F32)


def _dot_nt(a, b):
    return lax.dot_general(a, b, (((1,), (1,)), ((), ())), preferred_element_type=F32)


def _seg_sum(x, ones_bd):
    hi = x.astype(BF16)
    lo = (x - hi.astype(F32)).astype(BF16)
    return _dot(hi, ones_bd) + _dot(lo, ones_bd)


def _head_group_norm(x, ones_bd, w, b, eps):
    mean = _seg_sum(x, ones_bd) * (1.0 / HEAD_DIM)
    xc = x - mean
    var = _seg_sum(xc * xc, ones_bd) * (1.0 / HEAD_DIM)
    return xc * lax.rsqrt(var + eps) * w + b


def _rope(x, cos, sin_lo, sin_hi, half):
    n = x.shape[-1]
    return x * cos + pltpu.roll(x, n - half, 1) * sin_lo + pltpu.roll(x, half, 1) * sin_hi


def _norm_matmul_kernel(x_ref, nw_ref, w_ref, o_ref, h_ref):
    @pl.when(pl.program_id(1) == 0)
    def _():
        x = x_ref[...]
        y = x * lax.rsqrt(jnp.mean(x * x, axis=-1, keepdims=True) + NORM_EPS)
        h_ref[...] = (y * nw_ref[...]).astype(BF16)

    o_ref[...] = _dot(h_ref[...], w_ref[...])


def _norm_matmul(x, norm_w, w_bf16, tm, tn):
    n, d = x.shape
    p = w_bf16.shape[1]
    return pl.pallas_call(
        _norm_matmul_kernel,
        grid=(n // tm, p // tn),
        in_specs=[pl.BlockSpec((tm, d), lambda i, j: (i, 0)),
                  pl.BlockSpec((1, d), lambda i, j: (0, 0)),
                  pl.BlockSpec((d, tn), lambda i, j: (0, j))],
        out_specs=pl.BlockSpec((tm, tn), lambda i, j: (i, j)),
        out_shape=jax.ShapeDtypeStruct((n, p), F32),
        scratch_shapes=[pltpu.VMEM((tm, d), BF16)],
        compiler_params=_cparams(2),
        name="norm_in_proj",
    )(x, norm_w.reshape(1, d), w_bf16)


def _merge_kernel(x_ref, oa_ref, ob_ref, oc_ref, od_ref, gate_ref, wb_ref, wo_ref, out_ref):
    acc = None
    for b, o_ref in enumerate((oa_ref, ob_ref, oc_ref, od_ref)):
        up = _dot(o_ref[...].astype(BF16), wb_ref[b])
        g = jax.nn.sigmoid(gate_ref[:, b * D_MODEL:(b + 1) * D_MODEL])
        acc = g * up if acc is None else acc + g * up
    out_ref[...] = x_ref[...] + _dot(acc.astype(BF16), wo_ref[...])


def _merge(x, branches, p, w_branch_bf16, w_out_bf16, tm):
    n = x.shape[0]
    row = lambda w: pl.BlockSpec((tm, w), lambda i: (i, 0))
    return pl.pallas_call(
        _merge_kernel,
        grid=(n // tm,),
        in_specs=[row(D_MODEL), row(BRANCH_W), row(BRANCH_W), row(BRANCH_W), row(BRANCH_W),
                  pl.BlockSpec((tm, GATE_P), lambda i: (i, COL_GATE // GATE_P)),
                  pl.BlockSpec((N_BRANCH, BRANCH_W, D_MODEL), lambda i: (0, 0, 0)),
                  pl.BlockSpec((D_MODEL, D_MODEL), lambda i: (0, 0))],
        out_specs=row(D_MODEL),
        out_shape=jax.ShapeDtypeStruct((n, D_MODEL), F32),
        compiler_params=_cparams(1),
        name="gated_merge_out_proj",
    )(x, *branches, p, w_branch_bf16, w_out_bf16)


FF_CHUNK = D_FF // 2


def _ffn_kernel(x_ref, nw_ref, wi_ref, wo_ref, out_ref):
    x = x_ref[...]
    h = (x * lax.rsqrt(jnp.mean(x * x, axis=-1, keepdims=True) + NORM_EPS) * nw_ref[...]).astype(BF16)
    acc = x
    for c in range(D_FF // FF_CHUNK):
        g = _dot(h, wi_ref[:, c * FF_CHUNK:(c + 1) * FF_CHUNK])
        u = _dot(h, wi_ref[:, D_FF + c * FF_CHUNK:D_FF + (c + 1) * FF_CHUNK])
        a = (g * jax.nn.sigmoid(g) * u).astype(BF16)
        acc = acc + _dot(a, wo_ref[c * FF_CHUNK:(c + 1) * FF_CHUNK, :])
    out_ref[...] = acc


def _ffn(x, norm_w, w_in_bf16, w_out_bf16, tm):
    n = x.shape[0]
    return pl.pallas_call(
        _ffn_kernel,
        grid=(n // tm,),
        in_specs=[pl.BlockSpec((tm, D_MODEL), lambda i: (i, 0)),
                  pl.BlockSpec((1, D_MODEL), lambda i: (0, 0)),
                  pl.BlockSpec((D_MODEL, 2 * D_FF), lambda i: (0, 0)),
                  pl.BlockSpec((D_FF, D_MODEL), lambda i: (0, 0))],
        out_specs=pl.BlockSpec((tm, D_MODEL), lambda i: (i, 0)),
        out_shape=jax.ShapeDtypeStruct((n, D_MODEL), F32),
        compiler_params=_cparams(1),
        name="swiglu_ffn",
    )(x, norm_w.reshape(1, D_MODEL), w_in_bf16, w_out_bf16)


def _att_prep_kernel(p_ref, cos_ref, slo_ref, shi_ref, qnw_ref, knw_ref, inw_ref, ones_ref,
                     q_ref, k_ref, v_ref, ki_ref, kb_ref, vt_ref, qib_ref, kib_ref, wit_ref):
    cos, slo, shi = cos_ref[...], slo_ref[...], shi_ref[...]
    ones_bd = ones_ref[...]
    half = ROT_DIM // 2

    def head_rms(x, w):
        ms = _seg_sum(x * x, ones_bd) * (1.0 / HEAD_DIM)
        return x * lax.rsqrt(ms + NORM_EPS) * w

    q = _rope(head_rms(p_ref[:, 0:256], qnw_ref[...]), cos, slo, shi, half)
    k = _rope(head_rms(p_ref[:, 256:512], knw_ref[...]), cos, slo, shi, half)
    v = p_ref[:, 512:768]
    qi = _rope(p_ref[:, 768:1024], cos, slo, shi, half)
    tail = p_ref[:, 1024:1152]
    lane = lax.broadcasted_iota(I32, tail.shape, 1)
    is_ki = lane < HEAD_DIM
    kiraw = jnp.where(is_ki, tail, 0.0)
    ms = jnp.sum(kiraw * kiraw, axis=-1, keepdims=True) * (1.0 / HEAD_DIM)
    kin = kiraw * lax.rsqrt(ms + NORM_EPS) * inw_ref[...]
    ki = _rope(kin, cos[:, :LANES], slo[:, :LANES], shi[:, :LANES], half)
    ki = jnp.where(is_ki, ki, 0.0)

    q_ref[...] = q * ATT_SCALE
    k_ref[...] = k
    v_ref[...] = v
    ki_ref[...] = ki
    kb_ref[...] = k.astype(BF16)
    vt_ref[0] = v.T.astype(BF16)
    qib_ref[...] = qi.astype(BF16)
    kib_ref[...] = ki.astype(BF16)
    wi = jnp.where(is_ki, 0.0, tail) * IDX_SCALE
    wit_ref[0] = pltpu.roll(wi, LANES - HEAD_DIM, 1).T[0:8, :]


def _att_prep(p, tables, q_norm_w, k_norm_w, idx_k_norm_w, ones_bd, n_seq, t_len, tm):
    n = p.shape[0]
    tpb = t_len // tm
    cos, slo, shi = tables
    tile4 = lambda w: jnp.tile(w.reshape(1, HEAD_DIM), (1, N_HEADS))
    inw = jnp.concatenate([idx_k_norm_w.reshape(1, HEAD_DIM), jnp.ones((1, LANES - HEAD_DIM), F32)], axis=1)
    row = lambda w: pl.BlockSpec((tm, w), lambda i: (i, 0))
    tab = pl.BlockSpec((tm, BRANCH_W), lambda i: (i % tpb, 0))
    full = lambda a, b: pl.BlockSpec((a, b), lambda i: (0, 0))
    return pl.pallas_call(
        _att_prep_kernel,
        grid=(n // tm,),
        in_specs=[pl.BlockSpec((tm, ATT_P_PAD), lambda i: (i, COL_ATT // ATT_P_PAD)),
                  tab, tab, tab, full(1, BRANCH_W), full(1, BRANCH_W), full(1, LANES),
                  full(BRANCH_W, BRANCH_W)],
        out_specs=[row(BRANCH_W), row(BRANCH_W), row(BRANCH_W), row(LANES), row(BRANCH_W),
                   pl.BlockSpec((1, BRANCH_W, tm), lambda i: (i // tpb, 0, i % tpb)),
                   row(BRANCH_W), row(LANES),
                   pl.BlockSpec((1, 8, tm), lambda i: (i // tpb, 0, i % tpb))],
        out_shape=[jax.ShapeDtypeStruct((n, BRANCH_W), F32),
                   jax.ShapeDtypeStruct((n, BRANCH_W), F32),
                   jax.ShapeDtypeStruct((n, BRANCH_W), F32),
                   jax.ShapeDtypeStruct((n, LANES), F32),
                   jax.ShapeDtypeStruct((n, BRANCH_W), BF16),
                   jax.ShapeDtypeStruct((n_seq, BRANCH_W, t_len), BF16),
                   jax.ShapeDtypeStruct((n, BRANCH_W), BF16),
                   jax.ShapeDtypeStruct((n, LANES), BF16),
                   jax.ShapeDtypeStruct((n_seq, 8, t_len), F32)],
        compiler_params=_cparams(1),
        name="dsa_prep",
    )(p, cos, slo, shi, tile4(q_norm_w), tile4(k_norm_w), inw, ones_bd)


QB = 128


def _float_key(x):
    b = lax.bitcast_convert_type(x, I32)
    b = jnp.where(b == INT_MIN, 0, b)
    return jnp.where(b < 0, b ^ jnp.int32(0x7FFFFFFF), b)


def _dsa_prompt_kernel(n_sel, q_ref, qib_ref, wit_ref, kb_ref, vt_ref, kib_ref, o_ref,
                       key_ref, qbd_ref, vbd_ref, acc_ref, m_ref, l_ref):
    j = pl.program_id(1)
    nkb = j + 1
    q0 = j * QB

    @pl.when(jnp.logical_and(pl.program_id(0) == 0, j == 0))
    def _():
        qbd_ref[...] = jnp.zeros_like(qbd_ref)
        vbd_ref[...] = jnp.zeros_like(vbd_ref)

    qib = qib_ref[...]
    qis = jnp.concatenate([qib[:, h * HEAD_DIM:(h + 1) * HEAD_DIM] for h in range(N_HEADS)], axis=0)
    qt = q_ref[...].T.astype(BF16)
    for h in range(N_HEADS):
        qbd_ref[h * HEAD_DIM:(h + 1) * HEAD_DIM, h * QB:(h + 1) * QB] = qt[h * HEAD_DIM:(h + 1) * HEAD_DIM, :]
    wit = wit_ref[0]
    srow = lax.broadcasted_iota(I32, (QB, QB), 0)
    tcol = lax.broadcasted_iota(I32, (QB, QB), 1) + q0

    def score_blk(kb, carry):
        kib = kib_ref[0, pl.ds(kb * QB, QB), :][:, :HEAD_DIM]
        s = jnp.maximum(_dot_nt(kib, qis), 0.0)
        acc = jnp.zeros((QB, QB), F32)
        for h in range(N_HEADS):
            acc = acc + s[:, h * QB:(h + 1) * QB] * wit[h:h + 1, :]
        key = jnp.where(srow + kb * QB <= tcol, _float_key(acc), jnp.int32(INT_MIN))
        key_ref[pl.ds(kb * QB, QB), :] = key
        return carry

    lax.fori_loop(0, nkb, score_blk, 0)

    def count(pred):
        def body(kb, acc):
            m = pred(key_ref[pl.ds(kb * QB, QB), :]).astype(I32)
            return acc + jnp.sum(m.reshape(QB // 8, 8, QB), axis=0)
        acc = lax.fori_loop(0, nkb, body, jnp.zeros((8, QB), I32))
        return jnp.sum(acc, axis=0, keepdims=True)

    def bit_step(i, tau):
        cand = tau + (jnp.int32(1) << (31 - i))
        c = count(lambda blk: blk >= cand)
        return jnp.where(c >= n_sel, cand, tau)

    tau = lax.fori_loop(0, 32, bit_step, jnp.full((1, QB), INT_MIN, I32))

    cnt_gt = count(lambda blk: blk > tau)
    cnt_ge = count(lambda blk: blk >= tau)
    need = (n_sel - cnt_gt).astype(F32)
    excess = jnp.max(jnp.where(tau > INT_MIN, cnt_ge, 0)) > n_sel

    @pl.when(excess)
    def _():
        ltri = (lax.broadcasted_iota(I32, (QB, QB), 1) < srow).astype(BF16)

        def demote(kb, seen):
            blk = key_ref[pl.ds(kb * QB, QB), :]
            tie = blk == tau
            tief = jnp.where(tie, 1.0, 0.0)
            rank = _dot(ltri, tief.astype(BF16)) + seen
            key_ref[pl.ds(kb * QB, QB), :] = jnp.where(jnp.logical_and(tie, rank >= need),
                                                       jnp.int32(INT_MIN), blk)
            return seen + jnp.sum(tief, axis=0, keepdims=True)

        lax.fori_loop(0, nkb, demote, jnp.zeros((1, QB), F32))

    thr = jnp.maximum(tau, INT_MIN + 1)
    acc_ref[...] = jnp.zeros_like(acc_ref)
    m_ref[...] = jnp.full_like(m_ref, NEG_BIG)
    l_ref[...] = jnp.zeros_like(l_ref)

    def attend(kb, carry):
        sel = key_ref[pl.ds(kb * QB, QB), :] >= thr
        lg = _dot(kb_ref[0, pl.ds(kb * QB, QB), :], qbd_ref[...])
        ps = []
        for h in range(N_HEADS):
            lh = jnp.where(sel, lg[:, h * QB:(h + 1) * QB], NEG_BIG)
            m_old = m_ref[h:h + 1, :]
            m_new = jnp.maximum(m_old, jnp.max(lh, axis=0, keepdims=True))
            alpha = jnp.exp(m_old - m_new)
            p = jnp.exp(lh - m_new)
            l_ref[h:h + 1, :] = alpha * l_ref[h:h + 1, :] + jnp.sum(p, axis=0, keepdims=True)
            m_ref[h:h + 1, :] = m_new
            acc_ref[h * HEAD_DIM:(h + 1) * HEAD_DIM, :] = acc_ref[h * HEAD_DIM:(h + 1) * HEAD_DIM, :] * alpha
            ps.append(p.astype(BF16))
            vbd_ref[h * HEAD_DIM:(h + 1) * HEAD_DIM, h * QB:(h + 1) * QB] = (
                vt_ref[0, h * HEAD_DIM:(h + 1) * HEAD_DIM, pl.ds(kb * QB, QB)])
        acc_ref[...] += _dot(vbd_ref[...], jnp.concatenate(ps, axis=0))
        return carry

    lax.fori_loop(0, nkb, attend, 0)

    outs = [acc_ref[h * HEAD_DIM:(h + 1) * HEAD_DIM, :] / l_ref[h:h + 1, :] for h in range(N_HEADS)]
    o_ref[...] = jnp.concatenate(outs, axis=0).T


def _dsa_prompt(q, qib, wit, kb, vt, kib, n_seq, t_len):
    n_sel = min(TOPK_MAX, t_len // 4)
    nblk = t_len // QB
    return pl.pallas_call(
        functools.partial(_dsa_prompt_kernel, n_sel),
        grid=(n_seq, nblk),
        in_specs=[pl.BlockSpec((QB, BRANCH_W), lambda b, j: (b * nblk + j, 0)),
                  pl.BlockSpec((QB, BRANCH_W), lambda b, j: (b * nblk + j, 0)),
                  pl.BlockSpec((1, 8, QB), lambda b, j: (b, 0, j)),
                  pl.BlockSpec((1, t_len, BRANCH_W), lambda b, j: (b, 0, 0)),
                  pl.BlockSpec((1, BRANCH_W, t_len), lambda b, j: (b, 0, 0)),
                  pl.BlockSpec((1, t_len, LANES), lambda b, j: (b, 0, 0))],
        out_specs=pl.BlockSpec((QB, BRANCH_W), lambda b, j: (b * nblk + j, 0)),
        out_shape=jax.ShapeDtypeStruct((n_seq * t_len, BRANCH_W), F32),
        scratch_shapes=[pltpu.VMEM((t_len, QB), I32),
                        pltpu.VMEM((BRANCH_W, N_HEADS * QB), BF16),
                        pltpu.VMEM((BRANCH_W, N_HEADS * QB), BF16),
                        pltpu.VMEM((BRANCH_W, QB), F32),
                        pltpu.VMEM((8, QB), F32),
                        pltpu.VMEM((8, QB), F32)],
        compiler_params=_cparams(2),
        name="dsa_prompt",
    )(q, qib, wit, kb.reshape(n_seq, t_len, BRANCH_W), vt, kib.reshape(n_seq, t_len, LANES))


RW_CHUNK = 128
RW_GROUP = 8


def _col_bcast(xt, i):
    lane = lax.broadcasted_iota(I32, (HEAD_DIM, LANES), 1)
    halves = []
    for pair in range(2):
        r0 = 2 * pair * HEAD_DIM
        c0 = jnp.broadcast_to(xt[r0:r0 + HEAD_DIM, i:i + 1], (HEAD_DIM, LANES))
        c1 = jnp.broadcast_to(xt[r0 + HEAD_DIM:r0 + 2 * HEAD_DIM, i:i + 1], (HEAD_DIM, LANES))
        halves.append(jnp.where(lane < HEAD_DIM, c0, c1))
    return jnp.concatenate(halves, axis=1)


def _group_cols(xt_ref, n_arr, gi):
    sh = lax.rem(LANES - gi * RW_GROUP, LANES)
    return [pltpu.roll(xt_ref[a], sh, 1) for a in range(n_arr)]


def _rwkv_prep(p, prev, mu, w0, w_wa, a0, w_g2, k_k, k_a, r_k, ones_bd):
    xl = p + (prev - p) * mu
    r, k, v = xl[:, 0:256], xl[:, 256:512], xl[:, 512:768]
    wa = xl[:, 768:896]
    lane = lax.broadcasted_iota(I32, wa.shape, 1)
    wa = jnp.where(lane < RWKV_DECAY_LORA, jnp.tanh(wa), wa)
    z = _dot(wa.astype(BF16), w_wa)
    nz = -(w0 + z[:, 0:256])
    w_log = -(jnp.maximum(nz, 0.0) + jnp.log1p(jnp.exp(-jnp.abs(nz)))) - 0.5
    decay = jnp.exp(-jnp.exp(w_log))
    a = jax.nn.sigmoid(a0 + z[:, 256:512])
    g = _dot(jax.nn.sigmoid(xl[:, 896:1024]).astype(BF16), w_g2)
    kk = k * k_k
    kk = kk / jnp.sqrt(_seg_sum(kk * kk, ones_bd) + 1e-12)
    kp = k * (1.0 + (a - 1.0) * k_a)
    bonus = _seg_sum(r * kp * r_k, ones_bd) * v
    return r, decay, kp, v, kk, kk * a, g, bonus


def _rwkv_kernel(sequential, p_ref, prev_ref, st0_ref, mu_ref, w0_ref, wwa_ref, a0_ref, wg2_ref,
                 kk_ref, ka_ref, rk_ref, lnw_ref, lnb_ref, ones_ref,
                 o_ref, st_out_ref, carry_ref, st_ref, xt_ref, v_ref, y_ref):
    c = pl.program_id(1)
    p = p_ref[...]
    if sequential:
        @pl.when(c == 0)
        def _():
            carry_ref[...] = prev_ref[0]
            st_ref[...] = st0_ref[0]

        row = lax.broadcasted_iota(I32, p.shape, 0)
        prev = jnp.where(row == 0, carry_ref[...], pltpu.roll(p, 1, 0))
        carry_ref[...] = p[RW_CHUNK - 1:RW_CHUNK, :]
    else:
        prev = prev_ref[...]
    ones_bd = ones_ref[...]
    r, decay, kp, v, kk, kka, g, bonus = _rwkv_prep(
        p, prev, mu_ref[...], w0_ref[...], wwa_ref[...], a0_ref[...], wg2_ref[...],
        kk_ref[...], ka_ref[...], rk_ref[...], ones_bd)
    for a, arr in enumerate((r, decay, kp, kk, kka)):
        xt_ref[a] = arr.T
    v_ref[...] = v

    def group(gi, st):
        rt, wt, kt, kkt, kkat = _group_cols(xt_ref, 5, gi)
        base = pl.multiple_of(gi * RW_GROUP, RW_GROUP)
        vrows = v_ref[pl.ds(base, RW_GROUP), :]
        ys = []
        for i in range(RW_GROUP):
            if not sequential:
                st = st0_ref[base + i]
            sa = jnp.sum(st * _col_bcast(kkt, i), axis=0, keepdims=True)
            st = st * _col_bcast(wt, i) - _col_bcast(kkat, i) * sa + _col_bcast(kt, i) * vrows[i:i + 1, :]
            ys.append(jnp.sum(st * _col_bcast(rt, i), axis=0, keepdims=True))
            if not sequential:
                st_out_ref[base + i] = st
        y_ref[pl.ds(base, RW_GROUP), :] = jnp.concatenate(ys, axis=0)
        return st

    if sequential:
        st = lax.fori_loop(0, RW_CHUNK // RW_GROUP, group, st_ref[...])
        st_ref[...] = st

        @pl.when(c == pl.num_programs(1) - 1)
        def _():
            st_out_ref[0] = st
    else:
        lax.fori_loop(0, RW_CHUNK // RW_GROUP, group, jnp.zeros((HEAD_DIM, BRANCH_W), F32))

    o = _head_group_norm(y_ref[...], ones_bd, lnw_ref[...], lnb_ref[...], RWKV_LN_EPS)
    o_ref[...] = (o + bonus) * g


def _rwkv(p, prev, st0, lw, ones_bd, n_seq, t_len, sequential):
    n = n_seq * t_len
    nchunk = t_len // RW_CHUNK if sequential else 1
    full2 = lambda a, b: pl.BlockSpec((a, b), lambda s, c: (0, 0))
    if sequential:
        prev_spec = pl.BlockSpec((1, 1, RWKV_P), lambda s, c: (s, 0, 0))
        st_spec = pl.BlockSpec((1, HEAD_DIM, BRANCH_W), lambda s, c: (s, 0, 0))
        grid = (n_seq, nchunk)
    else:
        assert n == RW_CHUNK
        prev_spec = pl.BlockSpec((RW_CHUNK, RWKV_P), lambda s, c: (0, 0))
        st_spec = pl.BlockSpec((RW_CHUNK, HEAD_DIM, BRANCH_W), lambda s, c: (0, 0, 0))
        grid = (1, 1)
    return pl.pallas_call(
        functools.partial(_rwkv_kernel, sequential),
        grid=grid,
        in_specs=[pl.BlockSpec((RW_CHUNK, RWKV_P), lambda s, c: (s * nchunk + c, COL_RWKV // RWKV_P)),
                  prev_spec, st_spec,
                  full2(1, RWKV_P), full2(1, BRANCH_W), full2(LANES, 2 * BRANCH_W), full2(1, BRANCH_W),
                  full2(RWKV_G_LORA, BRANCH_W), full2(1, BRANCH_W), full2(1, BRANCH_W), full2(1, BRANCH_W),
                  full2(1, BRANCH_W), full2(1, BRANCH_W), full2(BRANCH_W, BRANCH_W)],
        out_specs=[pl.BlockSpec((RW_CHUNK, BRANCH_W), lambda s, c: (s * nchunk + c, 0)), st_spec],
        out_shape=[jax.ShapeDtypeStruct((n, BRANCH_W), F32),
                   jax.ShapeDtypeStruct(st0.shape, F32)],
        scratch_shapes=[pltpu.VMEM((1, RWKV_P), F32),
                        pltpu.VMEM((HEAD_DIM, BRANCH_W), F32),
                        pltpu.VMEM((5, BRANCH_W, RW_CHUNK), F32),
                        pltpu.VMEM((RW_CHUNK, BRANCH_W), F32),
                        pltpu.VMEM((RW_CHUNK, BRANCH_W), F32)],
        compiler_params=_cparams(2),
        name="rwkv7_prompt" if sequential else "rwkv7_sample",
    )(p, prev, st0, lw["mu"], lw["w0"], lw["w_wa"], lw["a0"], lw["w_g2"], lw["k_k"], lw["k_a"],
      lw["r_k"], lw["ln_w"], lw["ln_b"], ones_bd)


def _ret_prompt_kernel(p_ref, cos_ref, slo_ref, shi_ref, dmask_ref, qdec_ref, kdec_ref, sdec_ref,
                       gnw_ref, gnb_ref, ones_ref, o_ref, s_out_ref, s_ref):
    c = pl.program_id(1)

    @pl.when(c == 0)
    def _():
        s_ref[...] = jnp.zeros_like(s_ref)

    cos, slo, shi = cos_ref[...], slo_ref[...], shi_ref[...]
    half = HEAD_DIM // 2
    rq = _rope(p_ref[:, 0:256], cos, slo, shi, half)
    rk = _rope(p_ref[:, 256:512], cos, slo, shi, half) * RET_KSCALE
    rv = p_ref[:, 512:768]
    rg = p_ref[:, 768:1024]
    kdt = (rk * kdec_ref[...]).T.astype(BF16)
    qdec = qdec_ref[...]
    outs = []
    for h in range(N_HEADS):
        hs = slice(h * HEAD_DIM, (h + 1) * HEAD_DIM)
        qh, kh, vh = rq[:, hs].astype(BF16), rk[:, hs].astype(BF16), rv[:, hs].astype(BF16)
        att = _dot_nt(qh, kh) * dmask_ref[h]
        s_old = s_ref[hs, :]
        outs.append(_dot(att.astype(BF16), vh) + _dot(qh, s_old.astype(BF16)) * qdec[:, hs])
        s_ref[hs, :] = s_old * sdec_ref[hs, :] + _dot(kdt[hs, :], vh)
    o = jnp.concatenate(outs, axis=1)
    gn = _head_group_norm(o, ones_ref[...], gnw_ref[...], gnb_ref[...], GN_EPS)
    o_ref[...] = rg * jax.nn.sigmoid(rg) * gn

    @pl.when(c == pl.num_programs(1) - 1)
    def _():
        s_out_ref[0] = s_ref[...]


def _ret_prompt(p, tables, dec, gn_w, gn_b, ones_bd, n_seq, t_len):
    nchunk = t_len // RET_CHUNK
    cos, slo, shi = tables
    dmask, qdec, kdec, sdec = dec
    tab = pl.BlockSpec((RET_CHUNK, BRANCH_W), lambda s, c: (c, 0))
    full2 = lambda a, b: pl.BlockSpec((a, b), lambda s, c: (0, 0))
    return pl.pallas_call(
        _ret_prompt_kernel,
        grid=(n_seq, nchunk),
        in_specs=[pl.BlockSpec((RET_CHUNK, RET_P), lambda s, c: (s * nchunk + c, COL_RET // RET_P)),
                  tab, tab, tab,
                  pl.BlockSpec((N_HEADS, RET_CHUNK, RET_CHUNK), lambda s, c: (0, 0, 0)),
                  full2(RET_CHUNK, BRANCH_W), full2(RET_CHUNK, BRANCH_W), full2(BRANCH_W, HEAD_DIM),
                  full2(1, BRANCH_W), full2(1, BRANCH_W), full2(BRANCH_W, BRANCH_W)],
        out_specs=[pl.BlockSpec((RET_CHUNK, BRANCH_W), lambda s, c: (s * nchunk + c, 0)),
                   pl.BlockSpec((1, BRANCH_W, HEAD_DIM), lambda s, c: (s, 0, 0))],
        out_shape=[jax.ShapeDtypeStruct((n_seq * t_len, BRANCH_W), F32),
                   jax.ShapeDtypeStruct((n_seq, BRANCH_W, HEAD_DIM), F32)],
        scratch_shapes=[pltpu.VMEM((BRANCH_W, HEAD_DIM), F32)],
        compiler_params=_cparams(2),
        name="retention_prompt",
    )(p, cos, slo, shi, dmask, qdec, kdec, sdec, gn_w.reshape(1, BRANCH_W), gn_b.reshape(1, BRANCH_W),
      ones_bd)


def _ret_sample_kernel(p_ref, cos_ref, slo_ref, shi_ref, st0_ref, gamma_ref, gnw_ref, gnb_ref, ones_ref,
                       o_ref, st_out_ref, xt_ref, v_ref, y_ref):
    cos, slo, shi = cos_ref[...], slo_ref[...], shi_ref[...]
    half = HEAD_DIM // 2
    rq = _rope(p_ref[:, 0:256], cos, slo, shi, half)
    rk = _rope(p_ref[:, 256:512], cos, slo, shi, half) * RET_KSCALE
    rg = p_ref[:, 768:1024]
    xt_ref[0] = rq.T
    xt_ref[1] = rk.T
    v_ref[...] = p_ref[:, 512:768]
    gamma = gamma_ref[...]

    def group(gi, carry):
        qt, kt = _group_cols(xt_ref, 2, gi)
        base = pl.multiple_of(gi * RW_GROUP, RW_GROUP)
        vrows = v_ref[pl.ds(base, RW_GROUP), :]
        ys = []
        for i in range(RW_GROUP):
            st = st0_ref[base + i] * gamma + _col_bcast(kt, i) * vrows[i:i + 1, :]
            st_out_ref[base + i] = st
            ys.append(jnp.sum(st * _col_bcast(qt, i), axis=0, keepdims=True))
        y_ref[pl.ds(base, RW_GROUP), :] = jnp.concatenate(ys, axis=0)
        return carry

    lax.fori_loop(0, RW_CHUNK // RW_GROUP, group, 0)
    gn = _head_group_norm(y_ref[...], ones_ref[...], gnw_ref[...], gnb_ref[...], GN_EPS)
    o_ref[...] = rg * jax.nn.sigmoid(rg) * gn


def _ret_sample(p, tables, st0, gamma, gn_w, gn_b, ones_bd):
    n = st0.shape[0]
    assert n == RW_CHUNK
    cos, slo, shi = tables
    full2 = lambda a, b: pl.BlockSpec((a, b), lambda i: (0, 0))
    st_spec = pl.BlockSpec((n, HEAD_DIM, BRANCH_W), lambda i: (0, 0, 0))
    return pl.pallas_call(
        _ret_sample_kernel,
        grid=(1,),
        in_specs=[pl.BlockSpec((n, RET_P), lambda i: (0, COL_RET // RET_P)),
                  full2(n, BRANCH_W), full2(n, BRANCH_W), full2(n, BRANCH_W), st_spec,
                  full2(1, BRANCH_W), full2(1, BRANCH_W), full2(1, BRANCH_W), full2(BRANCH_W, BRANCH_W)],
        out_specs=[full2(n, BRANCH_W), st_spec],
        out_shape=[jax.ShapeDtypeStruct((n, BRANCH_W), F32), jax.ShapeDtypeStruct(st0.shape, F32)],
        scratch_shapes=[pltpu.VMEM((2, BRANCH_W, RW_CHUNK), F32),
                        pltpu.VMEM((RW_CHUNK, BRANCH_W), F32),
                        pltpu.VMEM((RW_CHUNK, BRANCH_W), F32)],
        compiler_params=_cparams(1),
        name="retention_sample",
    )(p, cos, slo, shi, st0, gamma, gn_w.reshape(1, BRANCH_W), gn_b.reshape(1, BRANCH_W), ones_bd)


POOL_HIST = 16


def _pool_windows(lane):
    win = jnp.full(lane.shape, float(POOL_WINDOWS[-1]), F32)
    for gi in range(len(POOL_WINDOWS) - 2, -1, -1):
        win = jnp.where(lane < (gi + 1) * HEAD_DIM, float(POOL_WINDOWS[gi]), win)
    return win


def _pool_select(sums, lane):
    out = sums[-1]
    for gi in range(len(POOL_WINDOWS) - 2, -1, -1):
        out = jnp.where(lane < (gi + 1) * HEAD_DIM, sums[gi], out)
    return out


def _pool_prompt_kernel(tm, u_ref, wp_ref, scale_ref, y_ref, nbuf_ref, ext_ref):
    t = pl.program_id(1)

    @pl.when(t == 0)
    def _():
        ext_ref[0:POOL_HIST, :] = jnp.zeros((POOL_HIST, BRANCH_W), F32)

    @pl.when(t > 0)
    def _():
        ext_ref[0:POOL_HIST, :] = ext_ref[tm:tm + POOL_HIST, :]

    u = u_ref[...]
    ext_ref[POOL_HIST:POOL_HIST + tm, :] = u
    sums, acc, i = [], u, 1
    for win in POOL_WINDOWS:
        while i < win:
            acc = acc + ext_ref[pl.ds(POOL_HIST - i, tm), :]
            i += 1
        sums.append(acc)
    lane = lax.broadcasted_iota(I32, u.shape, 1)
    pos = (lax.broadcasted_iota(I32, u.shape, 0) + t * tm).astype(F32)
    cnt = jnp.minimum(pos + 1.0, _pool_windows(lane))
    d = _pool_select(sums, lane) / cnt - u
    y_ref[...] = _dot(d.astype(BF16), wp_ref[...]) * scale_ref[...]

    @pl.when(t == pl.num_programs(1) - 1)
    def _():
        nbuf_ref[0] = ext_ref[tm:tm + POOL_HIST, :]


def _pool_prompt(p, wp_bd, scale, n_seq, t_len, tm):
    tpb = t_len // tm
    return pl.pallas_call(
        functools.partial(_pool_prompt_kernel, tm),
        grid=(n_seq, tpb),
        in_specs=[pl.BlockSpec((tm, BRANCH_W), lambda s, t: (s * tpb + t, COL_POOL // BRANCH_W)),
                  pl.BlockSpec((BRANCH_W, BRANCH_W), lambda s, t: (0, 0)),
                  pl.BlockSpec((1, BRANCH_W), lambda s, t: (0, 0))],
        out_specs=[pl.BlockSpec((tm, BRANCH_W), lambda s, t: (s * tpb + t, 0)),
                   pl.BlockSpec((1, POOL_HIST, BRANCH_W), lambda s, t: (s, 0, 0))],
        out_shape=[jax.ShapeDtypeStruct((n_seq * t_len, BRANCH_W), F32),
                   jax.ShapeDtypeStruct((n_seq, POOL_HIST, BRANCH_W), F32)],
        scratch_shapes=[pltpu.VMEM((tm + POOL_HIST, BRANCH_W), F32)],
        compiler_params=_cparams(2),
        name="pool_prompt",
    )(p, wp_bd, scale.reshape(1, BRANCH_W))


def _pool_sample_kernel(pos0, u_ref, buf_ref, wp_ref, scale_ref, y_ref, nbuf_ref):
    u = u_ref[...]
    sums, acc, i = [], u, 1
    for win in POOL_WINDOWS:
        while i < win:
            acc = acc + buf_ref[POOL_BUF - i]
            i += 1
        sums.append(acc)
    lane = lax.broadcasted_iota(I32, u.shape, 1)
    cnt = jnp.minimum(float(pos0) + 1.0, _pool_windows(lane))
    d = _pool_select(sums, lane) / cnt - u
    y_ref[...] = _dot(d.astype(BF16), wp_ref[...]) * scale_ref[...]
    for r in range(POOL_BUF - 1):
        nbuf_ref[r] = buf_ref[r + 1]
    nbuf_ref[POOL_BUF - 1] = u


def _pool_sample(p, buf_t, wp_bd, scale, pos0):
    n = buf_t.shape[1]
    full3 = pl.BlockSpec((POOL_BUF, n, BRANCH_W), lambda i: (0, 0, 0))
    return pl.pallas_call(
        functools.partial(_pool_sample_kernel, pos0),
        grid=(1,),
        in_specs=[pl.BlockSpec((n, BRANCH_W), lambda i: (0, COL_POOL // BRANCH_W)), full3,
                  pl.BlockSpec((BRANCH_W, BRANCH_W), lambda i: (0, 0)),
                  pl.BlockSpec((1, BRANCH_W), lambda i: (0, 0))],
        out_specs=[pl.BlockSpec((n, BRANCH_W), lambda i: (0, 0)), full3],
        out_shape=[jax.ShapeDtypeStruct((n, BRANCH_W), F32),
                   jax.ShapeDtypeStruct((POOL_BUF, n, BRANCH_W), F32)],
        compiler_params=_cparams(1),
        name="pool_sample",
    )(p, buf_t, wp_bd, scale.reshape(1, BRANCH_W))


def _page_copies(pt_ref, seq, slot, hbm_ref, buf_ref, sem_ref, n_pages):
    return [pltpu.make_async_copy(hbm_ref.at[pt_ref[seq, pg]], buf_ref.at[slot, pg], sem_ref.at[slot])
            for pg in range(n_pages)]


def _paged_prefetch(pt_ref, pairs, n_pages):
    b = pl.program_id(0)
    nb = pl.num_programs(0)
    slot = lax.rem(b, 2)

    @pl.when(b == 0)
    def _():
        for hbm_ref, buf_ref, sem_ref in pairs:
            for cp in _page_copies(pt_ref, 0, 0, hbm_ref, buf_ref, sem_ref, n_pages):
                cp.start()

    @pl.when(b + 1 < nb)
    def _():
        for hbm_ref, buf_ref, sem_ref in pairs:
            for cp in _page_copies(pt_ref, b + 1, 1 - slot, hbm_ref, buf_ref, sem_ref, n_pages):
                cp.start()

    for hbm_ref, buf_ref, sem_ref in pairs:
        for cp in _page_copies(pt_ref, b, slot, hbm_ref, buf_ref, sem_ref, n_pages):
            cp.wait()
    return slot


def _dsa_sample_score_kernel(n_pages, pt_ref, qi_ref, wi_ref, cki_ref, out_ref, buf_ref, sem_ref):
    slot = _paged_prefetch(pt_ref, [(cki_ref, buf_ref, sem_ref)], n_pages)
    keys = buf_ref[slot]
    keys = keys.reshape(keys.shape[0] * keys.shape[1], HEAD_DIM).astype(BF16)
    s = jnp.maximum(_dot_nt(qi_ref[0], keys), 0.0)
    out_ref[0] = jnp.sum(s * wi_ref[0], axis=0, keepdims=True)


def _dsa_sample_scores(page_table, qi4, wi4, cache_kidx):
    nb, n_pages = page_table.shape
    page = cache_kidx.shape[1]
    past = n_pages * page
    return pl.pallas_call(
        functools.partial(_dsa_sample_score_kernel, n_pages),
        grid_spec=pltpu.PrefetchScalarGridSpec(
            num_scalar_prefetch=1,
            grid=(nb,),
            in_specs=[pl.BlockSpec((1, N_HEADS, HEAD_DIM), lambda b, pt: (b, 0, 0)),
                      pl.BlockSpec((1, N_HEADS, 1), lambda b, pt: (b, 0, 0)),
                      pl.BlockSpec(memory_space=pl.ANY)],
            out_specs=pl.BlockSpec((1, 1, past), lambda b, pt: (b, 0, 0)),
            scratch_shapes=[pltpu.VMEM((2, n_pages, page, HEAD_DIM), F32),
                            pltpu.SemaphoreType.DMA((2,))]),
        out_shape=jax.ShapeDtypeStruct((nb, 1, past), F32),
        compiler_params=_cparams(1),
        name="dsa_sample_scores",
    )(page_table, qi4, wi4, cache_kidx)


def _dsa_sample_select_kernel(n_sel, sc_ref, qib_ref, kib_ref, wi_ref, sel_ref, selself_ref):
    nb, past = sc_ref.shape
    qi = qib_ref[...].astype(F32)
    ki = kib_ref[...].astype(F32)[:, :HEAD_DIM]
    wi = wi_ref[...]
    s_self = jnp.zeros((nb, 1), F32)
    for h in range(N_HEADS):
        dot_h = jnp.sum(qi[:, h * HEAD_DIM:(h + 1) * HEAD_DIM] * ki, axis=-1, keepdims=True)
        s_self = s_self + jnp.maximum(dot_h, 0.0) * wi[:, h:h + 1]
    key = _float_key(sc_ref[...] + 0.0)
    key_self = _float_key(s_self)

    def count(pred):
        return (jnp.sum(pred(key).astype(I32), axis=-1, keepdims=True) + pred(key_self).astype(I32))

    def bit_step(i, tau):
        cand = tau + (jnp.int32(1) << (31 - i))
        return jnp.where(count(lambda x: x >= cand) >= n_sel, cand, tau)

    tau = lax.fori_loop(0, 32, bit_step, jnp.full((nb, 1), INT_MIN, I32))
    gt = key > tau
    tie = key == tau
    need = (n_sel - count(lambda x: x > tau)).astype(F32)
    utri = (lax.broadcasted_iota(I32, (LANES, LANES), 0)
            < lax.broadcasted_iota(I32, (LANES, LANES), 1)).astype(BF16)
    seen = jnp.zeros((nb, 1), F32)
    for c in range(past // LANES):
        cs = slice(c * LANES, (c + 1) * LANES)
        tief = jnp.where(tie[:, cs], 1.0, 0.0)
        rank = _dot(tief.astype(BF16), utri) + seen
        keep = jnp.logical_or(gt[:, cs], jnp.logical_and(tie[:, cs], rank < need))
        sel_ref[:, cs] = jnp.where(keep, 1.0, 0.0)
        seen = seen + jnp.sum(tief, axis=-1, keepdims=True)
    keep_self = jnp.logical_or(key_self > tau, jnp.logical_and(key_self == tau, seen < need))
    selself_ref[...] = jnp.where(keep_self, 1.0, 0.0)


def _dsa_sample_select(scores, qib, kib, wi, n_sel):
    nb, past = scores.shape
    full2 = lambda a, b: pl.BlockSpec((a, b), lambda i: (0, 0))
    return pl.pallas_call(
        functools.partial(_dsa_sample_select_kernel, n_sel),
        grid=(1,),
        in_specs=[full2(nb, past), full2(nb, BRANCH_W), full2(nb, LANES), full2(nb, 8)],
        out_specs=[full2(nb, past), full2(nb, 1)],
        out_shape=[jax.ShapeDtypeStruct((nb, past), F32), jax.ShapeDtypeStruct((nb, 1), F32)],
        compiler_params=_cparams(1),
        name="dsa_sample_select",
    )(scores, qib, kib, wi)


def _dsa_sample_attend_kernel(n_pages, pt_ref, q_ref, sel_ref, selself_ref, kself_ref, vself_ref,
                              ck_ref, cv_ref, o_ref, kbuf_ref, vbuf_ref, ksem_ref, vsem_ref):
    slot = _paged_prefetch(pt_ref, [(ck_ref, kbuf_ref, ksem_ref), (cv_ref, vbuf_ref, vsem_ref)], n_pages)
    past = n_pages * kbuf_ref.shape[2]
    keys = kbuf_ref[slot].reshape(past, BRANCH_W).astype(BF16)
    vals = vbuf_ref[slot].reshape(past, BRANCH_W).astype(BF16)
    head_of_lane = lax.broadcasted_iota(I32, (N_HEADS, BRANCH_W), 1) // HEAD_DIM
    own = head_of_lane == lax.broadcasted_iota(I32, (N_HEADS, BRANCH_W), 0)
    qrows = jnp.where(own, jnp.broadcast_to(q_ref[0], (N_HEADS, BRANCH_W)), 0.0).astype(BF16)
    lg = jnp.where(sel_ref[0] > 0.0, _dot_nt(qrows, keys), NEG_BIG)
    kself = kself_ref[0].astype(BF16).astype(F32)
    lself = jnp.sum(qrows.astype(F32) * kself, axis=-1, keepdims=True)
    lself = jnp.where(selself_ref[0] > 0.0, lself, NEG_BIG)
    m = jnp.maximum(jnp.max(lg, axis=-1, keepdims=True), lself)
    p = jnp.exp(lg - m)
    pself = jnp.exp(lself - m)
    denom = jnp.sum(p, axis=-1, keepdims=True) + pself
    o4 = (_dot(p.astype(BF16), vals) + pself * vself_ref[0]) / denom
    o_ref[0] = jnp.sum(jnp.where(own, o4, 0.0), axis=0, keepdims=True)


def _dsa_sample_attend(page_table, q, sel, selself, kself, vself, cache_k, cache_v):
    nb, n_pages = page_table.shape
    n_pool, page = cache_k.shape[0], cache_k.shape[1]
    past = n_pages * page
    row3 = lambda w: pl.BlockSpec((1, 1, w), lambda b, pt: (b, 0, 0))
    return pl.pallas_call(
        functools.partial(_dsa_sample_attend_kernel, n_pages),
        grid_spec=pltpu.PrefetchScalarGridSpec(
            num_scalar_prefetch=1,
            grid=(nb,),
            in_specs=[row3(BRANCH_W), row3(past), row3(1), row3(BRANCH_W), row3(BRANCH_W),
                      pl.BlockSpec(memory_space=pl.ANY), pl.BlockSpec(memory_space=pl.ANY)],
            out_specs=row3(BRANCH_W),
            scratch_shapes=[pltpu.VMEM((2, n_pages, page, BRANCH_W), F32),
                            pltpu.VMEM((2, n_pages, page, BRANCH_W), F32),
                            pltpu.SemaphoreType.DMA((2,)),
                            pltpu.SemaphoreType.DMA((2,))]),
        out_shape=jax.ShapeDtypeStruct((nb, 1, BRANCH_W), F32),
        compiler_params=_cparams(1),
        name="dsa_sample_attend",
    )(page_table, q.reshape(nb, 1, BRANCH_W), sel.reshape(nb, 1, past), selself.reshape(nb, 1, 1),
      kself.reshape(nb, 1, BRANCH_W), vself.reshape(nb, 1, BRANCH_W),
      cache_k.reshape(n_pool, page, BRANCH_W), cache_v.reshape(n_pool, page, BRANCH_W))


def _rope_tables(pos, rot_dim, theta):
    half = rot_dim // 2
    inv = theta ** (-jnp.arange(half, dtype=F32) / half)
    ang = pos.astype(F32)[:, None] * inv[None, :]
    cos, sin = jnp.cos(ang), jnp.sin(ang)
    n = pos.shape[0]
    zh = jnp.zeros((n, half), F32)
    zr = jnp.zeros((n, HEAD_DIM - rot_dim), F32)
    c = jnp.concatenate([cos, cos, jnp.ones((n, HEAD_DIM - rot_dim), F32)], axis=1)
    lo = jnp.concatenate([-sin, zh, zr], axis=1)
    hi = jnp.concatenate([zh, sin, zr], axis=1)
    return tuple(jnp.tile(t, (1, N_HEADS)) for t in (c, lo, hi))


def _ret_decay(chunk):
    log_g = jnp.log1p(-jnp.exp2(-5.0 - jnp.arange(N_HEADS, dtype=F32)))
    n = jnp.arange(chunk, dtype=F32)
    diff = n[:, None] - n[None, :]
    dmask = jnp.where(diff >= 0, jnp.exp(log_g[:, None, None] * jnp.maximum(diff, 0.0)), 0.0)
    q_dec = jnp.exp(log_g[:, None] * (n + 1.0)).T
    k_dec = jnp.exp(log_g[:, None] * (chunk - 1.0 - n))
    s_dec = jnp.exp(log_g * chunk)
    qdec = jnp.repeat(q_dec, HEAD_DIM, axis=1)
    kdec = jnp.repeat(k_dec.T, HEAD_DIM, axis=1)
    sdec_rows = jnp.repeat(s_dec, HEAD_DIM)
    return dmask, qdec, kdec, sdec_rows


def _block_diag(blocks):
    n = len(blocks)
    rows = []
    for i, blk in enumerate(blocks):
        rows.append(jnp.concatenate([blk if j == i else jnp.zeros((blk.shape[0], blocks[j].shape[1]), blk.dtype)
                                     for j in range(n)], axis=1))
    return jnp.concatenate(rows, axis=0)


def kernel(x_prompt, x_sample, cache_k, cache_v, cache_kidx, state_wkv, state_shift, state_ret, state_pool, page_table, norm1_w, w_in, rwkv_mu, rwkv_w0, rwkv_w_w2, rwkv_a0, rwkv_w_a2, rwkv_w_g2, rwkv_k_k, rwkv_k_a, rwkv_r_k, rwkv_ln_w, rwkv_ln_b, q_norm_w, k_norm_w, idx_k_norm_w, ret_gn_w, ret_gn_b, pool_w, pool_scale, w_branch, w_out, norm2_w, w_ffn_in, w_ffn_out):
    n_seq, t_len, _ = x_prompt.shape
    nb = x_sample.shape[0]
    assert x_sample.shape[1] == 1
    page = cache_k.shape[2]
    past = page_table.shape[1] * page
    n_sel_sample = min(TOPK_MAX, (past + 1) // 4)
    tm_p = min(512, t_len)
    tm_m = min(256, t_len)

    ones_bd = _block_diag([jnp.ones((HEAD_DIM, HEAD_DIM), BF16)] * N_HEADS)
    pos_p = jnp.arange(t_len)
    pos_s = jnp.full((nb,), past)
    att_tab_p, att_tab_s = _rope_tables(pos_p, ROT_DIM, ROPE_THETA), _rope_tables(pos_s, ROT_DIM, ROPE_THETA)
    ret_tab_p, ret_tab_s = _rope_tables(pos_p, HEAD_DIM, RET_THETA), _rope_tables(pos_s, HEAD_DIM, RET_THETA)
    dmask, qdec, kdec, sdec_rows = _ret_decay(math.gcd(t_len, RET_CHUNK))
    dec_p = (dmask, qdec, kdec, jnp.broadcast_to(sdec_rows[:, None], (BRANCH_W, HEAD_DIM)))
    gamma_s = _ret_decay(1)[3].reshape(1, BRANCH_W)

    xp = x_prompt.reshape(n_seq * t_len, D_MODEL)
    xs = x_sample.reshape(nb, D_MODEL)
    zero_shift = jnp.zeros((n_seq, 1, RWKV_P), F32)
    zero_wkv = jnp.zeros((n_seq, HEAD_DIM, BRANCH_W), F32)
    prompt_new = [[] for _ in range(7)]
    sample_new = [[] for _ in range(7)]
    o_att, o_ret, o_pool = RWKV_P, RWKV_P + ATT_P, RWKV_P + ATT_P + RET_P
    o_gate = o_pool + BRANCH_W
    row1 = lambda a: a.reshape(1, -1)

    for l in range(DEPTH):
        w = w_in[l]
        w_pad = jnp.concatenate(
            [w[:, o_gate:], w[:, :RWKV_P], w[:, o_ret:o_pool], w[:, o_pool:o_gate], w[:, o_att:o_ret],
             jnp.zeros((D_MODEL, ATT_P_PAD - ATT_P), F32)], axis=1).astype(BF16)
        lw = dict(mu=row1(rwkv_mu[l]), w0=row1(rwkv_w0[l]),
                  w_wa=_block_diag([rwkv_w_w2[l], rwkv_w_a2[l]]).astype(BF16),
                  a0=row1(rwkv_a0[l]), w_g2=rwkv_w_g2[l].astype(BF16), k_k=row1(rwkv_k_k[l]),
                  k_a=row1(rwkv_k_a[l]), r_k=row1(rwkv_r_k[l]), ln_w=row1(rwkv_ln_w[l]), ln_b=row1(rwkv_ln_b[l]))
        wp_bd = _block_diag([pool_w[l, g] for g in range(N_HEADS)]).astype(BF16)
        wb = w_branch[l].astype(BF16)
        wo = w_out[l].astype(BF16)
        wfi = w_ffn_in[l].astype(BF16)
        wfo = w_ffn_out[l].astype(BF16)

        p = _norm_matmul(xp, norm1_w[l], w_pad, tm_p, P_PAD // 5)
        o_a, wkv = _rwkv(p, zero_shift, zero_wkv, lw, ones_bd, n_seq, t_len, True)
        q, k, v, ki, kb, vt, qib, kib, wit = _att_prep(p, att_tab_p, q_norm_w[l], k_norm_w[l], idx_k_norm_w[l],
                                                     ones_bd, n_seq, t_len, tm_p)
        o_b = _dsa_prompt(q, qib, wit, kb, vt, kib, n_seq, t_len)
        o_c, ret = _ret_prompt(p, ret_tab_p, dec_p, ret_gn_w[l], ret_gn_b[l], ones_bd, n_seq, t_len)
        o_d, pbuf = _pool_prompt(p, wp_bd, pool_scale[l], n_seq, t_len, tm_p)
        x1 = _merge(xp, (o_a, o_b, o_c, o_d), p, wb, wo, tm_m)
        xp = _ffn(x1, norm2_w[l], wfi, wfo, tm_m)
        st = (k.reshape(n_seq, t_len, N_HEADS, HEAD_DIM), v.reshape(n_seq, t_len, N_HEADS, HEAD_DIM),
              ki[:, :HEAD_DIM].reshape(n_seq, t_len, HEAD_DIM),
              wkv.reshape(n_seq, HEAD_DIM, N_HEADS, HEAD_DIM).transpose(0, 2, 3, 1),
              p.reshape(n_seq, t_len, P_PAD)[:, -1, COL_RWKV:COL_RWKV + RWKV_P],
              ret.reshape(n_seq, N_HEADS, HEAD_DIM, HEAD_DIM),
              pbuf[:, POOL_HIST - POOL_BUF:, :])
        for lst, s in zip(prompt_new, st):
            lst.append(s)

        ps = _norm_matmul(xs, norm1_w[l], w_pad, nb, P_PAD // 5)
        wkv0 = state_wkv[l].transpose(0, 3, 1, 2).reshape(nb, HEAD_DIM, BRANCH_W)
        o_a, wkv = _rwkv(ps, state_shift[l], wkv0, lw, ones_bd, 1, nb, False)
        q, k, v, ki, kb, vt, qib, kib, wit = _att_prep(ps, att_tab_s, q_norm_w[l], k_norm_w[l], idx_k_norm_w[l],
                                                     ones_bd, 1, nb, nb)
        wi_rows = wit[0].T
        scores = _dsa_sample_scores(page_table, qib.reshape(nb, N_HEADS, HEAD_DIM),
                                    wi_rows[:, :N_HEADS].reshape(nb, N_HEADS, 1), cache_kidx[l])
        sel, selself = _dsa_sample_select(scores.reshape(nb, past), qib, kib, wi_rows, n_sel_sample)
        o_b = _dsa_sample_attend(page_table, q, sel, selself, k, v, cache_k[l], cache_v[l]).reshape(nb, BRANCH_W)
        ret0 = state_ret[l].transpose(0, 2, 1, 3).reshape(nb, HEAD_DIM, BRANCH_W)
        o_c, ret = _ret_sample(ps, ret_tab_s, ret0, gamma_s, ret_gn_w[l], ret_gn_b[l], ones_bd)
        o_d, pbuf = _pool_sample(ps, state_pool[l].transpose(1, 0, 2), wp_bd, pool_scale[l], past)
        x1 = _merge(xs, (o_a, o_b, o_c, o_d), ps, wb, wo, nb)
        xs = _ffn(x1, norm2_w[l], wfi, wfo, nb)
        st = (k.reshape(nb, 1, N_HEADS, HEAD_DIM), v.reshape(nb, 1, N_HEADS, HEAD_DIM),
              ki[:, :HEAD_DIM].reshape(nb, 1, HEAD_DIM),
              wkv.reshape(nb, HEAD_DIM, N_HEADS, HEAD_DIM).transpose(0, 2, 3, 1),
              ps[:, COL_RWKV:COL_RWKV + RWKV_P],
              ret.reshape(nb, HEAD_DIM, N_HEADS, HEAD_DIM).transpose(0, 2, 1, 3),
              pbuf.transpose(1, 0, 2))
        for lst, s in zip(sample_new, st):
            lst.append(s)

    outs_p = [jnp.stack(a) for a in prompt_new]
    outs_s = [jnp.stack(a) for a in sample_new]
    return (xp.reshape(n_seq, t_len, D_MODEL), xs.reshape(nb, 1, D_MODEL), *outs_p, *outs_s)
```

```python
import functools
import math

import jax
import jax.numpy as jnp
from jax import lax
from jax.experimental import pallas as pl
from jax.experimental.pallas import tpu as pltpu

F32 = jnp.float32
BF16 = jnp.bfloat16
I32 = jnp.int32

D_MODEL = 1024
DEPTH = 4
HEAD_DIM = 64
N_HEADS = 4
BRANCH_W = N_HEADS * HEAD_DIM
N_BRANCH = 4
RWKV_DECAY_LORA = 64
RWKV_A_LORA = 64
RWKV_G_LORA = 128
RWKV_P = 3 * BRANCH_W + RWKV_DECAY_LORA + RWKV_A_LORA + RWKV_G_LORA
RWKV_LN_EPS = 64e-5
ATT_P = 3 * BRANCH_W + N_HEADS * HEAD_DIM + HEAD_DIM + N_HEADS
ATT_P_PAD = 1280
TOPK_MAX = 256
ROPE_THETA = 500000.0
ROT_DIM = HEAD_DIM // 4
ATT_SCALE = HEAD_DIM ** -0.5
IDX_SCALE = (N_HEADS * HEAD_DIM) ** -0.5
RET_P = 4 * BRANCH_W
RET_CHUNK = 128
RET_THETA = 10000.0
RET_KSCALE = HEAD_DIM ** -0.5
POOL_WINDOWS = (2, 4, 8, 16)
POOL_BUF = 15
GATE_P = N_BRANCH * D_MODEL
D_FF = -(-8 * D_MODEL // 768) * 256
NORM_EPS = 1e-6
GN_EPS = 1e-5

COL_GATE = 0
COL_RWKV = GATE_P
COL_RET = COL_RWKV + RWKV_P
COL_POOL = COL_RET + RET_P
COL_ATT = COL_POOL + BRANCH_W
P_PAD = COL_ATT + ATT_P_PAD

LANES = 128
INT_MIN = -2 ** 31
NEG_BIG = -1e30
VMEM_LIMIT = 56 * 1024 * 1024


def _cparams(n_axes):
    return pltpu.CompilerParams(dimension_semantics=("arbitrary",) * n_axes,
                                vmem_limit_bytes=VMEM_LIMIT)


def _dot(a, b):
    return jnp.dot(a, b, preferred_element_type=F32)


def _dot_nt(a, b):
    return lax.dot_general(a, b, (((1,), (1,)), ((), ())), preferred_element_type=F32)


def _seg_sum(x, ones_bd):
    hi = x.astype(BF16)
    lo = (x - hi.astype(F32)).astype(BF16)
    return _dot(hi, ones_bd) + _dot(lo, ones_bd)


def _head_group_norm(x, ones_bd, w, b, eps):
    mean = _seg_sum(x, ones_bd) * (1.0 / HEAD_DIM)
    xc = x - mean
    var = _seg_sum(xc * xc, ones_bd) * (1.0 / HEAD_DIM)
    return xc * lax.rsqrt(var + eps) * w + b


def _rope(x, cos, sin_lo, sin_hi, half):
    n = x.shape[-1]
    return x * cos + pltpu.roll(x, n - half, 1) * sin_lo + pltpu.roll(x, half, 1) * sin_hi


def _norm_matmul_kernel(x_ref, nw_ref, w_ref, o_ref, h_ref):
    @pl.when(pl.program_id(1) == 0)
    def _():
        x = x_ref[...]
        y = x * lax.rsqrt(jnp.mean(x * x, axis=-1, keepdims=True) + NORM_EPS)
        h_ref[...] = (y * nw_ref[...]).astype(BF16)

    o_ref[...] = _dot(h_ref[...], w_ref[...])


def _norm_matmul(x, norm_w, w_bf16, tm, tn):
    n, d = x.shape
    p = w_bf16.shape[1]
    return pl.pallas_call(
        _norm_matmul_kernel,
        grid=(n // tm, p // tn),
        in_specs=[pl.BlockSpec((tm, d), lambda i, j: (i, 0)),
                  pl.BlockSpec((1, d), lambda i, j: (0, 0)),
                  pl.BlockSpec((d, tn), lambda i, j: (0, j))],
        out_specs=pl.BlockSpec((tm, tn), lambda i, j: (i, j)),
        out_shape=jax.ShapeDtypeStruct((n, p), F32),
        scratch_shapes=[pltpu.VMEM((tm, d), BF16)],
        compiler_params=_cparams(2),
        name="norm_in_proj",
    )(x, norm_w.reshape(1, d), w_bf16)


def _merge_kernel(x_ref, oa_ref, ob_ref, oc_ref, od_ref, gate_ref, wb_ref, wo_ref, out_ref):
    acc = None
    for b, o_ref in enumerate((oa_ref, ob_ref, oc_ref, od_ref)):
        up = _dot(o_ref[...].astype(BF16), wb_ref[b])
        g = jax.nn.sigmoid(gate_ref[:, b * D_MODEL:(b + 1) * D_MODEL])
        acc = g * up if acc is None else acc + g * up
    out_ref[...] = x_ref[...] + _dot(acc.astype(BF16), wo_ref[...])


def _merge(x, branches, p, w_branch_bf16, w_out_bf16, tm):
    n = x.shape[0]
    row = lambda w: pl.BlockSpec((tm, w), lambda i: (i, 0))
    return pl.pallas_call(
        _merge_kernel,
        grid=(n // tm,),
        in_specs=[row(D_MODEL), row(BRANCH_W), row(BRANCH_W), row(BRANCH_W), row(BRANCH_W),
                  pl.BlockSpec((tm, GATE_P), lambda i: (i, COL_GATE // GATE_P)),
                  pl.BlockSpec((N_BRANCH, BRANCH_W, D_MODEL), lambda i: (0, 0, 0)),
                  pl.BlockSpec((D_MODEL, D_MODEL), lambda i: (0, 0))],
        out_specs=row(D_MODEL),
        out_shape=jax.ShapeDtypeStruct((n, D_MODEL), F32),
        compiler_params=_cparams(1),
        name="gated_merge_out_proj",
    )(x, *branches, p, w_branch_bf16, w_out_bf16)


FF_CHUNK = D_FF // 2


def _ffn_kernel(x_ref, nw_ref, wi_ref, wo_ref, out_ref):
    x = x_ref[...]
    h = (x * lax.rsqrt(jnp.mean(x * x, axis=-1, keepdims=True) + NORM_EPS) * nw_ref[...]).astype(BF16)
    acc = x
    for c in range(D_FF // FF_CHUNK):
        g = _dot(h, wi_ref[:, c * FF_CHUNK:(c + 1) * FF_CHUNK])
        u = _dot(h, wi_ref[:, D_FF + c * FF_CHUNK:D_FF + (c + 1) * FF_CHUNK])
        a = (g * jax.nn.sigmoid(g) * u).astype(BF16)
        acc = acc + _dot(a, wo_ref[c * FF_CHUNK:(c + 1) * FF_CHUNK, :])
    out_ref[...] = acc


def _ffn(x, norm_w, w_in_bf16, w_out_bf16, tm):
    n = x.shape[0]
    return pl.pallas_call(
        _ffn_kernel,
        grid=(n // tm,),
        in_specs=[pl.BlockSpec((tm, D_MODEL), lambda i: (i, 0)),
                  pl.BlockSpec((1, D_MODEL), lambda i: (0, 0)),
                  pl.BlockSpec((D_MODEL, 2 * D_FF), lambda i: (0, 0)),
                  pl.BlockSpec((D_FF, D_MODEL), lambda i: (0, 0))],
        out_specs=pl.BlockSpec((tm, D_MODEL), lambda i: (i, 0)),
        out_shape=jax.ShapeDtypeStruct((n, D_MODEL), F32),
        compiler_params=_cparams(1),
        name="swiglu_ffn",
    )(x, norm_w.reshape(1, D_MODEL), w_in_bf16, w_out_bf16)


def _att_prep_kernel(p_ref, cos_ref, slo_ref, shi_ref, qnw_ref, knw_ref, inw_ref, ones_ref,
                     q_ref, k_ref, v_ref, ki_ref, kb_ref, vt_ref, qib_ref, kib_ref, wit_ref):
    cos, slo, shi = cos_ref[...], slo_ref[...], shi_ref[...]
    ones_bd = ones_ref[...]
    half = ROT_DIM // 2

    def head_rms(x, w):
        ms = _seg_sum(x * x, ones_bd) * (1.0 / HEAD_DIM)
        return x * lax.rsqrt(ms + NORM_EPS) * w

    q = _rope(head_rms(p_ref[:, 0:256], qnw_ref[...]), cos, slo, shi, half)
    k = _rope(head_rms(p_ref[:, 256:512], knw_ref[...]), cos, slo, shi, half)
    v = p_ref[:, 512:768]
    qi = _rope(p_ref[:, 768:1024], cos, slo, shi, half)
    tail = p_ref[:, 1024:1152]
    lane = lax.broadcasted_iota(I32, tail.shape, 1)
    is_ki = lane < HEAD_DIM
    kiraw = jnp.where(is_ki, tail, 0.0)
    ms = jnp.sum(kiraw * kiraw, axis=-1, keepdims=True) * (1.0 / HEAD_DIM)
    kin = kiraw * lax.rsqrt(ms + NORM_EPS) * inw_ref[...]
    ki = _rope(kin, cos[:, :LANES], slo[:, :LANES], shi[:, :LANES], half)
    ki = jnp.where(is_ki, ki, 0.0)

    q_ref[...] = q * ATT_SCALE
    k_ref[...] = k
    v_ref[...] = v
    ki_ref[...] = ki
    kb_ref[...] = k.astype(BF16)
    vt_ref[0] = v.T.astype(BF16)
    qib_ref[...] = qi.astype(BF16)
    kib_ref[...] = ki.astype(BF16)
    wi = jnp.where(is_ki, 0.0, tail) * IDX_SCALE
    wit_ref[0] = pltpu.roll(wi, LANES - HEAD_DIM, 1).T[0:8, :]


def _att_prep(p, tables, q_norm_w, k_norm_w, idx_k_norm_w, ones_bd, n_seq, t_len, tm):
    n = p.shape[0]
    tpb = t_len // tm
    cos, slo, shi = tables
    tile4 = lambda w: jnp.tile(w.reshape(1, HEAD_DIM), (1, N_HEADS))
    inw = jnp.concatenate([idx_k_norm_w.reshape(1, HEAD_DIM), jnp.ones((1, LANES - HEAD_DIM), F32)], axis=1)
    row = lambda w: pl.BlockSpec((tm, w), lambda i: (i, 0))
    tab = pl.BlockSpec((tm, BRANCH_W), lambda i: (i % tpb, 0))
    full = lambda a, b: pl.BlockSpec((a, b), lambda i: (0, 0))
    return pl.pallas_call(
        _att_prep_kernel,
        grid=(n // tm,),
        in_specs=[pl.BlockSpec((tm, ATT_P_PAD), lambda i: (i, COL_ATT // ATT_P_PAD)),
                  tab, tab, tab, full(1, BRANCH_W), full(1, BRANCH_W), full(1, LANES),
                  full(BRANCH_W, BRANCH_W)],
        out_specs=[row(BRANCH_W), row(BRANCH_W), row(BRANCH_W), row(LANES), row(BRANCH_W),
                   pl.BlockSpec((1, BRANCH_W, tm), lambda i: (i // tpb, 0, i % tpb)),
                   row(BRANCH_W), row(LANES),
                   pl.BlockSpec((1, 8, tm), lambda i: (i // tpb, 0, i % tpb))],
        out_shape=[jax.ShapeDtypeStruct((n, BRANCH_W), F32),
                   jax.ShapeDtypeStruct((n, BRANCH_W), F32),
                   jax.ShapeDtypeStruct((n, BRANCH_W), F32),
                   jax.ShapeDtypeStruct((n, LANES), F32),
                   jax.ShapeDtypeStruct((n, BRANCH_W), BF16),
                   jax.ShapeDtypeStruct((n_seq, BRANCH_W, t_len), BF16),
                   jax.ShapeDtypeStruct((n, BRANCH_W), BF16),
                   jax.ShapeDtypeStruct((n, LANES), BF16),
                   jax.ShapeDtypeStruct((n_seq, 8, t_len), F32)],
        compiler_params=_cparams(1),
        name="dsa_prep",
    )(p, cos, slo, shi, tile4(q_norm_w), tile4(k_norm_w), inw, ones_bd)


QB = 128


def _float_key(x):
    b = lax.bitcast_convert_type(x, I32)
    b = jnp.where(b == INT_MIN, 0, b)
    return jnp.where(b < 0, b ^ jnp.int32(0x7FFFFFFF), b)


KT = 512
BITS_PER_CHECK = 4


def _dsa_prompt_kernel(n_sel, q_ref, qib_ref, wit_ref, kb_ref, vt_ref, kib_ref, o_ref,
                       key_ref, qbd_ref, acc_ref, m_ref, l_ref):
    j = pl.program_id(1)
    nkt = (j + KT // QB) // (KT // QB)
    q0 = j * QB

    @pl.when(jnp.logical_and(pl.program_id(0) == 0, j == 0))
    def _():
        qbd_ref[...] = jnp.zeros_like(qbd_ref)

    qib = qib_ref[...]
    qis = jnp.concatenate([qib[:, h * HEAD_DIM:(h + 1) * HEAD_DIM] for h in range(N_HEADS)], axis=0)
    qt = q_ref[...].T.astype(BF16)
    for h in range(N_HEADS):
        qbd_ref[h * HEAD_DIM:(h + 1) * HEAD_DIM, h * QB:(h + 1) * QB] = qt[h * HEAD_DIM:(h + 1) * HEAD_DIM, :]
    wit = wit_ref[0]
    srow = lax.broadcasted_iota(I32, (QB, QB), 0)
    tcol = lax.broadcasted_iota(I32, (QB, QB), 1) + q0
    tq = lax.broadcasted_iota(I32, (1, QB), 1) + q0

    def score_tile(kt, carry):
        for u in range(KT // QB):
            k0 = pl.multiple_of(kt * KT + u * QB, QB)
            kib = kib_ref[0, pl.ds(k0, QB), :][:, :HEAD_DIM]
            s = jnp.maximum(_dot_nt(kib, qis), 0.0)
            acc = jnp.zeros((QB, QB), F32)
            for h in range(N_HEADS):
                acc = acc + s[:, h * QB:(h + 1) * QB] * wit[h:h + 1, :]
            key_ref[pl.ds(k0, QB), :] = jnp.where(srow + k0 <= tcol, _float_key(acc), jnp.int32(INT_MIN))
        return carry

    lax.fori_loop(0, nkt, score_tile, 0)

    def count(pred):
        def body(kt, acc):
            k0 = pl.multiple_of(kt * KT, KT)
            m = pred(key_ref[pl.ds(k0, KT), :]).astype(I32)
            return acc + jnp.sum(m.reshape(KT // 8, 8, QB), axis=0)
        acc = lax.fori_loop(0, nkt, body, jnp.zeros((8, QB), I32))
        return jnp.sum(acc, axis=0, keepdims=True)

    def unfinished(st):
        return jnp.logical_and(st[0] < 32, jnp.logical_not(st[3]))

    def bit_step(st):
        i, tau, done, _ = st
        for _ in range(BITS_PER_CHECK):
            cand = tau + (jnp.int32(1) << (31 - i))
            c = count(lambda blk: blk >= cand)
            tau = jnp.where(jnp.logical_and(done == 0, c >= n_sel), cand, tau)
            done = jnp.where(c == n_sel, 1, done)
            i = i + 1
        return i, tau, done, jnp.min(done) > 0

    done0 = jnp.where(tq + 1 <= n_sel, 1, 0)
    _, tau, done, all_done = lax.while_loop(
        unfinished, bit_step, (jnp.int32(0), jnp.full((1, QB), INT_MIN, I32), done0, jnp.min(done0) > 0))

    @pl.when(jnp.logical_not(all_done))
    def _():
        cnt_gt = count(lambda blk: blk > tau)
        cnt_ge = count(lambda blk: blk >= tau)
        need = (n_sel - cnt_gt).astype(F32)
        excess = jnp.max(jnp.where(tau > INT_MIN, cnt_ge, 0)) > n_sel

        @pl.when(excess)
        def _():
            ltri = (lax.broadcasted_iota(I32, (QB, QB), 1) < srow).astype(BF16)

            def demote(kb, seen):
                k0 = pl.multiple_of(kb * QB, QB)
                blk = key_ref[pl.ds(k0, QB), :]
                tie = blk == tau
                tief = jnp.where(tie, 1.0, 0.0)
                rank = _dot(ltri, tief.astype(BF16)) + seen
                key_ref[pl.ds(k0, QB), :] = jnp.where(jnp.logical_and(tie, rank >= need),
                                                      jnp.int32(INT_MIN), blk)
                return seen + jnp.sum(tief, axis=0, keepdims=True)

            lax.fori_loop(0, j + 1, demote, jnp.zeros((1, QB), F32))

    thr = jnp.maximum(tau, INT_MIN + 1)
    acc_ref[...] = jnp.zeros_like(acc_ref)
    m_ref[...] = jnp.full_like(m_ref, NEG_BIG)
    l_ref[...] = jnp.zeros_like(l_ref)

    def attend(kt, carry):
        k0 = pl.multiple_of(kt * KT, KT)
        sel = key_ref[pl.ds(k0, KT), :] >= thr
        lg = _dot(kb_ref[0, pl.ds(k0, KT), :], qbd_ref[...])
        for h in range(N_HEADS):
            hs = slice(h * HEAD_DIM, (h + 1) * HEAD_DIM)
            lh = jnp.where(sel, lg[:, h * QB:(h + 1) * QB], NEG_BIG)
            m_old = m_ref[h:h + 1, :]
            m_new = jnp.maximum(m_old, jnp.max(lh, axis=0, keepdims=True))
            alpha = jnp.exp(m_old - m_new)
            p = jnp.exp(lh - m_new)
            l_ref[h:h + 1, :] = alpha * l_ref[h:h + 1, :] + jnp.sum(p, axis=0, keepdims=True)
            m_ref[h:h + 1, :] = m_new
            acc_ref[hs, :] = acc_ref[hs, :] * alpha + _dot(vt_ref[0, hs, pl.ds(k0, KT)], p.astype(BF16))
        return carry

    lax.fori_loop(0, nkt, attend, 0)

    outs = [acc_ref[h * HEAD_DIM:(h + 1) * HEAD_DIM, :] / l_ref[h:h + 1, :] for h in range(N_HEADS)]
    o_ref[...] = jnp.concatenate(outs, axis=0).T


def _dsa_prompt(q, qib, wit, kb, vt, kib, n_seq, t_len):
    assert t_len % KT == 0
    n_sel = min(TOPK_MAX, t_len // 4)
    nblk = t_len // QB
    return pl.pallas_call(
        functools.partial(_dsa_prompt_kernel, n_sel),
        grid=(n_seq, nblk),
        in_specs=[pl.BlockSpec((QB, BRANCH_W), lambda b, j: (b * nblk + j, 0)),
                  pl.BlockSpec((QB, BRANCH_W), lambda b, j: (b * nblk + j, 0)),
                  pl.BlockSpec((1, 8, QB), lambda b, j: (b, 0, j)),
                  pl.BlockSpec((1, t_len, BRANCH_W), lambda b, j: (b, 0, 0)),
                  pl.BlockSpec((1, BRANCH_W, t_len), lambda b, j: (b, 0, 0)),
                  pl.BlockSpec((1, t_len, LANES), lambda b, j: (b, 0, 0))],
        out_specs=pl.BlockSpec((QB, BRANCH_W), lambda b, j: (b * nblk + j, 0)),
        out_shape=jax.ShapeDtypeStruct((n_seq * t_len, BRANCH_W), F32),
        scratch_shapes=[pltpu.VMEM((t_len, QB), I32),
                        pltpu.VMEM((BRANCH_W, N_HEADS * QB), BF16),
                        pltpu.VMEM((BRANCH_W, QB), F32),
                        pltpu.VMEM((8, QB), F32),
                        pltpu.VMEM((8, QB), F32)],
        compiler_params=_cparams(2),
        name="dsa_prompt",
    )(q, qib, wit, kb.reshape(n_seq, t_len, BRANCH_W), vt, kib.reshape(n_seq, t_len, LANES))


RW_CHUNK = 128
RW_GROUP = 8


def _col_bcast(xt, i):
    lane = lax.broadcasted_iota(I32, (HEAD_DIM, LANES), 1)
    halves = []
    for pair in range(2):
        r0 = 2 * pair * HEAD_DIM
        c0 = jnp.broadcast_to(xt[r0:r0 + HEAD_DIM, i:i + 1], (HEAD_DIM, LANES))
        c1 = jnp.broadcast_to(xt[r0 + HEAD_DIM:r0 + 2 * HEAD_DIM, i:i + 1], (HEAD_DIM, LANES))
        halves.append(jnp.where(lane < HEAD_DIM, c0, c1))
    return jnp.concatenate(halves, axis=1)


def _group_cols(xt_ref, n_arr, gi):
    sh = lax.rem(LANES - gi * RW_GROUP, LANES)
    return [pltpu.roll(xt_ref[a], sh, 1) for a in range(n_arr)]


def _rwkv_prep(p, prev, mu, w0, w_wa, a0, w_g2, k_k, k_a, r_k, ones_bd):
    xl = p + (prev - p) * mu
    r, k, v = xl[:, 0:256], xl[:, 256:512], xl[:, 512:768]
    wa = xl[:, 768:896]
    lane = lax.broadcasted_iota(I32, wa.shape, 1)
    wa = jnp.where(lane < RWKV_DECAY_LORA, jnp.tanh(wa), wa)
    z = _dot(wa.astype(BF16), w_wa)
    nz = -(w0 + z[:, 0:256])
    w_log = -(jnp.maximum(nz, 0.0) + jnp.log1p(jnp.exp(-jnp.abs(nz)))) - 0.5
    log_decay = -jnp.exp(w_log)
    a = jax.nn.sigmoid(a0 + z[:, 256:512])
    g = _dot(jax.nn.sigmoid(xl[:, 896:1024]).astype(BF16), w_g2)
    kk = k * k_k
    kk = kk / jnp.sqrt(_seg_sum(kk * kk, ones_bd) + 1e-12)
    kp = k * (1.0 + (a - 1.0) * k_a)
    bonus = _seg_sum(r * kp * r_k, ones_bd) * v
    return r, log_decay, kp, v, kk, kk * a, g, bonus


def _rwkv_kernel(sequential, p_ref, prev_ref, st0_ref, mu_ref, w0_ref, wwa_ref, a0_ref, wg2_ref,
                 kk_ref, ka_ref, rk_ref, lnw_ref, lnb_ref, ones_ref,
                 o_ref, st_out_ref, carry_ref, st_ref, xt_ref, v_ref, y_ref):
    c = pl.program_id(1)
    p = p_ref[...]
    if sequential:
        @pl.when(c == 0)
        def _():
            carry_ref[...] = prev_ref[0]
            st_ref[...] = st0_ref[0]

        row = lax.broadcasted_iota(I32, p.shape, 0)
        prev = jnp.where(row == 0, carry_ref[...], pltpu.roll(p, 1, 0))
        carry_ref[...] = p[RW_CHUNK - 1:RW_CHUNK, :]
    else:
        prev = prev_ref[...]
    ones_bd = ones_ref[...]
    r, log_decay, kp, v, kk, kka, g, bonus = _rwkv_prep(
        p, prev, mu_ref[...], w0_ref[...], wwa_ref[...], a0_ref[...], wg2_ref[...],
        kk_ref[...], ka_ref[...], rk_ref[...], ones_bd)
    for a, arr in enumerate((r, jnp.exp(log_decay), kp, kk, kka)):
        xt_ref[a] = arr.T
    v_ref[...] = v

    def group(gi, st):
        rt, wt, kt, kkt, kkat = _group_cols(xt_ref, 5, gi)
        base = pl.multiple_of(gi * RW_GROUP, RW_GROUP)
        vrows = v_ref[pl.ds(base, RW_GROUP), :]
        ys = []
        for i in range(RW_GROUP):
            if not sequential:
                st = st0_ref[base + i]
            sa = jnp.sum(st * _col_bcast(kkt, i), axis=0, keepdims=True)
            st = st * _col_bcast(wt, i) - _col_bcast(kkat, i) * sa + _col_bcast(kt, i) * vrows[i:i + 1, :]
            ys.append(jnp.sum(st * _col_bcast(rt, i), axis=0, keepdims=True))
            if not sequential:
                st_out_ref[base + i] = st
        y_ref[pl.ds(base, RW_GROUP), :] = jnp.concatenate(ys, axis=0)
        return st

    if sequential:
        st = lax.fori_loop(0, RW_CHUNK // RW_GROUP, group, st_ref[...])
        st_ref[...] = st

        @pl.when(c == pl.num_programs(1) - 1)
        def _():
            st_out_ref[0] = st
    else:
        lax.fori_loop(0, RW_CHUNK // RW_GROUP, group, jnp.zeros((HEAD_DIM, BRANCH_W), F32))

    o = _head_group_norm(y_ref[...], ones_bd, lnw_ref[...], lnb_ref[...], RWKV_LN_EPS)
    o_ref[...] = (o + bonus) * g


def _rwkv(p, prev, st0, lw, ones_bd, n_seq, t_len, sequential):
    n = n_seq * t_len
    nchunk = t_len // RW_CHUNK if sequential else 1
    full2 = lambda a, b: pl.BlockSpec((a, b), lambda s, c: (0, 0))
    if sequential:
        prev_spec = pl.BlockSpec((1, 1, RWKV_P), lambda s, c: (s, 0, 0))
        st_spec = pl.BlockSpec((1, HEAD_DIM, BRANCH_W), lambda s, c: (s, 0, 0))
        grid = (n_seq, nchunk)
    else:
        assert n == RW_CHUNK
        prev_spec = pl.BlockSpec((RW_CHUNK, RWKV_P), lambda s, c: (0, 0))
        st_spec = pl.BlockSpec((RW_CHUNK, HEAD_DIM, BRANCH_W), lambda s, c: (0, 0, 0))
        grid = (1, 1)
    return pl.pallas_call(
        functools.partial(_rwkv_kernel, sequential),
        grid=grid,
        in_specs=[pl.BlockSpec((RW_CHUNK, RWKV_P), lambda s, c: (s * nchunk + c, COL_RWKV // RWKV_P)),
                  prev_spec, st_spec,
                  full2(1, RWKV_P), full2(1, BRANCH_W), full2(LANES, 2 * BRANCH_W), full2(1, BRANCH_W),
                  full2(RWKV_G_LORA, BRANCH_W), full2(1, BRANCH_W), full2(1, BRANCH_W), full2(1, BRANCH_W),
                  full2(1, BRANCH_W), full2(1, BRANCH_W), full2(BRANCH_W, BRANCH_W)],
        out_specs=[pl.BlockSpec((RW_CHUNK, BRANCH_W), lambda s, c: (s * nchunk + c, 0)), st_spec],
        out_shape=[jax.ShapeDtypeStruct((n, BRANCH_W), F32),
                   jax.ShapeDtypeStruct(st0.shape, F32)],
        scratch_shapes=[pltpu.VMEM((1, RWKV_P), F32),
                        pltpu.VMEM((HEAD_DIM, BRANCH_W), F32),
                        pltpu.VMEM((5, BRANCH_W, RW_CHUNK), F32),
                        pltpu.VMEM((RW_CHUNK, BRANCH_W), F32),
                        pltpu.VMEM((RW_CHUNK, BRANCH_W), F32)],
        compiler_params=_cparams(2),
        name="rwkv7_prompt" if sequential else "rwkv7_sample",
    )(p, prev, st0, lw["mu"], lw["w0"], lw["w_wa"], lw["a0"], lw["w_g2"], lw["k_k"], lw["k_a"],
      lw["r_k"], lw["ln_w"], lw["ln_b"], ones_bd)


RC = 64


def _mm(a, b):
    return _dot(a.astype(BF16), b.astype(BF16))


def _rwkv_chunk_kernel(p_ref, mu_ref, w0_ref, wwa_ref, a0_ref, wg2_ref, kk_ref, ka_ref, rk_ref,
                       lnw_ref, lnb_ref, ones_ref, o_ref, st_out_ref, carry_ref, z_ref):
    c = pl.program_id(1)

    @pl.when(c == 0)
    def _():
        carry_ref[...] = jnp.zeros_like(carry_ref)
        z_ref[...] = jnp.zeros_like(z_ref)

    p = p_ref[...]
    row = lax.broadcasted_iota(I32, p.shape, 0)
    prev = jnp.where(row == 0, carry_ref[...], pltpu.roll(p, 1, 0))
    carry_ref[...] = p[RC - 1:RC, :]
    ones_bd = ones_ref[...]
    r, log_decay, kp, v, kk, kka, g, bonus = _rwkv_prep(
        p, prev, mu_ref[...], w0_ref[...], wwa_ref[...], a0_ref[...], wg2_ref[...],
        kk_ref[...], ka_ref[...], rk_ref[...], ones_bd)

    trow = lax.broadcasted_iota(I32, (RC, BRANCH_W), 0)
    cs = log_decay
    for s in (1, 2, 4, 8, 16, 32):
        cs = cs + jnp.where(trow >= s, pltpu.roll(cs, s, 0), 0.0)
    pw = jnp.exp(cs)
    inv_pw = jnp.exp(-cs)
    a_m = -(kk * jnp.exp(cs - log_decay))
    r_m = r * pw
    b_m = kka * inv_pw
    k_m = kp * inv_pw
    bkt = jnp.concatenate([b_m, k_m], axis=0).T
    kbt = jnp.concatenate([k_m, b_m], axis=0).T
    pwt = jnp.concatenate([pw, pw], axis=0).T

    lane_head = lax.broadcasted_iota(I32, (RC, BRANCH_W), 1) // HEAD_DIM
    hm = [lane_head == h for h in range(N_HEADS)]
    xstack = jnp.concatenate([jnp.where(hm[h], a_m, 0.0) for h in range(N_HEADS)]
                             + [jnp.where(hm[h], r_m, 0.0) for h in range(N_HEADS)], axis=0)
    g1 = _mm(xstack, bkt)
    g2 = _mm(xstack, kbt)
    nst = N_HEADS * RC
    t_of_row = lax.rem(lax.broadcasted_iota(I32, (nst, LANES), 0), RC)
    s_of_lane = lax.broadcasted_iota(I32, (nst, LANES), 1)
    strict = s_of_lane < t_of_row
    incl = jnp.logical_and(s_of_lane <= t_of_row, s_of_lane < RC)
    l_st = jnp.where(strict, g1[0:nst], 0.0)
    m_st = jnp.where(strict, g2[0:nst], 0.0)
    n_st = jnp.where(incl, g1[nst:2 * nst], 0.0)
    q_st = jnp.where(incl, g2[nst:2 * nst], 0.0)

    eye = jnp.where(lax.broadcasted_iota(I32, (RC, LANES), 0) == lax.broadcasted_iota(I32, (RC, LANES), 1),
                    1.0, 0.0)
    ws = []
    for h in range(N_HEADS):
        lp = l_st[h * RC:(h + 1) * RC]
        w_h = eye + lp
        for _ in range(5):
            lp = _mm(lp[:, :RC], lp)
            w_h = w_h + _mm(w_h[:, :RC], lp)
        ws.append(w_h)

    def per_head(stacked):
        out = jnp.where(hm[0], stacked[0:RC], 0.0)
        for h in range(1, N_HEADS):
            out = out + jnp.where(hm[h], stacked[h * RC:(h + 1) * RC], 0.0)
        return out

    z0 = z_ref[...]
    row_head = lax.broadcasted_iota(I32, (BRANCH_W, BRANCH_W), 0) // HEAD_DIM
    col_head = lax.broadcasted_iota(I32, (BRANCH_W, BRANCH_W), 1) // HEAD_DIM
    zbd = jnp.where(row_head == col_head, jnp.concatenate([z0] * N_HEADS, axis=0), 0.0)
    rhs = _mm(a_m, zbd) + per_head(_mm(m_st[:, :RC], v))
    u = per_head(jnp.concatenate([_mm(ws[h][:, :RC], rhs) for h in range(N_HEADS)], axis=0))
    y = _mm(r_m, zbd) + per_head(_mm(n_st[:, :RC], u) + _mm(q_st[:, :RC], v))
    uv = jnp.concatenate([u, v], axis=0)
    upd = jnp.concatenate([_mm(bkt[h * HEAD_DIM:(h + 1) * HEAD_DIM, :], uv) for h in range(N_HEADS)], axis=0)
    z_new = _col_bcast(pwt, RC - 1) * (z0 + per_head(upd))
    z_ref[...] = z_new

    @pl.when(c == pl.num_programs(1) - 1)
    def _():
        st_out_ref[0] = z_new

    o = _head_group_norm(y, ones_bd, lnw_ref[...], lnb_ref[...], RWKV_LN_EPS)
    o_ref[...] = (o + bonus) * g


def _rwkv_prompt(p, lw, ones_bd, n_seq, t_len):
    nchunk = t_len // RC
    full2 = lambda a, b: pl.BlockSpec((a, b), lambda s, c: (0, 0))
    return pl.pallas_call(
        _rwkv_chunk_kernel,
        grid=(n_seq, nchunk),
        in_specs=[pl.BlockSpec((RC, RWKV_P), lambda s, c: (s * nchunk + c, COL_RWKV // RWKV_P)),
                  full2(1, RWKV_P), full2(1, BRANCH_W), full2(LANES, 2 * BRANCH_W), full2(1, BRANCH_W),
                  full2(RWKV_G_LORA, BRANCH_W), full2(1, BRANCH_W), full2(1, BRANCH_W), full2(1, BRANCH_W),
                  full2(1, BRANCH_W), full2(1, BRANCH_W), full2(BRANCH_W, BRANCH_W)],
        out_specs=[pl.BlockSpec((RC, BRANCH_W), lambda s, c: (s * nchunk + c, 0)),
                   pl.BlockSpec((1, HEAD_DIM, BRANCH_W), lambda s, c: (s, 0, 0))],
        out_shape=[jax.ShapeDtypeStruct((n_seq * t_len, BRANCH_W), F32),
                   jax.ShapeDtypeStruct((n_seq, HEAD_DIM, BRANCH_W), F32)],
        scratch_shapes=[pltpu.VMEM((1, RWKV_P), F32),
                        pltpu.VMEM((HEAD_DIM, BRANCH_W), F32)],
        compiler_params=_cparams(2),
        name="rwkv7_prompt",
    )(p, lw["mu"], lw["w0"], lw["w_wa"], lw["a0"], lw["w_g2"], lw["k_k"], lw["k_a"],
      lw["r_k"], lw["ln_w"], lw["ln_b"], ones_bd)


def _ret_prompt_kernel(p_ref, cos_ref, slo_ref, shi_ref, dmask_ref, qdec_ref, kdec_ref, sdec_ref,
                       gnw_ref, gnb_ref, ones_ref, o_ref, s_out_ref, s_ref):
    c = pl.program_id(1)

    @pl.when(c == 0)
    def _():
        s_ref[...] = jnp.zeros_like(s_ref)

    cos, slo, shi = cos_ref[...], slo_ref[...], shi_ref[...]
    half = HEAD_DIM // 2
    rq = _rope(p_ref[:, 0:256], cos, slo, shi, half)
    rk = _rope(p_ref[:, 256:512], cos, slo, shi, half) * RET_KSCALE
    rv = p_ref[:, 512:768]
    rg = p_ref[:, 768:1024]
    kdt = (rk * kdec_ref[...]).T.astype(BF16)
    qdec = qdec_ref[...]
    outs = []
    for h in range(N_HEADS):
        hs = slice(h * HEAD_DIM, (h + 1) * HEAD_DIM)
        qh, kh, vh = rq[:, hs].astype(BF16), rk[:, hs].astype(BF16), rv[:, hs].astype(BF16)
        att = _dot_nt(qh, kh) * dmask_ref[h]
        s_old = s_ref[hs, :]
        outs.append(_dot(att.astype(BF16), vh) + _dot(qh, s_old.astype(BF16)) * qdec[:, hs])
        s_ref[hs, :] = s_old * sdec_ref[hs, :] + _dot(kdt[hs, :], vh)
    o = jnp.concatenate(outs, axis=1)
    gn = _head_group_norm(o, ones_ref[...], gnw_ref[...], gnb_ref[...], GN_EPS)
    o_ref[...] = rg * jax.nn.sigmoid(rg) * gn

    @pl.when(c == pl.num_programs(1) - 1)
    def _():
        s_out_ref[0] = s_ref[...]


def _ret_prompt(p, tables, dec, gn_w, gn_b, ones_bd, n_seq, t_len):
    nchunk = t_len // RET_CHUNK
    cos, slo, shi = tables
    dmask, qdec, kdec, sdec = dec
    tab = pl.BlockSpec((RET_CHUNK, BRANCH_W), lambda s, c: (c, 0))
    full2 = lambda a, b: pl.BlockSpec((a, b), lambda s, c: (0, 0))
    return pl.pallas_call(
        _ret_prompt_kernel,
        grid=(n_seq, nchunk),
        in_specs=[pl.BlockSpec((RET_CHUNK, RET_P), lambda s, c: (s * nchunk + c, COL_RET // RET_P)),
                  tab, tab, tab,
                  pl.BlockSpec((N_HEADS, RET_CHUNK, RET_CHUNK), lambda s, c: (0, 0, 0)),
                  full2(RET_CHUNK, BRANCH_W), full2(RET_CHUNK, BRANCH_W), full2(BRANCH_W, HEAD_DIM),
                  full2(1, BRANCH_W), full2(1, BRANCH_W), full2(BRANCH_W, BRANCH_W)],
        out_specs=[pl.BlockSpec((RET_CHUNK, BRANCH_W), lambda s, c: (s * nchunk + c, 0)),
                   pl.BlockSpec((1, BRANCH_W, HEAD_DIM), lambda s, c: (s, 0, 0))],
        out_shape=[jax.ShapeDtypeStruct((n_seq * t_len, BRANCH_W), F32),
                   jax.ShapeDtypeStruct((n_seq, BRANCH_W, HEAD_DIM), F32)],
        scratch_shapes=[pltpu.VMEM((BRANCH_W, HEAD_DIM), F32)],
        compiler_params=_cparams(2),
        name="retention_prompt",
    )(p, cos, slo, shi, dmask, qdec, kdec, sdec, gn_w.reshape(1, BRANCH_W), gn_b.reshape(1, BRANCH_W),
      ones_bd)


def _ret_sample_kernel(p_ref, cos_ref, slo_ref, shi_ref, st0_ref, gamma_ref, gnw_ref, gnb_ref, ones_ref,
                       o_ref, st_out_ref, xt_ref, v_ref, y_ref):
    cos, slo, shi = cos_ref[...], slo_ref[...], shi_ref[...]
    half = HEAD_DIM // 2
    rq = _rope(p_ref[:, 0:256], cos, slo, shi, half)
    rk = _rope(p_ref[:, 256:512], cos, slo, shi, half) * RET_KSCALE
    rg = p_ref[:, 768:1024]
    xt_ref[0] = rq.T
    xt_ref[1] = rk.T
    v_ref[...] = p_ref[:, 512:768]
    gamma = gamma_ref[...]

    def group(gi, carry):
        qt, kt = _group_cols(xt_ref, 2, gi)
        base = pl.multiple_of(gi * RW_GROUP, RW_GROUP)
        vrows = v_ref[pl.ds(base, RW_GROUP), :]
        ys = []
        for i in range(RW_GROUP):
            st = st0_ref[base + i] * gamma + _col_bcast(kt, i) * vrows[i:i + 1, :]
            st_out_ref[base + i] = st
            ys.append(jnp.sum(st * _col_bcast(qt, i), axis=0, keepdims=True))
        y_ref[pl.ds(base, RW_GROUP), :] = jnp.concatenate(ys, axis=0)
        return carry

    lax.fori_loop(0, RW_CHUNK // RW_GROUP, group, 0)
    gn = _head_group_norm(y_ref[...], ones_ref[...], gnw_ref[...], gnb_ref[...], GN_EPS)
    o_ref[...] = rg * jax.nn.sigmoid(rg) * gn


def _ret_sample(p, tables, st0, gamma, gn_w, gn_b, ones_bd):
    n = st0.shape[0]
    assert n == RW_CHUNK
    cos, slo, shi = tables
    full2 = lambda a, b: pl.BlockSpec((a, b), lambda i: (0, 0))
    st_spec = pl.BlockSpec((n, HEAD_DIM, BRANCH_W), lambda i: (0, 0, 0))
    return pl.pallas_call(
        _ret_sample_kernel,
        grid=(1,),
        in_specs=[pl.BlockSpec((n, RET_P), lambda i: (0, COL_RET // RET_P)),
                  full2(n, BRANCH_W), full2(n, BRANCH_W), full2(n, BRANCH_W), st_spec,
                  full2(1, BRANCH_W), full2(1, BRANCH_W), full2(1, BRANCH_W), full2(BRANCH_W, BRANCH_W)],
        out_specs=[full2(n, BRANCH_W), st_spec],
        out_shape=[jax.ShapeDtypeStruct((n, BRANCH_W), F32), jax.ShapeDtypeStruct(st0.shape, F32)],
        scratch_shapes=[pltpu.VMEM((2, BRANCH_W, RW_CHUNK), F32),
                        pltpu.VMEM((RW_CHUNK, BRANCH_W), F32),
                        pltpu.VMEM((RW_CHUNK, BRANCH_W), F32)],
        compiler_params=_cparams(1),
        name="retention_sample",
    )(p, cos, slo, shi, st0, gamma, gn_w.reshape(1, BRANCH_W), gn_b.reshape(1, BRANCH_W), ones_bd)


POOL_HIST = 16


def _pool_windows(lane):
    win = jnp.full(lane.shape, float(POOL_WINDOWS[-1]), F32)
    for gi in range(len(POOL_WINDOWS) - 2, -1, -1):
        win = jnp.where(lane < (gi + 1) * HEAD_DIM, float(POOL_WINDOWS[gi]), win)
    return win


def _pool_select(sums, lane):
    out = sums[-1]
    for gi in range(len(POOL_WINDOWS) - 2, -1, -1):
        out = jnp.where(lane < (gi + 1) * HEAD_DIM, sums[gi], out)
    return out


def _pool_prompt_kernel(tm, u_ref, wp_ref, scale_ref, y_ref, nbuf_ref, ext_ref):
    t = pl.program_id(1)

    @pl.when(t == 0)
    def _():
        ext_ref[0:POOL_HIST, :] = jnp.zeros((POOL_HIST, BRANCH_W), F32)

    @pl.when(t > 0)
    def _():
        ext_ref[0:POOL_HIST, :] = ext_ref[tm:tm + POOL_HIST, :]

    u = u_ref[...]
    ext_ref[POOL_HIST:POOL_HIST + tm, :] = u
    sums, acc, i = [], u, 1
    for win in POOL_WINDOWS:
        while i < win:
            acc = acc + ext_ref[pl.ds(POOL_HIST - i, tm), :]
            i += 1
        sums.append(acc)
    lane = lax.broadcasted_iota(I32, u.shape, 1)
    pos = (lax.broadcasted_iota(I32, u.shape, 0) + t * tm).astype(F32)
    cnt = jnp.minimum(pos + 1.0, _pool_windows(lane))
    d = _pool_select(sums, lane) / cnt - u
    y_ref[...] = _dot(d.astype(BF16), wp_ref[...]) * scale_ref[...]

    @pl.when(t == pl.num_programs(1) - 1)
    def _():
        nbuf_ref[0] = ext_ref[tm:tm + POOL_HIST, :]


def _pool_prompt(p, wp_bd, scale, n_seq, t_len, tm):
    tpb = t_len // tm
    return pl.pallas_call(
        functools.partial(_pool_prompt_kernel, tm),
        grid=(n_seq, tpb),
        in_specs=[pl.BlockSpec((tm, BRANCH_W), lambda s, t: (s * tpb + t, COL_POOL // BRANCH_W)),
                  pl.BlockSpec((BRANCH_W, BRANCH_W), lambda s, t: (0, 0)),
                  pl.BlockSpec((1, BRANCH_W), lambda s, t: (0, 0))],
        out_specs=[pl.BlockSpec((tm, BRANCH_W), lambda s, t: (s * tpb + t, 0)),
                   pl.BlockSpec((1, POOL_HIST, BRANCH_W), lambda s, t: (s, 0, 0))],
        out_shape=[jax.ShapeDtypeStruct((n_seq * t_len, BRANCH_W), F32),
                   jax.ShapeDtypeStruct((n_seq, POOL_HIST, BRANCH_W), F32)],
        scratch_shapes=[pltpu.VMEM((tm + POOL_HIST, BRANCH_W), F32)],
        compiler_params=_cparams(2),
        name="pool_prompt",
    )(p, wp_bd, scale.reshape(1, BRANCH_W))


def _pool_sample_kernel(pos0, u_ref, buf_ref, wp_ref, scale_ref, y_ref, nbuf_ref):
    u = u_ref[...]
    sums, acc, i = [], u, 1
    for win in POOL_WINDOWS:
        while i < win:
            acc = acc + buf_ref[POOL_BUF - i]
            i += 1
        sums.append(acc)
    lane = lax.broadcasted_iota(I32, u.shape, 1)
    cnt = jnp.minimum(float(pos0) + 1.0, _pool_windows(lane))
    d = _pool_select(sums, lane) / cnt - u
    y_ref[...] = _dot(d.astype(BF16), wp_ref[...]) * scale_ref[...]
    for r in range(POOL_BUF - 1):
        nbuf_ref[r] = buf_ref[r + 1]
    nbuf_ref[POOL_BUF - 1] = u


def _pool_sample(p, buf_t, wp_bd, scale, pos0):
    n = buf_t.shape[1]
    full3 = pl.BlockSpec((POOL_BUF, n, BRANCH_W), lambda i: (0, 0, 0))
    return pl.pallas_call(
        functools.partial(_pool_sample_kernel, pos0),
        grid=(1,),
        in_specs=[pl.BlockSpec((n, BRANCH_W), lambda i: (0, COL_POOL // BRANCH_W)), full3,
                  pl.BlockSpec((BRANCH_W, BRANCH_W), lambda i: (0, 0)),
                  pl.BlockSpec((1, BRANCH_W), lambda i: (0, 0))],
        out_specs=[pl.BlockSpec((n, BRANCH_W), lambda i: (0, 0)), full3],
        out_shape=[jax.ShapeDtypeStruct((n, BRANCH_W), F32),
                   jax.ShapeDtypeStruct((POOL_BUF, n, BRANCH_W), F32)],
        compiler_params=_cparams(1),
        name="pool_sample",
    )(p, buf_t, wp_bd, scale.reshape(1, BRANCH_W))


def _page_copies(pt_ref, layer, seq, slot, hbm_ref, buf_ref, sem_ref, n_pages):
    return [pltpu.make_async_copy(hbm_ref.at[layer, pt_ref[seq, pg]], buf_ref.at[slot, pg], sem_ref.at[slot])
            for pg in range(n_pages)]


def _paged_prefetch(pt_ref, layer, pairs, n_pages):
    b = pl.program_id(0)
    nb = pl.num_programs(0)
    slot = lax.rem(b, 2)

    @pl.when(b == 0)
    def _():
        for hbm_ref, buf_ref, sem_ref in pairs:
            for cp in _page_copies(pt_ref, layer, 0, 0, hbm_ref, buf_ref, sem_ref, n_pages):
                cp.start()

    @pl.when(b + 1 < nb)
    def _():
        for hbm_ref, buf_ref, sem_ref in pairs:
            for cp in _page_copies(pt_ref, layer, b + 1, 1 - slot, hbm_ref, buf_ref, sem_ref, n_pages):
                cp.start()

    for hbm_ref, buf_ref, sem_ref in pairs:
        for cp in _page_copies(pt_ref, layer, b, slot, hbm_ref, buf_ref, sem_ref, n_pages):
            cp.wait()
    return slot


def _dsa_sample_score_kernel(layer, n_pages, pt_ref, qi_ref, wi_ref, cki_ref, out_ref, buf_ref, sem_ref):
    slot = _paged_prefetch(pt_ref, layer, [(cki_ref, buf_ref, sem_ref)], n_pages)
    qi4, wi4 = qi_ref[0], wi_ref[0]
    rows = []
    for pg in range(n_pages):
        s = jnp.maximum(_dot(qi4, buf_ref[slot, pg].astype(BF16)), 0.0)
        rows.append(jnp.sum(s * wi4, axis=0, keepdims=True))
    out_ref[0] = jnp.concatenate(rows, axis=1)


def _dsa_sample_scores(page_table, qi4, wi4, cache_kidx_t, layer):
    nb, n_pages = page_table.shape
    page = cache_kidx_t.shape[3]
    past = n_pages * page
    return pl.pallas_call(
        functools.partial(_dsa_sample_score_kernel, layer, n_pages),
        grid_spec=pltpu.PrefetchScalarGridSpec(
            num_scalar_prefetch=1,
            grid=(nb,),
            in_specs=[pl.BlockSpec((1, N_HEADS, HEAD_DIM), lambda b, pt: (b, 0, 0)),
                      pl.BlockSpec((1, N_HEADS, 1), lambda b, pt: (b, 0, 0)),
                      pl.BlockSpec(memory_space=pl.ANY)],
            out_specs=pl.BlockSpec((1, 1, past), lambda b, pt: (b, 0, 0)),
            scratch_shapes=[pltpu.VMEM((2, n_pages, HEAD_DIM, page), F32),
                            pltpu.SemaphoreType.DMA((2,))]),
        out_shape=jax.ShapeDtypeStruct((nb, 1, past), F32),
        compiler_params=_cparams(1),
        name="dsa_sample_scores",
    )(page_table, qi4, wi4, cache_kidx_t)


def _dsa_sample_select_kernel(n_sel, sc_ref, qib_ref, kib_ref, wi_ref, sel_ref, selself_ref):
    nb, past = sc_ref.shape
    qi = qib_ref[...].astype(F32)
    ki = kib_ref[...].astype(F32)[:, :HEAD_DIM]
    wi = wi_ref[...]
    s_self = jnp.zeros((nb, 1), F32)
    for h in range(N_HEADS):
        dot_h = jnp.sum(qi[:, h * HEAD_DIM:(h + 1) * HEAD_DIM] * ki, axis=-1, keepdims=True)
        s_self = s_self + jnp.maximum(dot_h, 0.0) * wi[:, h:h + 1]
    key = _float_key(sc_ref[...] + 0.0)
    key_self = _float_key(s_self)

    def count(pred):
        return (jnp.sum(pred(key).astype(I32), axis=-1, keepdims=True) + pred(key_self).astype(I32))

    def bit_step(i, tau):
        cand = tau + (jnp.int32(1) << (31 - i))
        return jnp.where(count(lambda x: x >= cand) >= n_sel, cand, tau)

    tau = lax.fori_loop(0, 32, bit_step, jnp.full((nb, 1), INT_MIN, I32))
    gt = key > tau
    tie = key == tau
    need = (n_sel - count(lambda x: x > tau)).astype(F32)
    utri = (lax.broadcasted_iota(I32, (LANES, LANES), 0)
            < lax.broadcasted_iota(I32, (LANES, LANES), 1)).astype(BF16)
    seen = jnp.zeros((nb, 1), F32)
    for c in range(past // LANES):
        cs = slice(c * LANES, (c + 1) * LANES)
        tief = jnp.where(tie[:, cs], 1.0, 0.0)
        rank = _dot(tief.astype(BF16), utri) + seen
        keep = jnp.logical_or(gt[:, cs], jnp.logical_and(tie[:, cs], rank < need))
        sel_ref[:, cs] = jnp.where(keep, 1.0, 0.0)
        seen = seen + jnp.sum(tief, axis=-1, keepdims=True)
    keep_self = jnp.logical_or(key_self > tau, jnp.logical_and(key_self == tau, seen < need))
    selself_ref[...] = jnp.where(keep_self, 1.0, 0.0)


def _dsa_sample_select(scores, qib, kib, wi, n_sel):
    nb, past = scores.shape
    full2 = lambda a, b: pl.BlockSpec((a, b), lambda i: (0, 0))
    return pl.pallas_call(
        functools.partial(_dsa_sample_select_kernel, n_sel),
        grid=(1,),
        in_specs=[full2(nb, past), full2(nb, BRANCH_W), full2(nb, LANES), full2(nb, 8)],
        out_specs=[full2(nb, past), full2(nb, 1)],
        out_shape=[jax.ShapeDtypeStruct((nb, past), F32), jax.ShapeDtypeStruct((nb, 1), F32)],
        compiler_params=_cparams(1),
        name="dsa_sample_select",
    )(scores, qib, kib, wi)


def _dsa_sample_attend_kernel(layer, n_pages, pt_ref, q_ref, sel_ref, selself_ref, kself_ref, vself_ref,
                              ck_ref, cv_ref, o_ref, kbuf_ref, vbuf_ref, ksem_ref, vsem_ref):
    slot = _paged_prefetch(pt_ref, layer, [(ck_ref, kbuf_ref, ksem_ref), (cv_ref, vbuf_ref, vsem_ref)], n_pages)
    head_of_lane = lax.broadcasted_iota(I32, (N_HEADS, BRANCH_W), 1) // HEAD_DIM
    own = head_of_lane == lax.broadcasted_iota(I32, (N_HEADS, BRANCH_W), 0)
    qrows = jnp.where(own, jnp.broadcast_to(q_ref[0], (N_HEADS, BRANCH_W)), 0.0).astype(BF16)
    lg = jnp.concatenate([_dot(qrows, kbuf_ref[slot, pg].astype(BF16)) for pg in range(n_pages)], axis=1)
    lg = jnp.where(sel_ref[0] > 0.0, lg, NEG_BIG)
    kself = kself_ref[0].astype(BF16).astype(F32)
    lself = jnp.sum(qrows.astype(F32) * kself, axis=-1, keepdims=True)
    lself = jnp.where(selself_ref[0] > 0.0, lself, NEG_BIG)
    m = jnp.maximum(jnp.max(lg, axis=-1, keepdims=True), lself)
    p = jnp.exp(lg - m)
    pself = jnp.exp(lself - m)
    denom = jnp.sum(p, axis=-1, keepdims=True) + pself
    p = p.astype(BF16)
    page = kbuf_ref.shape[3]
    o4 = pself * vself_ref[0]
    for pg in range(n_pages):
        o4 = o4 + _dot_nt(p[:, pg * page:(pg + 1) * page], vbuf_ref[slot, pg].astype(BF16))
    o_ref[0] = jnp.sum(jnp.where(own, o4 / denom, 0.0), axis=0, keepdims=True)


def _dsa_sample_attend(page_table, q, sel, selself, kself, vself, cache_k_t, cache_v_t, layer):
    nb, n_pages = page_table.shape
    page = cache_k_t.shape[3]
    past = n_pages * page
    row3 = lambda w: pl.BlockSpec((1, 1, w), lambda b, pt: (b, 0, 0))
    return pl.pallas_call(
        functools.partial(_dsa_sample_attend_kernel, layer, n_pages),
        grid_spec=pltpu.PrefetchScalarGridSpec(
            num_scalar_prefetch=1,
            grid=(nb,),
            in_specs=[row3(BRANCH_W), row3(past), row3(1), row3(BRANCH_W), row3(BRANCH_W),
                      pl.BlockSpec(memory_space=pl.ANY), pl.BlockSpec(memory_space=pl.ANY)],
            out_specs=row3(BRANCH_W),
            scratch_shapes=[pltpu.VMEM((2, n_pages, BRANCH_W, page), F32),
                            pltpu.VMEM((2, n_pages, BRANCH_W, page), F32),
                            pltpu.SemaphoreType.DMA((2,)),
                            pltpu.SemaphoreType.DMA((2,))]),
        out_shape=jax.ShapeDtypeStruct((nb, 1, BRANCH_W), F32),
        compiler_params=_cparams(1),
        name="dsa_sample_attend",
    )(page_table, q.reshape(nb, 1, BRANCH_W), sel.reshape(nb, 1, past), selself.reshape(nb, 1, 1),
      kself.reshape(nb, 1, BRANCH_W), vself.reshape(nb, 1, BRANCH_W), cache_k_t, cache_v_t)


def _rope_tables(pos, rot_dim, theta):
    half = rot_dim // 2
    inv = theta ** (-jnp.arange(half, dtype=F32) / half)
    ang = pos.astype(F32)[:, None] * inv[None, :]
    cos, sin = jnp.cos(ang), jnp.sin(ang)
    n = pos.shape[0]
    zh = jnp.zeros((n, half), F32)
    zr = jnp.zeros((n, HEAD_DIM - rot_dim), F32)
    c = jnp.concatenate([cos, cos, jnp.ones((n, HEAD_DIM - rot_dim), F32)], axis=1)
    lo = jnp.concatenate([-sin, zh, zr], axis=1)
    hi = jnp.concatenate([zh, sin, zr], axis=1)
    return tuple(jnp.tile(t, (1, N_HEADS)) for t in (c, lo, hi))


def _ret_decay(chunk):
    log_g = jnp.log1p(-jnp.exp2(-5.0 - jnp.arange(N_HEADS, dtype=F32)))
    n = jnp.arange(chunk, dtype=F32)
    diff = n[:, None] - n[None, :]
    dmask = jnp.where(diff >= 0, jnp.exp(log_g[:, None, None] * jnp.maximum(diff, 0.0)), 0.0)
    q_dec = jnp.exp(log_g[:, None] * (n + 1.0)).T
    k_dec = jnp.exp(log_g[:, None] * (chunk - 1.0 - n))
    s_dec = jnp.exp(log_g * chunk)
    qdec = jnp.repeat(q_dec, HEAD_DIM, axis=1)
    kdec = jnp.repeat(k_dec.T, HEAD_DIM, axis=1)
    sdec_rows = jnp.repeat(s_dec, HEAD_DIM)
    return dmask, qdec, kdec, sdec_rows


def _block_diag(blocks):
    n = len(blocks)
    rows = []
    for i, blk in enumerate(blocks):
        rows.append(jnp.concatenate([blk if j == i else jnp.zeros((blk.shape[0], blocks[j].shape[1]), blk.dtype)
                                     for j in range(n)], axis=1))
    return jnp.concatenate(rows, axis=0)


def kernel(x_prompt, x_sample, cache_k, cache_v, cache_kidx, state_wkv, state_shift, state_ret, state_pool, page_table, norm1_w, w_in, rwkv_mu, rwkv_w0, rwkv_w_w2, rwkv_a0, rwkv_w_a2, rwkv_w_g2, rwkv_k_k, rwkv_k_a, rwkv_r_k, rwkv_ln_w, rwkv_ln_b, q_norm_w, k_norm_w, idx_k_norm_w, ret_gn_w, ret_gn_b, pool_w, pool_scale, w_branch, w_out, norm2_w, w_ffn_in, w_ffn_out):
    n_seq, t_len, _ = x_prompt.shape
    nb = x_sample.shape[0]
    assert x_sample.shape[1] == 1
    page = cache_k.shape[2]
    past = page_table.shape[1] * page
    n_sel_sample = min(TOPK_MAX, (past + 1) // 4)
    tm_p = min(512, t_len)
    tm_m = min(256, t_len)

    ones_bd = _block_diag([jnp.ones((HEAD_DIM, HEAD_DIM), BF16)] * N_HEADS)
    pos_p = jnp.arange(t_len)
    pos_s = jnp.full((nb,), past)
    att_tab_p, att_tab_s = _rope_tables(pos_p, ROT_DIM, ROPE_THETA), _rope_tables(pos_s, ROT_DIM, ROPE_THETA)
    ret_tab_p, ret_tab_s = _rope_tables(pos_p, HEAD_DIM, RET_THETA), _rope_tables(pos_s, HEAD_DIM, RET_THETA)
    dmask, qdec, kdec, sdec_rows = _ret_decay(math.gcd(t_len, RET_CHUNK))
    dec_p = (dmask, qdec, kdec, jnp.broadcast_to(sdec_rows[:, None], (BRANCH_W, HEAD_DIM)))
    gamma_s = _ret_decay(1)[3].reshape(1, BRANCH_W)

    n_pool = cache_k.shape[1]
    cki_t = cache_kidx.transpose(0, 1, 3, 2)
    ck_t = cache_k.transpose(0, 1, 3, 4, 2).reshape(-1, n_pool, BRANCH_W, page)
    cv_t = cache_v.transpose(0, 1, 3, 4, 2).reshape(-1, n_pool, BRANCH_W, page)

    xp = x_prompt.reshape(n_seq * t_len, D_MODEL)
    xs = x_sample.reshape(nb, D_MODEL)
    zero_shift = jnp.zeros((n_seq, 1, RWKV_P), F32)
    zero_wkv = jnp.zeros((n_seq, HEAD_DIM, BRANCH_W), F32)
    prompt_new = [[] for _ in range(7)]
    sample_new = [[] for _ in range(7)]
    o_att, o_ret, o_pool = RWKV_P, RWKV_P + ATT_P, RWKV_P + ATT_P + RET_P
    o_gate = o_pool + BRANCH_W
    row1 = lambda a: a.reshape(1, -1)

    for l in range(DEPTH):
        w = w_in[l]
        w_pad = jnp.concatenate(
            [w[:, o_gate:], w[:, :RWKV_P], w[:, o_ret:o_pool], w[:, o_pool:o_gate], w[:, o_att:o_ret],
             jnp.zeros((D_MODEL, ATT_P_PAD - ATT_P), F32)], axis=1).astype(BF16)
        lw = dict(mu=row1(rwkv_mu[l]), w0=row1(rwkv_w0[l]),
                  w_wa=_block_diag([rwkv_w_w2[l], rwkv_w_a2[l]]).astype(BF16),
                  a0=row1(rwkv_a0[l]), w_g2=rwkv_w_g2[l].astype(BF16), k_k=row1(rwkv_k_k[l]),
                  k_a=row1(rwkv_k_a[l]), r_k=row1(rwkv_r_k[l]), ln_w=row1(rwkv_ln_w[l]), ln_b=row1(rwkv_ln_b[l]))
        wp_bd = _block_diag([pool_w[l, g] for g in range(N_HEADS)]).astype(BF16)
        wb = w_branch[l].astype(BF16)
        wo = w_out[l].astype(BF16)
        wfi = w_ffn_in[l].astype(BF16)
        wfo = w_ffn_out[l].astype(BF16)

        p = _norm_matmul(xp, norm1_w[l], w_pad, tm_p, P_PAD // 5)
        o_a, wkv = _rwkv_prompt(p, lw, ones_bd, n_seq, t_len)
        q, k, v, ki, kb, vt, qib, kib, wit = _att_prep(p, att_tab_p, q_norm_w[l], k_norm_w[l], idx_k_norm_w[l],
                                                     ones_bd, n_seq, t_len, tm_p)
        o_b = _dsa_prompt(q, qib, wit, kb, vt, kib, n_seq, t_len)
        o_c, ret = _ret_prompt(p, ret_tab_p, dec_p, ret_gn_w[l], ret_gn_b[l], ones_bd, n_seq, t_len)
        o_d, pbuf = _pool_prompt(p, wp_bd, pool_scale[l], n_seq, t_len, tm_p)
        x1 = _merge(xp, (o_a, o_b, o_c, o_d), p, wb, wo, tm_m)
        xp = _ffn(x1, norm2_w[l], wfi, wfo, tm_m)
        st = (k.reshape(n_seq, t_len, N_HEADS, HEAD_DIM), v.reshape(n_seq, t_len, N_HEADS, HEAD_DIM),
              ki[:, :HEAD_DIM].reshape(n_seq, t_len, HEAD_DIM),
              wkv.reshape(n_seq, HEAD_DIM, N_HEADS, HEAD_DIM).transpose(0, 2, 3, 1),
              p.reshape(n_seq, t_len, P_PAD)[:, -1, COL_RWKV:COL_RWKV + RWKV_P],
              ret.reshape(n_seq, N_HEADS, HEAD_DIM, HEAD_DIM),
              pbuf[:, POOL_HIST - POOL_BUF:, :])
        for lst, s in zip(prompt_new, st):
            lst.append(s)

        ps = _norm_matmul(xs, norm1_w[l], w_pad, nb, P_PAD // 5)
        wkv0 = state_wkv[l].transpose(0, 3, 1, 2).reshape(nb, HEAD_DIM, BRANCH_W)
        o_a, wkv = _rwkv(ps, state_shift[l], wkv0, lw, ones_bd, 1, nb, False)
        q, k, v, ki, kb, vt, qib, kib, wit = _att_prep(ps, att_tab_s, q_norm_w[l], k_norm_w[l], idx_k_norm_w[l],
                                                     ones_bd, 1, nb, nb)
        wi_rows = wit[0].T
        scores = _dsa_sample_scores(page_table, qib.reshape(nb, N_HEADS, HEAD_DIM),
                                    wi_rows[:, :N_HEADS].reshape(nb, N_HEADS, 1), cki_t, l)
        sel, selself = _dsa_sample_select(scores.reshape(nb, past), qib, kib, wi_rows, n_sel_sample)
        o_b = _dsa_sample_attend(page_table, q, sel, selself, k, v, ck_t, cv_t, l).reshape(nb, BRANCH_W)
        ret0 = state_ret[l].transpose(0, 2, 1, 3).reshape(nb, HEAD_DIM, BRANCH_W)
        o_c, ret = _ret_sample(ps, ret_tab_s, ret0, gamma_s, ret_gn_w[l], ret_gn_b[l], ones_bd)
        o_d, pbuf = _pool_sample(ps, state_pool[l].transpose(1, 0, 2), wp_bd, pool_scale[l], past)
        x1 = _merge(xs, (o_a, o_b, o_c, o_d), ps, wb, wo, nb)
        xs = _ffn(x1, norm2_w[l], wfi, wfo, nb)
        st = (k.reshape(nb, 1, N_HEADS, HEAD_DIM), v.reshape(nb, 1, N_HEADS, HEAD_DIM),
              ki[:, :HEAD_DIM].reshape(nb, 1, HEAD_DIM),
              wkv.reshape(nb, HEAD_DIM, N_HEADS, HEAD_DIM).transpose(0, 2, 3, 1),
              ps[:, COL_RWKV:COL_RWKV + RWKV_P],
              ret.reshape(nb, HEAD_DIM, N_HEADS, HEAD_DIM).transpose(0, 2, 1, 3),
              pbuf.transpose(1, 0, 2))
        for lst, s in zip(sample_new, st):
            lst.append(s)

    outs_p = [jnp.stack(a) for a in prompt_new]
    outs_s = [jnp.stack(a) for a in sample_new]
    return (xp.reshape(n_seq, t_len, D_MODEL), xs.reshape(nb, 1, D_MODEL), *outs_p, *outs_s)
```

```python
import functools
import math

import jax
import jax.numpy as jnp
from jax import lax
from jax.experimental import pallas as pl
from jax.experimental.pallas import tpu as pltpu

F32 = jnp.float32
BF16 = jnp.bfloat16
I32 = jnp.int32

D_MODEL = 1024
DEPTH = 4
HEAD_DIM = 64
N_HEADS = 4
BRANCH_W = N_HEADS * HEAD_DIM
N_BRANCH = 4
RWKV_DECAY_LORA = 64
RWKV_A_LORA = 64
RWKV_G_LORA = 128
RWKV_P = 3 * BRANCH_W + RWKV_DECAY_LORA + RWKV_A_LORA + RWKV_G_LORA
RWKV_LN_EPS = 64e-5
ATT_P = 3 * BRANCH_W + N_HEADS * HEAD_DIM + HEAD_DIM + N_HEADS
ATT_P_PAD = 1280
TOPK_MAX = 256
ROPE_THETA = 500000.0
ROT_DIM = HEAD_DIM // 4
ATT_SCALE = HEAD_DIM ** -0.5
IDX_SCALE = (N_HEADS * HEAD_DIM) ** -0.5
RET_P = 4 * BRANCH_W
RET_CHUNK = 128
RET_THETA = 10000.0
RET_KSCALE = HEAD_DIM ** -0.5
POOL_WINDOWS = (2, 4, 8, 16)
POOL_BUF = 15
GATE_P = N_BRANCH * D_MODEL
D_FF = -(-8 * D_MODEL // 768) * 256
NORM_EPS = 1e-6
GN_EPS = 1e-5

COL_GATE = 0
COL_RWKV = GATE_P
COL_RET = COL_RWKV + RWKV_P
COL_POOL = COL_RET + RET_P
COL_ATT = COL_POOL + BRANCH_W
P_PAD = COL_ATT + ATT_P_PAD

LANES = 128
INT_MIN = -2 ** 31
NEG_BIG = -1e30
VMEM_LIMIT = 56 * 1024 * 1024


def _cparams(n_axes):
    return pltpu.CompilerParams(dimension_semantics=("arbitrary",) * n_axes,
                                vmem_limit_bytes=VMEM_LIMIT)


def _dot(a, b):
    return jnp.dot(a, b, preferred_element_type=F32)


def _dot_nt(a, b):
    return lax.dot_general(a, b, (((1,), (1,)), ((), ())), preferred_element_type=F32)


def _seg_sum(x, ones_bd):
    hi = x.astype(BF16)
    lo = (x - hi.astype(F32)).astype(BF16)
    return _dot(hi, ones_bd) + _dot(lo, ones_bd)


def _head_group_norm(x, ones_bd, w, b, eps):
    mean = _seg_sum(x, ones_bd) * (1.0 / HEAD_DIM)
    xc = x - mean
    var = _seg_sum(xc * xc, ones_bd) * (1.0 / HEAD_DIM)
    return xc * lax.rsqrt(var + eps) * w + b


def _rope(x, cos, sin_lo, sin_hi, half):
    n = x.shape[-1]
    return x * cos + pltpu.roll(x, n - half, 1) * sin_lo + pltpu.roll(x, half, 1) * sin_hi


def _norm_matmul_kernel(x_ref, nw_ref, w_ref, o_ref, h_ref):
    @pl.when(pl.program_id(1) == 0)
    def _():
        x = x_ref[...]
        y = x * lax.rsqrt(jnp.mean(x * x, axis=-1, keepdims=True) + NORM_EPS)
        h_ref[...] = (y * nw_ref[...]).astype(BF16)

    o_ref[...] = _dot_nt(h_ref[...], w_ref[...])


def _norm_matmul(x, norm_w, w_t_bf16, tm, tn):
    n, d = x.shape
    p = w_t_bf16.shape[0]
    return pl.pallas_call(
        _norm_matmul_kernel,
        grid=(n // tm, p // tn),
        in_specs=[pl.BlockSpec((tm, d), lambda i, j: (i, 0)),
                  pl.BlockSpec((1, d), lambda i, j: (0, 0)),
                  pl.BlockSpec((tn, d), lambda i, j: (j, 0))],
        out_specs=pl.BlockSpec((tm, tn), lambda i, j: (i, j)),
        out_shape=jax.ShapeDtypeStruct((n, p), F32),
        scratch_shapes=[pltpu.VMEM((tm, d), BF16)],
        compiler_params=_cparams(2),
        name="norm_in_proj",
    )(x, norm_w.reshape(1, d), w_t_bf16)


def _merge_kernel(x_ref, oa_ref, ob_ref, oc_ref, od_ref, gate_ref, wb_ref, wo_ref, out_ref):
    acc = None
    for b, o_ref in enumerate((oa_ref, ob_ref, oc_ref, od_ref)):
        up = _dot(o_ref[...].astype(BF16), wb_ref[b])
        g = jax.nn.sigmoid(gate_ref[:, b * D_MODEL:(b + 1) * D_MODEL])
        acc = g * up if acc is None else acc + g * up
    out_ref[...] = x_ref[...] + _dot(acc.astype(BF16), wo_ref[...])


def _merge(x, branches, p, w_branch_bf16, w_out_bf16, tm):
    n = x.shape[0]
    row = lambda w: pl.BlockSpec((tm, w), lambda i: (i, 0))
    return pl.pallas_call(
        _merge_kernel,
        grid=(n // tm,),
        in_specs=[row(D_MODEL), row(BRANCH_W), row(BRANCH_W), row(BRANCH_W), row(BRANCH_W),
                  pl.BlockSpec((tm, GATE_P), lambda i: (i, COL_GATE // GATE_P)),
                  pl.BlockSpec((N_BRANCH, BRANCH_W, D_MODEL), lambda i: (0, 0, 0)),
                  pl.BlockSpec((D_MODEL, D_MODEL), lambda i: (0, 0))],
        out_specs=row(D_MODEL),
        out_shape=jax.ShapeDtypeStruct((n, D_MODEL), F32),
        compiler_params=_cparams(1),
        name="gated_merge_out_proj",
    )(x, *branches, p, w_branch_bf16, w_out_bf16)


FF_CHUNK = D_FF // 2


def _ffn_kernel(x_ref, nw_ref, wi_ref, wo_ref, out_ref):
    x = x_ref[...]
    h = (x * lax.rsqrt(jnp.mean(x * x, axis=-1, keepdims=True) + NORM_EPS) * nw_ref[...]).astype(BF16)
    acc = x
    for c in range(D_FF // FF_CHUNK):
        g = _dot(h, wi_ref[:, c * FF_CHUNK:(c + 1) * FF_CHUNK])
        u = _dot(h, wi_ref[:, D_FF + c * FF_CHUNK:D_FF + (c + 1) * FF_CHUNK])
        a = (g * jax.nn.sigmoid(g) * u).astype(BF16)
        acc = acc + _dot(a, wo_ref[c * FF_CHUNK:(c + 1) * FF_CHUNK, :])
    out_ref[...] = acc


def _ffn(x, norm_w, w_in_bf16, w_out_bf16, tm):
    n = x.shape[0]
    return pl.pallas_call(
        _ffn_kernel,
        grid=(n // tm,),
        in_specs=[pl.BlockSpec((tm, D_MODEL), lambda i: (i, 0)),
                  pl.BlockSpec((1, D_MODEL), lambda i: (0, 0)),
                  pl.BlockSpec((D_MODEL, 2 * D_FF), lambda i: (0, 0)),
                  pl.BlockSpec((D_FF, D_MODEL), lambda i: (0, 0))],
        out_specs=pl.BlockSpec((tm, D_MODEL), lambda i: (i, 0)),
        out_shape=jax.ShapeDtypeStruct((n, D_MODEL), F32),
        compiler_params=_cparams(1),
        name="swiglu_ffn",
    )(x, norm_w.reshape(1, D_MODEL), w_in_bf16, w_out_bf16)


def _att_prep_kernel(p_ref, cos_ref, slo_ref, shi_ref, qnw_ref, knw_ref, inw_ref, ones_ref,
                     q_ref, k_ref, v_ref, ki_ref, kb_ref, vt_ref, qib_ref, kib_ref, wit_ref):
    cos, slo, shi = cos_ref[...], slo_ref[...], shi_ref[...]
    ones_bd = ones_ref[...]
    half = ROT_DIM // 2

    def head_rms(x, w):
        ms = _seg_sum(x * x, ones_bd) * (1.0 / HEAD_DIM)
        return x * lax.rsqrt(ms + NORM_EPS) * w

    q = _rope(head_rms(p_ref[:, 0:256], qnw_ref[...]), cos, slo, shi, half)
    k = _rope(head_rms(p_ref[:, 256:512], knw_ref[...]), cos, slo, shi, half)
    v = p_ref[:, 512:768]
    qi = _rope(p_ref[:, 768:1024], cos, slo, shi, half)
    tail = p_ref[:, 1024:1152]
    lane = lax.broadcasted_iota(I32, tail.shape, 1)
    is_ki = lane < HEAD_DIM
    kiraw = jnp.where(is_ki, tail, 0.0)
    ms = jnp.sum(kiraw * kiraw, axis=-1, keepdims=True) * (1.0 / HEAD_DIM)
    kin = kiraw * lax.rsqrt(ms + NORM_EPS) * inw_ref[...]
    ki = _rope(kin, cos[:, :LANES], slo[:, :LANES], shi[:, :LANES], half)
    ki = jnp.where(is_ki, ki, 0.0)

    q_ref[...] = q * ATT_SCALE
    k_ref[...] = k
    v_ref[...] = v
    ki_ref[...] = ki
    kb_ref[...] = k.astype(BF16)
    vt_ref[0] = v.T.astype(BF16)
    qib_ref[...] = qi.astype(BF16)
    kib_ref[...] = ki.astype(BF16)
    wi = jnp.where(is_ki, 0.0, tail) * IDX_SCALE
    wit_ref[0] = pltpu.roll(wi, LANES - HEAD_DIM, 1).T[0:8, :]


def _att_prep(p, tables, q_norm_w, k_norm_w, idx_k_norm_w, ones_bd, n_seq, t_len, tm):
    n = p.shape[0]
    tpb = t_len // tm
    cos, slo, shi = tables
    tile4 = lambda w: jnp.tile(w.reshape(1, HEAD_DIM), (1, N_HEADS))
    inw = jnp.concatenate([idx_k_norm_w.reshape(1, HEAD_DIM), jnp.ones((1, LANES - HEAD_DIM), F32)], axis=1)
    row = lambda w: pl.BlockSpec((tm, w), lambda i: (i, 0))
    tab = pl.BlockSpec((tm, BRANCH_W), lambda i: (i % tpb, 0))
    full = lambda a, b: pl.BlockSpec((a, b), lambda i: (0, 0))
    return pl.pallas_call(
        _att_prep_kernel,
        grid=(n // tm,),
        in_specs=[pl.BlockSpec((tm, ATT_P_PAD), lambda i: (i, COL_ATT // ATT_P_PAD)),
                  tab, tab, tab, full(1, BRANCH_W), full(1, BRANCH_W), full(1, LANES),
                  full(BRANCH_W, BRANCH_W)],
        out_specs=[row(BRANCH_W), row(BRANCH_W), row(BRANCH_W), row(LANES), row(BRANCH_W),
                   pl.BlockSpec((1, BRANCH_W, tm), lambda i: (i // tpb, 0, i % tpb)),
                   row(BRANCH_W), row(LANES),
                   pl.BlockSpec((1, 8, tm), lambda i: (i // tpb, 0, i % tpb))],
        out_shape=[jax.ShapeDtypeStruct((n, BRANCH_W), F32),
                   jax.ShapeDtypeStruct((n, BRANCH_W), F32),
                   jax.ShapeDtypeStruct((n, BRANCH_W), F32),
                   jax.ShapeDtypeStruct((n, LANES), F32),
                   jax.ShapeDtypeStruct((n, BRANCH_W), BF16),
                   jax.ShapeDtypeStruct((n_seq, BRANCH_W, t_len), BF16),
                   jax.ShapeDtypeStruct((n, BRANCH_W), BF16),
                   jax.ShapeDtypeStruct((n, LANES), BF16),
                   jax.ShapeDtypeStruct((n_seq, 8, t_len), F32)],
        compiler_params=_cparams(1),
        name="dsa_prep",
    )(p, cos, slo, shi, tile4(q_norm_w), tile4(k_norm_w), inw, ones_bd)


QB = 128


def _float_key(x):
    b = lax.bitcast_convert_type(x, I32)
    b = jnp.where(b == INT_MIN, 0, b)
    return jnp.where(b < 0, b ^ jnp.int32(0x7FFFFFFF), b)


KT = 512
BITS_PER_CHECK = 4


def _dsa_prompt_kernel(n_sel, q_ref, qib_ref, wit_ref, kb_ref, vt_ref, kib_ref, o_ref,
                       key_ref, qbd_ref, acc_ref, m_ref, l_ref):
    j = pl.program_id(1)
    nkt = (j + KT // QB) // (KT // QB)
    q0 = j * QB

    @pl.when(jnp.logical_and(pl.program_id(0) == 0, j == 0))
    def _():
        qbd_ref[...] = jnp.zeros_like(qbd_ref)

    qib = qib_ref[...]
    qis = jnp.concatenate([qib[:, h * HEAD_DIM:(h + 1) * HEAD_DIM] for h in range(N_HEADS)], axis=0)
    qt = q_ref[...].T.astype(BF16)
    for h in range(N_HEADS):
        qbd_ref[h * HEAD_DIM:(h + 1) * HEAD_DIM, h * QB:(h + 1) * QB] = qt[h * HEAD_DIM:(h + 1) * HEAD_DIM, :]
    wit = wit_ref[0]
    srow = lax.broadcasted_iota(I32, (QB, QB), 0)
    tcol = lax.broadcasted_iota(I32, (QB, QB), 1) + q0
    tq = lax.broadcasted_iota(I32, (1, QB), 1) + q0

    def score_tile(kt, carry):
        subs = range(KT // QB)
        k0 = [pl.multiple_of(kt * KT + u * QB, QB) for u in subs]
        s = [_dot_nt(kib_ref[0, pl.ds(k0[u], QB), :][:, :HEAD_DIM], qis) for u in subs]
        for u in subs:
            acc = jnp.zeros((QB, QB), F32)
            for h in range(N_HEADS):
                acc = acc + jnp.maximum(s[u][:, h * QB:(h + 1) * QB], 0.0) * wit[h:h + 1, :]
            key_ref[pl.ds(k0[u], QB), :] = jnp.where(srow + k0[u] <= tcol, _float_key(acc), jnp.int32(INT_MIN))
        return carry

    lax.fori_loop(0, nkt, score_tile, 0)

    def count(pred):
        def body(kt, acc):
            k0 = pl.multiple_of(kt * KT, KT)
            m = pred(key_ref[pl.ds(k0, KT), :]).astype(I32)
            return acc + jnp.sum(m.reshape(KT // 8, 8, QB), axis=0)
        acc = lax.fori_loop(0, nkt, body, jnp.zeros((8, QB), I32))
        return jnp.sum(acc, axis=0, keepdims=True)

    def unfinished(st):
        return jnp.logical_and(st[0] < 32, jnp.logical_not(st[3]))

    def bit_step(st):
        i, tau, done, _ = st
        for _ in range(BITS_PER_CHECK):
            cand = tau + (jnp.int32(1) << (31 - i))
            c = count(lambda blk: blk >= cand)
            tau = jnp.where(jnp.logical_and(done == 0, c >= n_sel), cand, tau)
            done = jnp.where(c == n_sel, 1, done)
            i = i + 1
        return i, tau, done, jnp.min(done) > 0

    done0 = jnp.where(tq + 1 <= n_sel, 1, 0)
    _, tau, done, all_done = lax.while_loop(
        unfinished, bit_step, (jnp.int32(0), jnp.full((1, QB), INT_MIN, I32), done0, jnp.min(done0) > 0))

    @pl.when(jnp.logical_not(all_done))
    def _():
        cnt_gt = count(lambda blk: blk > tau)
        cnt_ge = count(lambda blk: blk >= tau)
        need = (n_sel - cnt_gt).astype(F32)
        excess = jnp.max(jnp.where(tau > INT_MIN, cnt_ge, 0)) > n_sel

        @pl.when(excess)
        def _():
            ltri = (lax.broadcasted_iota(I32, (QB, QB), 1) < srow).astype(BF16)

            def demote(kb, seen):
                k0 = pl.multiple_of(kb * QB, QB)
                blk = key_ref[pl.ds(k0, QB), :]
                tie = blk == tau
                tief = jnp.where(tie, 1.0, 0.0)
                rank = _dot(ltri, tief.astype(BF16)) + seen
                key_ref[pl.ds(k0, QB), :] = jnp.where(jnp.logical_and(tie, rank >= need),
                                                      jnp.int32(INT_MIN), blk)
                return seen + jnp.sum(tief, axis=0, keepdims=True)

            lax.fori_loop(0, j + 1, demote, jnp.zeros((1, QB), F32))

    thr = jnp.maximum(tau, INT_MIN + 1)
    acc_ref[...] = jnp.zeros_like(acc_ref)
    m_ref[...] = jnp.full_like(m_ref, NEG_BIG)
    l_ref[...] = jnp.zeros_like(l_ref)

    def attend(kt, carry):
        k0 = pl.multiple_of(kt * KT, KT)
        sel = key_ref[pl.ds(k0, KT), :] >= thr
        lg = _dot(kb_ref[0, pl.ds(k0, KT), :], qbd_ref[...])
        heads = range(N_HEADS)
        hs = [slice(h * HEAD_DIM, (h + 1) * HEAD_DIM) for h in heads]
        lh = [jnp.where(sel, lg[:, h * QB:(h + 1) * QB], NEG_BIG) for h in heads]
        m_old = [m_ref[h:h + 1, :] for h in heads]
        m_new = [jnp.maximum(m_old[h], jnp.max(lh[h], axis=0, keepdims=True)) for h in heads]
        p = [jnp.exp(lh[h] - m_new[h]) for h in heads]
        pv = [_dot(vt_ref[0, hs[h], pl.ds(k0, KT)], p[h].astype(BF16)) for h in heads]
        alpha = [jnp.exp(m_old[h] - m_new[h]) for h in heads]
        for h in heads:
            l_ref[h:h + 1, :] = alpha[h] * l_ref[h:h + 1, :] + jnp.sum(p[h], axis=0, keepdims=True)
            m_ref[h:h + 1, :] = m_new[h]
            acc_ref[hs[h], :] = acc_ref[hs[h], :] * alpha[h] + pv[h]
        return carry

    lax.fori_loop(0, nkt, attend, 0)

    outs = [acc_ref[h * HEAD_DIM:(h + 1) * HEAD_DIM, :] / l_ref[h:h + 1, :] for h in range(N_HEADS)]
    o_ref[...] = jnp.concatenate(outs, axis=0).T


def _dsa_prompt(q, qib, wit, kb, vt, kib, n_seq, t_len):
    assert t_len % KT == 0
    n_sel = min(TOPK_MAX, t_len // 4)
    nblk = t_len // QB
    return pl.pallas_call(
        functools.partial(_dsa_prompt_kernel, n_sel),
        grid=(n_seq, nblk),
        in_specs=[pl.BlockSpec((QB, BRANCH_W), lambda b, j: (b * nblk + j, 0)),
                  pl.BlockSpec((QB, BRANCH_W), lambda b, j: (b * nblk + j, 0)),
                  pl.BlockSpec((1, 8, QB), lambda b, j: (b, 0, j)),
                  pl.BlockSpec((1, t_len, BRANCH_W), lambda b, j: (b, 0, 0)),
                  pl.BlockSpec((1, BRANCH_W, t_len), lambda b, j: (b, 0, 0)),
                  pl.BlockSpec((1, t_len, LANES), lambda b, j: (b, 0, 0))],
        out_specs=pl.BlockSpec((QB, BRANCH_W), lambda b, j: (b * nblk + j, 0)),
        out_shape=jax.ShapeDtypeStruct((n_seq * t_len, BRANCH_W), F32),
        scratch_shapes=[pltpu.VMEM((t_len, QB), I32),
                        pltpu.VMEM((BRANCH_W, N_HEADS * QB), BF16),
                        pltpu.VMEM((BRANCH_W, QB), F32),
                        pltpu.VMEM((8, QB), F32),
                        pltpu.VMEM((8, QB), F32)],
        compiler_params=_cparams(2),
        name="dsa_prompt",
    )(q, qib, wit, kb.reshape(n_seq, t_len, BRANCH_W), vt, kib.reshape(n_seq, t_len, LANES))


RW_CHUNK = 128
RW_GROUP = 8


def _col_bcast(xt, i):
    lane = lax.broadcasted_iota(I32, (HEAD_DIM, LANES), 1)
    halves = []
    for pair in range(2):
        r0 = 2 * pair * HEAD_DIM
        c0 = jnp.broadcast_to(xt[r0:r0 + HEAD_DIM, i:i + 1], (HEAD_DIM, LANES))
        c1 = jnp.broadcast_to(xt[r0 + HEAD_DIM:r0 + 2 * HEAD_DIM, i:i + 1], (HEAD_DIM, LANES))
        halves.append(jnp.where(lane < HEAD_DIM, c0, c1))
    return jnp.concatenate(halves, axis=1)


def _group_cols(xt_ref, n_arr, gi):
    sh = lax.rem(LANES - gi * RW_GROUP, LANES)
    return [pltpu.roll(xt_ref[a], sh, 1) for a in range(n_arr)]


def _rwkv_prep(p, prev, mu, w0, w_wa, a0, w_g2, k_k, k_a, r_k, ones_bd):
    xl = p + (prev - p) * mu
    r, k, v = xl[:, 0:256], xl[:, 256:512], xl[:, 512:768]
    wa = xl[:, 768:896]
    lane = lax.broadcasted_iota(I32, wa.shape, 1)
    wa = jnp.where(lane < RWKV_DECAY_LORA, jnp.tanh(wa), wa)
    z = _dot(wa.astype(BF16), w_wa)
    nz = -(w0 + z[:, 0:256])
    w_log = -(jnp.maximum(nz, 0.0) + jnp.log1p(jnp.exp(-jnp.abs(nz)))) - 0.5
    log_decay = -jnp.exp(w_log)
    a = jax.nn.sigmoid(a0 + z[:, 256:512])
    g = _dot(jax.nn.sigmoid(xl[:, 896:1024]).astype(BF16), w_g2)
    kk = k * k_k
    kk = kk / jnp.sqrt(_seg_sum(kk * kk, ones_bd) + 1e-12)
    kp = k * (1.0 + (a - 1.0) * k_a)
    bonus = _seg_sum(r * kp * r_k, ones_bd) * v
    return r, log_decay, kp, v, kk, kk * a, g, bonus


def _rwkv_kernel(sequential, p_ref, prev_ref, st0_ref, mu_ref, w0_ref, wwa_ref, a0_ref, wg2_ref,
                 kk_ref, ka_ref, rk_ref, lnw_ref, lnb_ref, ones_ref,
                 o_ref, st_out_ref, carry_ref, st_ref, xt_ref, v_ref, y_ref):
    c = pl.program_id(1)
    p = p_ref[...]
    if sequential:
        @pl.when(c == 0)
        def _():
            carry_ref[...] = prev_ref[0]
            st_ref[...] = st0_ref[0]

        row = lax.broadcasted_iota(I32, p.shape, 0)
        prev = jnp.where(row == 0, carry_ref[...], pltpu.roll(p, 1, 0))
        carry_ref[...] = p[RW_CHUNK - 1:RW_CHUNK, :]
    else:
        prev = prev_ref[...]
    ones_bd = ones_ref[...]
    r, log_decay, kp, v, kk, kka, g, bonus = _rwkv_prep(
        p, prev, mu_ref[...], w0_ref[...], wwa_ref[...], a0_ref[...], wg2_ref[...],
        kk_ref[...], ka_ref[...], rk_ref[...], ones_bd)
    for a, arr in enumerate((r, jnp.exp(log_decay), kp, kk, kka)):
        xt_ref[a] = arr.T
    v_ref[...] = v

    def group(gi, st):
        rt, wt, kt, kkt, kkat = _group_cols(xt_ref, 5, gi)
        base = pl.multiple_of(gi * RW_GROUP, RW_GROUP)
        vrows = v_ref[pl.ds(base, RW_GROUP), :]
        ys = []
        for i in range(RW_GROUP):
            if not sequential:
                st = st0_ref[base + i]
            sa = jnp.sum(st * _col_bcast(kkt, i), axis=0, keepdims=True)
            st = st * _col_bcast(wt, i) - _col_bcast(kkat, i) * sa + _col_bcast(kt, i) * vrows[i:i + 1, :]
            ys.append(jnp.sum(st * _col_bcast(rt, i), axis=0, keepdims=True))
            if not sequential:
                st_out_ref[base + i] = st
        y_ref[pl.ds(base, RW_GROUP), :] = jnp.concatenate(ys, axis=0)
        return st

    if sequential:
        st = lax.fori_loop(0, RW_CHUNK // RW_GROUP, group, st_ref[...])
        st_ref[...] = st

        @pl.when(c == pl.num_programs(1) - 1)
        def _():
            st_out_ref[0] = st
    else:
        lax.fori_loop(0, RW_CHUNK // RW_GROUP, group, jnp.zeros((HEAD_DIM, BRANCH_W), F32))

    o = _head_group_norm(y_ref[...], ones_bd, lnw_ref[...], lnb_ref[...], RWKV_LN_EPS)
    o_ref[...] = (o + bonus) * g


def _rwkv(p, prev, st0, lw, ones_bd, n_seq, t_len, sequential):
    n = n_seq * t_len
    nchunk = t_len // RW_CHUNK if sequential else 1
    full2 = lambda a, b: pl.BlockSpec((a, b), lambda s, c: (0, 0))
    if sequential:
        prev_spec = pl.BlockSpec((1, 1, RWKV_P), lambda s, c: (s, 0, 0))
        st_spec = pl.BlockSpec((1, HEAD_DIM, BRANCH_W), lambda s, c: (s, 0, 0))
        grid = (n_seq, nchunk)
    else:
        assert n == RW_CHUNK
        prev_spec = pl.BlockSpec((RW_CHUNK, RWKV_P), lambda s, c: (0, 0))
        st_spec = pl.BlockSpec((RW_CHUNK, HEAD_DIM, BRANCH_W), lambda s, c: (0, 0, 0))
        grid = (1, 1)
    return pl.pallas_call(
        functools.partial(_rwkv_kernel, sequential),
        grid=grid,
        in_specs=[pl.BlockSpec((RW_CHUNK, RWKV_P), lambda s, c: (s * nchunk + c, COL_RWKV // RWKV_P)),
                  prev_spec, st_spec,
                  full2(1, RWKV_P), full2(1, BRANCH_W), full2(LANES, 2 * BRANCH_W), full2(1, BRANCH_W),
                  full2(RWKV_G_LORA, BRANCH_W), full2(1, BRANCH_W), full2(1, BRANCH_W), full2(1, BRANCH_W),
                  full2(1, BRANCH_W), full2(1, BRANCH_W), full2(BRANCH_W, BRANCH_W)],
        out_specs=[pl.BlockSpec((RW_CHUNK, BRANCH_W), lambda s, c: (s * nchunk + c, 0)), st_spec],
        out_shape=[jax.ShapeDtypeStruct((n, BRANCH_W), F32),
                   jax.ShapeDtypeStruct(st0.shape, F32)],
        scratch_shapes=[pltpu.VMEM((1, RWKV_P), F32),
                        pltpu.VMEM((HEAD_DIM, BRANCH_W), F32),
                        pltpu.VMEM((5, BRANCH_W, RW_CHUNK), F32),
                        pltpu.VMEM((RW_CHUNK, BRANCH_W), F32),
                        pltpu.VMEM((RW_CHUNK, BRANCH_W), F32)],
        compiler_params=_cparams(2),
        name="rwkv7_prompt" if sequential else "rwkv7_sample",
    )(p, prev, st0, lw["mu"], lw["w0"], lw["w_wa"], lw["a0"], lw["w_g2"], lw["k_k"], lw["k_a"],
      lw["r_k"], lw["ln_w"], lw["ln_b"], ones_bd)


RC = 64


def _mm(a, b):
    return _dot(a.astype(BF16), b.astype(BF16))


def _rwkv_chunk_kernel(p_ref, mu_ref, w0_ref, wwa_ref, a0_ref, wg2_ref, kk_ref, ka_ref, rk_ref,
                       lnw_ref, lnb_ref, ones_ref, o_ref, st_out_ref, carry_ref, z_ref):
    ns = p_ref.shape[0]
    seqs = range(ns)
    c = pl.program_id(0)

    @pl.when(c == 0)
    def _():
        carry_ref[...] = jnp.zeros_like(carry_ref)
        z_ref[...] = jnp.zeros_like(z_ref)

    rows = ns * RC
    p = p_ref[...].reshape(rows, RWKV_P)
    t_loc = lax.rem(lax.broadcasted_iota(I32, (rows, RWKV_P), 0), RC)
    seq_of_row = lax.broadcasted_iota(I32, (rows, RWKV_P), 0) // RC
    first = carry_ref[0]
    for s in range(1, ns):
        first = jnp.where(seq_of_row == s, carry_ref[s], first)
    prev = jnp.where(t_loc == 0, first, pltpu.roll(p, 1, 0))
    for s in seqs:
        carry_ref[s] = p[s * RC + RC - 1:(s + 1) * RC, :]
    ones_bd = ones_ref[...]
    r, log_decay, kp, v, kk, kka, g, bonus = _rwkv_prep(
        p, prev, mu_ref[...], w0_ref[...], wwa_ref[...], a0_ref[...], wg2_ref[...],
        kk_ref[...], ka_ref[...], rk_ref[...], ones_bd)

    trow = lax.rem(lax.broadcasted_iota(I32, (rows, BRANCH_W), 0), RC)
    cs = log_decay
    for sh in (1, 2, 4, 8, 16, 32):
        cs = cs + jnp.where(trow >= sh, pltpu.roll(cs, sh, 0), 0.0)
    pw = jnp.exp(cs)
    inv_pw = jnp.exp(-cs)
    a_m = -(kk * jnp.exp(cs - log_decay))
    r_m = r * pw
    b_m = kka * inv_pw
    k_m = kp * inv_pw
    sl = lambda x, s: x[s * RC:(s + 1) * RC]
    bkt = [jnp.concatenate([sl(b_m, s), sl(k_m, s)], axis=0).T for s in seqs]
    kbt = [jnp.concatenate([sl(k_m, s), sl(b_m, s)], axis=0).T for s in seqs]
    pwt = [jnp.concatenate([sl(pw, s), sl(pw, s)], axis=0).T for s in seqs]

    lane_head = lax.broadcasted_iota(I32, (RC, BRANCH_W), 1) // HEAD_DIM
    hm = [lane_head == h for h in range(N_HEADS)]
    xstack = [jnp.concatenate([jnp.where(hm[h], sl(a_m, s), 0.0) for h in range(N_HEADS)]
                              + [jnp.where(hm[h], sl(r_m, s), 0.0) for h in range(N_HEADS)], axis=0)
              for s in seqs]
    g1 = [_mm(xstack[s], bkt[s]) for s in seqs]
    g2 = [_mm(xstack[s], kbt[s]) for s in seqs]
    nst = N_HEADS * RC
    t_of_row = lax.rem(lax.broadcasted_iota(I32, (nst, LANES), 0), RC)
    s_of_lane = lax.broadcasted_iota(I32, (nst, LANES), 1)
    strict = s_of_lane < t_of_row
    incl = jnp.logical_and(s_of_lane <= t_of_row, s_of_lane < RC)
    l_st = [jnp.where(strict, g1[s][0:nst], 0.0) for s in seqs]
    m_st = [jnp.where(strict, g2[s][0:nst], 0.0) for s in seqs]
    n_st = [jnp.where(incl, g1[s][nst:2 * nst], 0.0) for s in seqs]
    q_st = [jnp.where(incl, g2[s][nst:2 * nst], 0.0) for s in seqs]

    eye = jnp.where(lax.broadcasted_iota(I32, (RC, LANES), 0) == lax.broadcasted_iota(I32, (RC, LANES), 1),
                    1.0, 0.0)
    lps = [l_st[s][h * RC:(h + 1) * RC] for s in seqs for h in range(N_HEADS)]
    ws = [eye + lp for lp in lps]
    for _ in range(5):
        lps = [_mm(lp[:, :RC], lp) for lp in lps]
        ws = [w_h + _mm(w_h[:, :RC], lp) for w_h, lp in zip(ws, lps)]

    def per_head(stacked):
        out = jnp.where(hm[0], stacked[0:RC], 0.0)
        for h in range(1, N_HEADS):
            out = out + jnp.where(hm[h], stacked[h * RC:(h + 1) * RC], 0.0)
        return out

    row_head = lax.broadcasted_iota(I32, (BRANCH_W, BRANCH_W), 0) // HEAD_DIM
    col_head = lax.broadcasted_iota(I32, (BRANCH_W, BRANCH_W), 1) // HEAD_DIM
    z0 = [z_ref[s] for s in seqs]
    zbd = [jnp.where(row_head == col_head, jnp.concatenate([z0[s]] * N_HEADS, axis=0), 0.0) for s in seqs]
    vs = [sl(v, s) for s in seqs]
    az = [_mm(sl(a_m, s), zbd[s]) for s in seqs]
    mv = [_mm(m_st[s][:, :RC], vs[s]) for s in seqs]
    rz = [_mm(sl(r_m, s), zbd[s]) for s in seqs]
    qv = [_mm(q_st[s][:, :RC], vs[s]) for s in seqs]
    rhs = [az[s] + per_head(mv[s]) for s in seqs]
    wr = [[_mm(ws[s * N_HEADS + h][:, :RC], rhs[s]) for h in range(N_HEADS)] for s in seqs]
    u = [per_head(jnp.concatenate(wr[s], axis=0)) for s in seqs]
    nu = [_mm(n_st[s][:, :RC], u[s]) for s in seqs]
    uv = [jnp.concatenate([u[s], vs[s]], axis=0) for s in seqs]
    upd = [[_mm(bkt[s][h * HEAD_DIM:(h + 1) * HEAD_DIM, :], uv[s]) for h in range(N_HEADS)] for s in seqs]
    y = jnp.concatenate([rz[s] + per_head(nu[s] + qv[s]) for s in seqs], axis=0)
    z_new = [_col_bcast(pwt[s], RC - 1) * (z0[s] + per_head(jnp.concatenate(upd[s], axis=0))) for s in seqs]
    for s in seqs:
        z_ref[s] = z_new[s]

    @pl.when(c == pl.num_programs(0) - 1)
    def _():
        for s in seqs:
            st_out_ref[s] = z_new[s]

    o = (_head_group_norm(y, ones_bd, lnw_ref[...], lnb_ref[...], RWKV_LN_EPS) + bonus) * g
    o_ref[...] = o.reshape(ns, RC, BRANCH_W)


def _rwkv_prompt(p, lw, ones_bd, n_seq, t_len):
    nchunk = t_len // RC
    full2 = lambda a, b: pl.BlockSpec((a, b), lambda c: (0, 0))
    st_spec = pl.BlockSpec((n_seq, HEAD_DIM, BRANCH_W), lambda c: (0, 0, 0))
    o, st = pl.pallas_call(
        _rwkv_chunk_kernel,
        grid=(nchunk,),
        in_specs=[pl.BlockSpec((n_seq, RC, RWKV_P), lambda c: (0, c, COL_RWKV // RWKV_P)),
                  full2(1, RWKV_P), full2(1, BRANCH_W), full2(LANES, 2 * BRANCH_W), full2(1, BRANCH_W),
                  full2(RWKV_G_LORA, BRANCH_W), full2(1, BRANCH_W), full2(1, BRANCH_W), full2(1, BRANCH_W),
                  full2(1, BRANCH_W), full2(1, BRANCH_W), full2(BRANCH_W, BRANCH_W)],
        out_specs=[pl.BlockSpec((n_seq, RC, BRANCH_W), lambda c: (0, c, 0)), st_spec],
        out_shape=[jax.ShapeDtypeStruct((n_seq, t_len, BRANCH_W), F32),
                   jax.ShapeDtypeStruct((n_seq, HEAD_DIM, BRANCH_W), F32)],
        scratch_shapes=[pltpu.VMEM((n_seq, 1, RWKV_P), F32),
                        pltpu.VMEM((n_seq, HEAD_DIM, BRANCH_W), F32)],
        compiler_params=_cparams(1),
        name="rwkv7_prompt",
    )(p.reshape(n_seq, t_len, P_PAD), lw["mu"], lw["w0"], lw["w_wa"], lw["a0"], lw["w_g2"], lw["k_k"], lw["k_a"],
      lw["r_k"], lw["ln_w"], lw["ln_b"], ones_bd)
    return o.reshape(n_seq * t_len, BRANCH_W), st


def _ret_prompt_kernel(p_ref, cos_ref, slo_ref, shi_ref, dmask_ref, qdec_ref, kdec_ref, sdec_ref,
                       gnw_ref, gnb_ref, ones_ref, o_ref, s_out_ref, s_ref):
    c = pl.program_id(1)

    @pl.when(c == 0)
    def _():
        s_ref[...] = jnp.zeros_like(s_ref)

    cos, slo, shi = cos_ref[...], slo_ref[...], shi_ref[...]
    half = HEAD_DIM // 2
    rq = _rope(p_ref[:, 0:256], cos, slo, shi, half)
    rk = _rope(p_ref[:, 256:512], cos, slo, shi, half) * RET_KSCALE
    rv = p_ref[:, 512:768]
    rg = p_ref[:, 768:1024]
    kdt = (rk * kdec_ref[...]).T.astype(BF16)
    qdec = qdec_ref[...]
    outs = []
    for h in range(N_HEADS):
        hs = slice(h * HEAD_DIM, (h + 1) * HEAD_DIM)
        qh, kh, vh = rq[:, hs].astype(BF16), rk[:, hs].astype(BF16), rv[:, hs].astype(BF16)
        att = _dot_nt(qh, kh) * dmask_ref[h]
        s_old = s_ref[hs, :]
        outs.append(_dot(att.astype(BF16), vh) + _dot(qh, s_old.astype(BF16)) * qdec[:, hs])
        s_ref[hs, :] = s_old * sdec_ref[hs, :] + _dot(kdt[hs, :], vh)
    o = jnp.concatenate(outs, axis=1)
    gn = _head_group_norm(o, ones_ref[...], gnw_ref[...], gnb_ref[...], GN_EPS)
    o_ref[...] = rg * jax.nn.sigmoid(rg) * gn

    @pl.when(c == pl.num_programs(1) - 1)
    def _():
        s_out_ref[0] = s_ref[...]


def _ret_prompt(p, tables, dec, gn_w, gn_b, ones_bd, n_seq, t_len):
    nchunk = t_len // RET_CHUNK
    cos, slo, shi = tables
    dmask, qdec, kdec, sdec = dec
    tab = pl.BlockSpec((RET_CHUNK, BRANCH_W), lambda s, c: (c, 0))
    full2 = lambda a, b: pl.BlockSpec((a, b), lambda s, c: (0, 0))
    return pl.pallas_call(
        _ret_prompt_kernel,
        grid=(n_seq, nchunk),
        in_specs=[pl.BlockSpec((RET_CHUNK, RET_P), lambda s, c: (s * nchunk + c, COL_RET // RET_P)),
                  tab, tab, tab,
                  pl.BlockSpec((N_HEADS, RET_CHUNK, RET_CHUNK), lambda s, c: (0, 0, 0)),
                  full2(RET_CHUNK, BRANCH_W), full2(RET_CHUNK, BRANCH_W), full2(BRANCH_W, HEAD_DIM),
                  full2(1, BRANCH_W), full2(1, BRANCH_W), full2(BRANCH_W, BRANCH_W)],
        out_specs=[pl.BlockSpec((RET_CHUNK, BRANCH_W), lambda s, c: (s * nchunk + c, 0)),
                   pl.BlockSpec((1, BRANCH_W, HEAD_DIM), lambda s, c: (s, 0, 0))],
        out_shape=[jax.ShapeDtypeStruct((n_seq * t_len, BRANCH_W), F32),
                   jax.ShapeDtypeStruct((n_seq, BRANCH_W, HEAD_DIM), F32)],
        scratch_shapes=[pltpu.VMEM((BRANCH_W, HEAD_DIM), F32)],
        compiler_params=_cparams(2),
        name="retention_prompt",
    )(p, cos, slo, shi, dmask, qdec, kdec, sdec, gn_w.reshape(1, BRANCH_W), gn_b.reshape(1, BRANCH_W),
      ones_bd)


def _ret_sample_kernel(p_ref, cos_ref, slo_ref, shi_ref, st0_ref, gamma_ref, gnw_ref, gnb_ref, ones_ref,
                       o_ref, st_out_ref, xt_ref, v_ref, y_ref):
    cos, slo, shi = cos_ref[...], slo_ref[...], shi_ref[...]
    half = HEAD_DIM // 2
    rq = _rope(p_ref[:, 0:256], cos, slo, shi, half)
    rk = _rope(p_ref[:, 256:512], cos, slo, shi, half) * RET_KSCALE
    rg = p_ref[:, 768:1024]
    xt_ref[0] = rq.T
    xt_ref[1] = rk.T
    v_ref[...] = p_ref[:, 512:768]
    gamma = gamma_ref[...]

    def group(gi, carry):
        qt, kt = _group_cols(xt_ref, 2, gi)
        base = pl.multiple_of(gi * RW_GROUP, RW_GROUP)
        vrows = v_ref[pl.ds(base, RW_GROUP), :]
        ys = []
        for i in range(RW_GROUP):
            st = st0_ref[base + i] * gamma + _col_bcast(kt, i) * vrows[i:i + 1, :]
            st_out_ref[base + i] = st
            ys.append(jnp.sum(st * _col_bcast(qt, i), axis=0, keepdims=True))
        y_ref[pl.ds(base, RW_GROUP), :] = jnp.concatenate(ys, axis=0)
        return carry

    lax.fori_loop(0, RW_CHUNK // RW_GROUP, group, 0)
    gn = _head_group_norm(y_ref[...], ones_ref[...], gnw_ref[...], gnb_ref[...], GN_EPS)
    o_ref[...] = rg * jax.nn.sigmoid(rg) * gn


def _ret_sample(p, tables, st0, gamma, gn_w, gn_b, ones_bd):
    n = st0.shape[0]
    assert n == RW_CHUNK
    cos, slo, shi = tables
    full2 = lambda a, b: pl.BlockSpec((a, b), lambda i: (0, 0))
    st_spec = pl.BlockSpec((n, HEAD_DIM, BRANCH_W), lambda i: (0, 0, 0))
    return pl.pallas_call(
        _ret_sample_kernel,
        grid=(1,),
        in_specs=[pl.BlockSpec((n, RET_P), lambda i: (0, COL_RET // RET_P)),
                  full2(n, BRANCH_W), full2(n, BRANCH_W), full2(n, BRANCH_W), st_spec,
                  full2(1, BRANCH_W), full2(1, BRANCH_W), full2(1, BRANCH_W), full2(BRANCH_W, BRANCH_W)],
        out_specs=[full2(n, BRANCH_W), st_spec],
        out_shape=[jax.ShapeDtypeStruct((n, BRANCH_W), F32), jax.ShapeDtypeStruct(st0.shape, F32)],
        scratch_shapes=[pltpu.VMEM((2, BRANCH_W, RW_CHUNK), F32),
                        pltpu.VMEM((RW_CHUNK, BRANCH_W), F32),
                        pltpu.VMEM((RW_CHUNK, BRANCH_W), F32)],
        compiler_params=_cparams(1),
        name="retention_sample",
    )(p, cos, slo, shi, st0, gamma, gn_w.reshape(1, BRANCH_W), gn_b.reshape(1, BRANCH_W), ones_bd)


POOL_HIST = 16


def _pool_windows(lane):
    win = jnp.full(lane.shape, float(POOL_WINDOWS[-1]), F32)
    for gi in range(len(POOL_WINDOWS) - 2, -1, -1):
        win = jnp.where(lane < (gi + 1) * HEAD_DIM, float(POOL_WINDOWS[gi]), win)
    return win


def _pool_select(sums, lane):
    out = sums[-1]
    for gi in range(len(POOL_WINDOWS) - 2, -1, -1):
        out = jnp.where(lane < (gi + 1) * HEAD_DIM, sums[gi], out)
    return out


def _pool_prompt_kernel(tm, u_ref, wp_ref, scale_ref, y_ref, nbuf_ref, ext_ref):
    t = pl.program_id(1)

    @pl.when(t == 0)
    def _():
        ext_ref[0:POOL_HIST, :] = jnp.zeros((POOL_HIST, BRANCH_W), F32)

    @pl.when(t > 0)
    def _():
        ext_ref[0:POOL_HIST, :] = ext_ref[tm:tm + POOL_HIST, :]

    u = u_ref[...]
    ext_ref[POOL_HIST:POOL_HIST + tm, :] = u
    sums, acc, i = [], u, 1
    for win in POOL_WINDOWS:
        while i < win:
            acc = acc + ext_ref[pl.ds(POOL_HIST - i, tm), :]
            i += 1
        sums.append(acc)
    lane = lax.broadcasted_iota(I32, u.shape, 1)
    pos = (lax.broadcasted_iota(I32, u.shape, 0) + t * tm).astype(F32)
    cnt = jnp.minimum(pos + 1.0, _pool_windows(lane))
    d = _pool_select(sums, lane) / cnt - u
    y_ref[...] = _dot(d.astype(BF16), wp_ref[...]) * scale_ref[...]

    @pl.when(t == pl.num_programs(1) - 1)
    def _():
        nbuf_ref[0] = ext_ref[tm:tm + POOL_HIST, :]


def _pool_prompt(p, wp_bd, scale, n_seq, t_len, tm):
    tpb = t_len // tm
    return pl.pallas_call(
        functools.partial(_pool_prompt_kernel, tm),
        grid=(n_seq, tpb),
        in_specs=[pl.BlockSpec((tm, BRANCH_W), lambda s, t: (s * tpb + t, COL_POOL // BRANCH_W)),
                  pl.BlockSpec((BRANCH_W, BRANCH_W), lambda s, t: (0, 0)),
                  pl.BlockSpec((1, BRANCH_W), lambda s, t: (0, 0))],
        out_specs=[pl.BlockSpec((tm, BRANCH_W), lambda s, t: (s * tpb + t, 0)),
                   pl.BlockSpec((1, POOL_HIST, BRANCH_W), lambda s, t: (s, 0, 0))],
        out_shape=[jax.ShapeDtypeStruct((n_seq * t_len, BRANCH_W), F32),
                   jax.ShapeDtypeStruct((n_seq, POOL_HIST, BRANCH_W), F32)],
        scratch_shapes=[pltpu.VMEM((tm + POOL_HIST, BRANCH_W), F32)],
        compiler_params=_cparams(2),
        name="pool_prompt",
    )(p, wp_bd, scale.reshape(1, BRANCH_W))


def _pool_sample_kernel(pos0, u_ref, buf_ref, wp_ref, scale_ref, y_ref, nbuf_ref):
    u = u_ref[...]
    sums, acc, i = [], u, 1
    for win in POOL_WINDOWS:
        while i < win:
            acc = acc + buf_ref[POOL_BUF - i]
            i += 1
        sums.append(acc)
    lane = lax.broadcasted_iota(I32, u.shape, 1)
    cnt = jnp.minimum(float(pos0) + 1.0, _pool_windows(lane))
    d = _pool_select(sums, lane) / cnt - u
    y_ref[...] = _dot(d.astype(BF16), wp_ref[...]) * scale_ref[...]
    for r in range(POOL_BUF - 1):
        nbuf_ref[r] = buf_ref[r + 1]
    nbuf_ref[POOL_BUF - 1] = u


def _pool_sample(p, buf_t, wp_bd, scale, pos0):
    n = buf_t.shape[1]
    full3 = pl.BlockSpec((POOL_BUF, n, BRANCH_W), lambda i: (0, 0, 0))
    return pl.pallas_call(
        functools.partial(_pool_sample_kernel, pos0),
        grid=(1,),
        in_specs=[pl.BlockSpec((n, BRANCH_W), lambda i: (0, COL_POOL // BRANCH_W)), full3,
                  pl.BlockSpec((BRANCH_W, BRANCH_W), lambda i: (0, 0)),
                  pl.BlockSpec((1, BRANCH_W), lambda i: (0, 0))],
        out_specs=[pl.BlockSpec((n, BRANCH_W), lambda i: (0, 0)), full3],
        out_shape=[jax.ShapeDtypeStruct((n, BRANCH_W), F32),
                   jax.ShapeDtypeStruct((POOL_BUF, n, BRANCH_W), F32)],
        compiler_params=_cparams(1),
        name="pool_sample",
    )(p, buf_t, wp_bd, scale.reshape(1, BRANCH_W))


def _page_copies(pt_ref, layer, seq, slot, hbm_ref, buf_ref, sem_ref, n_pages):
    return [pltpu.make_async_copy(hbm_ref.at[layer, pt_ref[seq, pg]], buf_ref.at[slot, pg], sem_ref.at[slot])
            for pg in range(n_pages)]


def _paged_prefetch(pt_ref, layer, pairs, n_pages):
    b = pl.program_id(0)
    nb = pl.num_programs(0)
    slot = lax.rem(b, 2)

    @pl.when(b == 0)
    def _():
        for hbm_ref, buf_ref, sem_ref in pairs:
            for cp in _page_copies(pt_ref, layer, 0, 0, hbm_ref, buf_ref, sem_ref, n_pages):
                cp.start()

    @pl.when(b + 1 < nb)
    def _():
        for hbm_ref, buf_ref, sem_ref in pairs:
            for cp in _page_copies(pt_ref, layer, b + 1, 1 - slot, hbm_ref, buf_ref, sem_ref, n_pages):
                cp.start()

    for hbm_ref, buf_ref, sem_ref in pairs:
        for cp in _page_copies(pt_ref, layer, b, slot, hbm_ref, buf_ref, sem_ref, n_pages):
            cp.wait()
    return slot


def _dsa_sample_score_kernel(layer, n_pages, pt_ref, qi_ref, wi_ref, cki_ref, out_ref, buf_ref, sem_ref):
    slot = _paged_prefetch(pt_ref, layer, [(cki_ref, buf_ref, sem_ref)], n_pages)
    qi4, wi4 = qi_ref[0], wi_ref[0]
    rows = []
    for pg in range(n_pages):
        s = jnp.maximum(_dot(qi4, buf_ref[slot, pg].astype(BF16)), 0.0)
        rows.append(jnp.sum(s * wi4, axis=0, keepdims=True))
    out_ref[0] = jnp.concatenate(rows, axis=1)


def _dsa_sample_scores(page_table, qi4, wi4, cache_kidx_t, layer):
    nb, n_pages = page_table.shape
    page = cache_kidx_t.shape[3]
    past = n_pages * page
    return pl.pallas_call(
        functools.partial(_dsa_sample_score_kernel, layer, n_pages),
        grid_spec=pltpu.PrefetchScalarGridSpec(
            num_scalar_prefetch=1,
            grid=(nb,),
            in_specs=[pl.BlockSpec((1, N_HEADS, HEAD_DIM), lambda b, pt: (b, 0, 0)),
                      pl.BlockSpec((1, N_HEADS, 1), lambda b, pt: (b, 0, 0)),
                      pl.BlockSpec(memory_space=pl.ANY)],
            out_specs=pl.BlockSpec((1, 1, past), lambda b, pt: (b, 0, 0)),
            scratch_shapes=[pltpu.VMEM((2, n_pages, HEAD_DIM, page), F32),
                            pltpu.SemaphoreType.DMA((2,))]),
        out_shape=jax.ShapeDtypeStruct((nb, 1, past), F32),
        compiler_params=_cparams(1),
        name="dsa_sample_scores",
    )(page_table, qi4, wi4, cache_kidx_t)


def _dsa_sample_select_kernel(n_sel, sc_ref, qib_ref, kib_ref, wi_ref, sel_ref, selself_ref):
    nb, past = sc_ref.shape
    qi = qib_ref[...].astype(F32)
    ki = kib_ref[...].astype(F32)[:, :HEAD_DIM]
    wi = wi_ref[...]
    s_self = jnp.zeros((nb, 1), F32)
    for h in range(N_HEADS):
        dot_h = jnp.sum(qi[:, h * HEAD_DIM:(h + 1) * HEAD_DIM] * ki, axis=-1, keepdims=True)
        s_self = s_self + jnp.maximum(dot_h, 0.0) * wi[:, h:h + 1]
    key = _float_key(sc_ref[...] + 0.0)
    key_self = _float_key(s_self)

    def count(pred):
        return (jnp.sum(pred(key).astype(I32), axis=-1, keepdims=True) + pred(key_self).astype(I32))

    def bit_step(i, tau):
        cand = tau + (jnp.int32(1) << (31 - i))
        return jnp.where(count(lambda x: x >= cand) >= n_sel, cand, tau)

    tau = lax.fori_loop(0, 32, bit_step, jnp.full((nb, 1), INT_MIN, I32))
    gt = key > tau
    tie = key == tau
    need = (n_sel - count(lambda x: x > tau)).astype(F32)
    utri = (lax.broadcasted_iota(I32, (LANES, LANES), 0)
            < lax.broadcasted_iota(I32, (LANES, LANES), 1)).astype(BF16)
    seen = jnp.zeros((nb, 1), F32)
    for c in range(past // LANES):
        cs = slice(c * LANES, (c + 1) * LANES)
        tief = jnp.where(tie[:, cs], 1.0, 0.0)
        rank = _dot(tief.astype(BF16), utri) + seen
        keep = jnp.logical_or(gt[:, cs], jnp.logical_and(tie[:, cs], rank < need))
        sel_ref[:, cs] = jnp.where(keep, 1.0, 0.0)
        seen = seen + jnp.sum(tief, axis=-1, keepdims=True)
    keep_self = jnp.logical_or(key_self > tau, jnp.logical_and(key_self == tau, seen < need))
    selself_ref[...] = jnp.where(keep_self, 1.0, 0.0)


def _dsa_sample_select(scores, qib, kib, wi, n_sel):
    nb, past = scores.shape
    full2 = lambda a, b: pl.BlockSpec((a, b), lambda i: (0, 0))
    return pl.pallas_call(
        functools.partial(_dsa_sample_select_kernel, n_sel),
        grid=(1,),
        in_specs=[full2(nb, past), full2(nb, BRANCH_W), full2(nb, LANES), full2(nb, 8)],
        out_specs=[full2(nb, past), full2(nb, 1)],
        out_shape=[jax.ShapeDtypeStruct((nb, past), F32), jax.ShapeDtypeStruct((nb, 1), F32)],
        compiler_params=_cparams(1),
        name="dsa_sample_select",
    )(scores, qib, kib, wi)


def _dsa_sample_attend_kernel(layer, n_pages, pt_ref, q_ref, sel_ref, selself_ref, kself_ref, vself_ref,
                              ck_ref, cv_ref, o_ref, kbuf_ref, vbuf_ref, ksem_ref, vsem_ref):
    slot = _paged_prefetch(pt_ref, layer, [(ck_ref, kbuf_ref, ksem_ref), (cv_ref, vbuf_ref, vsem_ref)], n_pages)
    head_of_lane = lax.broadcasted_iota(I32, (N_HEADS, BRANCH_W), 1) // HEAD_DIM
    own = head_of_lane == lax.broadcasted_iota(I32, (N_HEADS, BRANCH_W), 0)
    qrows = jnp.where(own, jnp.broadcast_to(q_ref[0], (N_HEADS, BRANCH_W)), 0.0).astype(BF16)
    lg = jnp.concatenate([_dot(qrows, kbuf_ref[slot, pg].astype(BF16)) for pg in range(n_pages)], axis=1)
    lg = jnp.where(sel_ref[0] > 0.0, lg, NEG_BIG)
    kself = kself_ref[0].astype(BF16).astype(F32)
    lself = jnp.sum(qrows.astype(F32) * kself, axis=-1, keepdims=True)
    lself = jnp.where(selself_ref[0] > 0.0, lself, NEG_BIG)
    m = jnp.maximum(jnp.max(lg, axis=-1, keepdims=True), lself)
    p = jnp.exp(lg - m)
    pself = jnp.exp(lself - m)
    denom = jnp.sum(p, axis=-1, keepdims=True) + pself
    p = p.astype(BF16)
    page = kbuf_ref.shape[3]
    o4 = pself * vself_ref[0]
    for pg in range(n_pages):
        o4 = o4 + _dot_nt(p[:, pg * page:(pg + 1) * page], vbuf_ref[slot, pg].astype(BF16))
    o_ref[0] = jnp.sum(jnp.where(own, o4 / denom, 0.0), axis=0, keepdims=True)


def _dsa_sample_attend(page_table, q, sel, selself, kself, vself, cache_k_t, cache_v_t, layer):
    nb, n_pages = page_table.shape
    page = cache_k_t.shape[3]
    past = n_pages * page
    row3 = lambda w: pl.BlockSpec((1, 1, w), lambda b, pt: (b, 0, 0))
    return pl.pallas_call(
        functools.partial(_dsa_sample_attend_kernel, layer, n_pages),
        grid_spec=pltpu.PrefetchScalarGridSpec(
            num_scalar_prefetch=1,
            grid=(nb,),
            in_specs=[row3(BRANCH_W), row3(past), row3(1), row3(BRANCH_W), row3(BRANCH_W),
                      pl.BlockSpec(memory_space=pl.ANY), pl.BlockSpec(memory_space=pl.ANY)],
            out_specs=row3(BRANCH_W),
            scratch_shapes=[pltpu.VMEM((2, n_pages, BRANCH_W, page), F32),
                            pltpu.VMEM((2, n_pages, BRANCH_W, page), F32),
                            pltpu.SemaphoreType.DMA((2,)),
                            pltpu.SemaphoreType.DMA((2,))]),
        out_shape=jax.ShapeDtypeStruct((nb, 1, BRANCH_W), F32),
        compiler_params=_cparams(1),
        name="dsa_sample_attend",
    )(page_table, q.reshape(nb, 1, BRANCH_W), sel.reshape(nb, 1, past), selself.reshape(nb, 1, 1),
      kself.reshape(nb, 1, BRANCH_W), vself.reshape(nb, 1, BRANCH_W), cache_k_t, cache_v_t)


def _rope_tables(pos, rot_dim, theta):
    half = rot_dim // 2
    inv = theta ** (-jnp.arange(half, dtype=F32) / half)
    ang = pos.astype(F32)[:, None] * inv[None, :]
    cos, sin = jnp.cos(ang), jnp.sin(ang)
    n = pos.shape[0]
    zh = jnp.zeros((n, half), F32)
    zr = jnp.zeros((n, HEAD_DIM - rot_dim), F32)
    c = jnp.concatenate([cos, cos, jnp.ones((n, HEAD_DIM - rot_dim), F32)], axis=1)
    lo = jnp.concatenate([-sin, zh, zr], axis=1)
    hi = jnp.concatenate([zh, sin, zr], axis=1)
    return tuple(jnp.tile(t, (1, N_HEADS)) for t in (c, lo, hi))


def _ret_decay(chunk):
    log_g = jnp.log1p(-jnp.exp2(-5.0 - jnp.arange(N_HEADS, dtype=F32)))
    n = jnp.arange(chunk, dtype=F32)
    diff = n[:, None] - n[None, :]
    dmask = jnp.where(diff >= 0, jnp.exp(log_g[:, None, None] * jnp.maximum(diff, 0.0)), 0.0)
    q_dec = jnp.exp(log_g[:, None] * (n + 1.0)).T
    k_dec = jnp.exp(log_g[:, None] * (chunk - 1.0 - n))
    s_dec = jnp.exp(log_g * chunk)
    qdec = jnp.repeat(q_dec, HEAD_DIM, axis=1)
    kdec = jnp.repeat(k_dec.T, HEAD_DIM, axis=1)
    sdec_rows = jnp.repeat(s_dec, HEAD_DIM)
    return dmask, qdec, kdec, sdec_rows


def _block_diag(blocks):
    n = len(blocks)
    rows = []
    for i, blk in enumerate(blocks):
        rows.append(jnp.concatenate([blk if j == i else jnp.zeros((blk.shape[0], blocks[j].shape[1]), blk.dtype)
                                     for j in range(n)], axis=1))
    return jnp.concatenate(rows, axis=0)


def kernel(x_prompt, x_sample, cache_k, cache_v, cache_kidx, state_wkv, state_shift, state_ret, state_pool, page_table, norm1_w, w_in, rwkv_mu, rwkv_w0, rwkv_w_w2, rwkv_a0, rwkv_w_a2, rwkv_w_g2, rwkv_k_k, rwkv_k_a, rwkv_r_k, rwkv_ln_w, rwkv_ln_b, q_norm_w, k_norm_w, idx_k_norm_w, ret_gn_w, ret_gn_b, pool_w, pool_scale, w_branch, w_out, norm2_w, w_ffn_in, w_ffn_out):
    n_seq, t_len, _ = x_prompt.shape
    nb = x_sample.shape[0]
    assert x_sample.shape[1] == 1
    page = cache_k.shape[2]
    past = page_table.shape[1] * page
    n_sel_sample = min(TOPK_MAX, (past + 1) // 4)
    tm_p = min(512, t_len)
    tm_m = min(256, t_len)

    ones_bd = _block_diag([jnp.ones((HEAD_DIM, HEAD_DIM), BF16)] * N_HEADS)
    pos_p = jnp.arange(t_len)
    pos_s = jnp.full((nb,), past)
    att_tab_p, att_tab_s = _rope_tables(pos_p, ROT_DIM, ROPE_THETA), _rope_tables(pos_s, ROT_DIM, ROPE_THETA)
    ret_tab_p, ret_tab_s = _rope_tables(pos_p, HEAD_DIM, RET_THETA), _rope_tables(pos_s, HEAD_DIM, RET_THETA)
    dmask, qdec, kdec, sdec_rows = _ret_decay(math.gcd(t_len, RET_CHUNK))
    dec_p = (dmask, qdec, kdec, jnp.broadcast_to(sdec_rows[:, None], (BRANCH_W, HEAD_DIM)))
    gamma_s = _ret_decay(1)[3].reshape(1, BRANCH_W)

    n_pool = cache_k.shape[1]
    cki_t = cache_kidx.transpose(0, 1, 3, 2)
    ck_t = cache_k.transpose(0, 1, 3, 4, 2).reshape(-1, n_pool, BRANCH_W, page)
    cv_t = cache_v.transpose(0, 1, 3, 4, 2).reshape(-1, n_pool, BRANCH_W, page)

    xp = x_prompt.reshape(n_seq * t_len, D_MODEL)
    xs = x_sample.reshape(nb, D_MODEL)
    zero_shift = jnp.zeros((n_seq, 1, RWKV_P), F32)
    zero_wkv = jnp.zeros((n_seq, HEAD_DIM, BRANCH_W), F32)
    prompt_new = [[] for _ in range(7)]
    sample_new = [[] for _ in range(7)]
    o_att, o_ret, o_pool = RWKV_P, RWKV_P + ATT_P, RWKV_P + ATT_P + RET_P
    o_gate = o_pool + BRANCH_W
    row1 = lambda a: a.reshape(1, -1)
    w_in_t = w_in.transpose(2, 0, 1)

    for l in range(DEPTH):
        w = w_in_t[:, l, :]
        w_pad = jnp.concatenate(
            [w[o_gate:], w[:RWKV_P], w[o_ret:o_pool], w[o_pool:o_gate], w[o_att:o_ret],
             jnp.zeros((ATT_P_PAD - ATT_P, D_MODEL), F32)], axis=0).astype(BF16)
        lw = dict(mu=row1(rwkv_mu[l]), w0=row1(rwkv_w0[l]),
                  w_wa=_block_diag([rwkv_w_w2[l], rwkv_w_a2[l]]).astype(BF16),
                  a0=row1(rwkv_a0[l]), w_g2=rwkv_w_g2[l].astype(BF16), k_k=row1(rwkv_k_k[l]),
                  k_a=row1(rwkv_k_a[l]), r_k=row1(rwkv_r_k[l]), ln_w=row1(rwkv_ln_w[l]), ln_b=row1(rwkv_ln_b[l]))
        wp_bd = _block_diag([pool_w[l, g] for g in range(N_HEADS)]).astype(BF16)
        wb = w_branch[l].astype(BF16)
        wo = w_out[l].astype(BF16)
        wfi = w_ffn_in[l].astype(BF16)
        wfo = w_ffn_out[l].astype(BF16)

        p = _norm_matmul(xp, norm1_w[l], w_pad, tm_p, P_PAD // 5)
        o_a, wkv = _rwkv_prompt(p, lw, ones_bd, n_seq, t_len)
        q, k, v, ki, kb, vt, qib, kib, wit = _att_prep(p, att_tab_p, q_norm_w[l], k_norm_w[l], idx_k_norm_w[l],
                                                     ones_bd, n_seq, t_len, tm_p)
        o_b = _dsa_prompt(q, qib, wit, kb, vt, kib, n_seq, t_len)
        o_c, ret = _ret_prompt(p, ret_tab_p, dec_p, ret_gn_w[l], ret_gn_b[l], ones_bd, n_seq, t_len)
        o_d, pbuf = _pool_prompt(p, wp_bd, pool_scale[l], n_seq, t_len, tm_p)
        x1 = _merge(xp, (o_a, o_b, o_c, o_d), p, wb, wo, tm_m)
        xp = _ffn(x1, norm2_w[l], wfi, wfo, tm_m)
        st = (k.reshape(n_seq, t_len, N_HEADS, HEAD_DIM), v.reshape(n_seq, t_len, N_HEADS, HEAD_DIM),
              ki[:, :HEAD_DIM].reshape(n_seq, t_len, HEAD_DIM),
              wkv.reshape(n_seq, HEAD_DIM, N_HEADS, HEAD_DIM).transpose(0, 2, 3, 1),
              p.reshape(n_seq, t_len, P_PAD)[:, -1, COL_RWKV:COL_RWKV + RWKV_P],
              ret.reshape(n_seq, N_HEADS, HEAD_DIM, HEAD_DIM),
              pbuf[:, POOL_HIST - POOL_BUF:, :])
        for lst, s in zip(prompt_new, st):
            lst.append(s)

        ps = _norm_matmul(xs, norm1_w[l], w_pad, nb, P_PAD // 5)
        wkv0 = state_wkv[l].transpose(0, 3, 1, 2).reshape(nb, HEAD_DIM, BRANCH_W)
        o_a, wkv = _rwkv(ps, state_shift[l], wkv0, lw, ones_bd, 1, nb, False)
        q, k, v, ki, kb, vt, qib, kib, wit = _att_prep(ps, att_tab_s, q_norm_w[l], k_norm_w[l], idx_k_norm_w[l],
                                                     ones_bd, 1, nb, nb)
        wi_rows = wit[0].T
        scores = _dsa_sample_scores(page_table, qib.reshape(nb, N_HEADS, HEAD_DIM),
                                    wi_rows[:, :N_HEADS].reshape(nb, N_HEADS, 1), cki_t, l)
        sel, selself = _dsa_sample_select(scores.reshape(nb, past), qib, kib, wi_rows, n_sel_sample)
        o_b = _dsa_sample_attend(page_table, q, sel, selself, k, v, ck_t, cv_t, l).reshape(nb, BRANCH_W)
        ret0 = state_ret[l].transpose(0, 2, 1, 3).reshape(nb, HEAD_DIM, BRANCH_W)
        o_c, ret = _ret_sample(ps, ret_tab_s, ret0, gamma_s, ret_gn_w[l], ret_gn_b[l], ones_bd)
        o_d, pbuf = _pool_sample(ps, state_pool[l].transpose(1, 0, 2), wp_bd, pool_scale[l], past)
        x1 = _merge(xs, (o_a, o_b, o_c, o_d), ps, wb, wo, nb)
        xs = _ffn(x1, norm2_w[l], wfi, wfo, nb)
        st = (k.reshape(nb, 1, N_HEADS, HEAD_DIM), v.reshape(nb, 1, N_HEADS, HEAD_DIM),
              ki[:, :HEAD_DIM].reshape(nb, 1, HEAD_DIM),
              wkv.reshape(nb, HEAD_DIM, N_HEADS, HEAD_DIM).transpose(0, 2, 3, 1),
              ps[:, COL_RWKV:COL_RWKV + RWKV_P],
              ret.reshape(nb, HEAD_DIM, N_HEADS, HEAD_DIM).transpose(0, 2, 1, 3),
              pbuf.transpose(1, 0, 2))
        for lst, s in zip(sample_new, st):
            lst.append(s)

    outs_p = [jnp.stack(a) for a in prompt_new]
    outs_s = [jnp.stack(a) for a in sample_new]
    return (xp.reshape(n_seq, t_len, D_MODEL), xs.reshape(nb, 1, D_MODEL), *outs_p, *outs_s)
```

```python
import functools
import math

import jax
import jax.numpy as jnp
from jax import lax
from jax.experimental import pallas as pl
from jax.experimental.pallas import tpu as pltpu

F32 = jnp.float32
BF16 = jnp.bfloat16
I32 = jnp.int32

D_MODEL = 1024
DEPTH = 4
HEAD_DIM = 64
N_HEADS = 4
BRANCH_W = N_HEADS * HEAD_DIM
N_BRANCH = 4
RWKV_DECAY_LORA = 64
RWKV_A_LORA = 64
RWKV_G_LORA = 128
RWKV_P = 3 * BRANCH_W + RWKV_DECAY_LORA + RWKV_A_LORA + RWKV_G_LORA
RWKV_LN_EPS = 64e-5
ATT_P = 3 * BRANCH_W + N_HEADS * HEAD_DIM + HEAD_DIM + N_HEADS
ATT_P_PAD = 1280
TOPK_MAX = 256
ROPE_THETA = 500000.0
ROT_DIM = HEAD_DIM // 4
ATT_SCALE = HEAD_DIM ** -0.5
IDX_SCALE = (N_HEADS * HEAD_DIM) ** -0.5
RET_P = 4 * BRANCH_W
RET_CHUNK = 128
RET_THETA = 10000.0
RET_KSCALE = HEAD_DIM ** -0.5
POOL_WINDOWS = (2, 4, 8, 16)
POOL_BUF = 15
GATE_P = N_BRANCH * D_MODEL
D_FF = -(-8 * D_MODEL // 768) * 256
NORM_EPS = 1e-6
GN_EPS = 1e-5

COL_GATE = 0
COL_RWKV = GATE_P
COL_RET = COL_RWKV + RWKV_P
COL_POOL = COL_RET + RET_P
COL_ATT = COL_POOL + BRANCH_W
P_PAD = COL_ATT + ATT_P_PAD

LANES = 128
INT_MIN = -2 ** 31
NEG_BIG = -1e30
VMEM_LIMIT = 56 * 1024 * 1024


def _cparams(n_axes):
    return pltpu.CompilerParams(dimension_semantics=("arbitrary",) * n_axes,
                                vmem_limit_bytes=VMEM_LIMIT)


def _dot(a, b):
    return jnp.dot(a, b, preferred_element_type=F32)


def _dot_nt(a, b):
    return lax.dot_general(a, b, (((1,), (1,)), ((), ())), preferred_element_type=F32)


def _seg_sum(x, ones_bd):
    hi = x.astype(BF16)
    lo = (x - hi.astype(F32)).astype(BF16)
    return _dot(hi, ones_bd) + _dot(lo, ones_bd)


def _head_group_norm(x, ones_bd, w, b, eps):
    mean = _seg_sum(x, ones_bd) * (1.0 / HEAD_DIM)
    xc = x - mean
    var = _seg_sum(xc * xc, ones_bd) * (1.0 / HEAD_DIM)
    return xc * lax.rsqrt(var + eps) * w + b


def _rope(x, cos, sin_lo, sin_hi, half):
    n = x.shape[-1]
    return x * cos + pltpu.roll(x, n - half, 1) * sin_lo + pltpu.roll(x, half, 1) * sin_hi


def _norm_matmul_kernel(x_ref, nw_ref, w_ref, o_ref, h_ref):
    @pl.when(pl.program_id(1) == 0)
    def _():
        x = x_ref[...]
        y = x * lax.rsqrt(jnp.mean(x * x, axis=-1, keepdims=True) + NORM_EPS)
        h_ref[...] = (y * nw_ref[...]).astype(BF16)

    o_ref[...] = _dot_nt(h_ref[...], w_ref[...])


def _norm_matmul(x, norm_w, w_t_bf16, tm, tn):
    n, d = x.shape
    p = w_t_bf16.shape[0]
    return pl.pallas_call(
        _norm_matmul_kernel,
        grid=(n // tm, p // tn),
        in_specs=[pl.BlockSpec((tm, d), lambda i, j: (i, 0)),
                  pl.BlockSpec((1, d), lambda i, j: (0, 0)),
                  pl.BlockSpec((tn, d), lambda i, j: (j, 0))],
        out_specs=pl.BlockSpec((tm, tn), lambda i, j: (i, j)),
        out_shape=jax.ShapeDtypeStruct((n, p), F32),
        scratch_shapes=[pltpu.VMEM((tm, d), BF16)],
        compiler_params=_cparams(2),
        name="norm_in_proj",
    )(x, norm_w.reshape(1, d), w_t_bf16)


def _merge_kernel(x_ref, oa_ref, ob_ref, oc_ref, od_ref, gate_ref, wb_ref, wo_ref, out_ref):
    acc = None
    for b, o_ref in enumerate((oa_ref, ob_ref, oc_ref, od_ref)):
        up = _dot(o_ref[...].astype(BF16), wb_ref[b])
        g = jax.nn.sigmoid(gate_ref[:, b * D_MODEL:(b + 1) * D_MODEL])
        acc = g * up if acc is None else acc + g * up
    out_ref[...] = x_ref[...] + _dot(acc.astype(BF16), wo_ref[...])


def _merge(x, branches, p, w_branch_bf16, w_out_bf16, tm):
    n = x.shape[0]
    row = lambda w: pl.BlockSpec((tm, w), lambda i: (i, 0))
    return pl.pallas_call(
        _merge_kernel,
        grid=(n // tm,),
        in_specs=[row(D_MODEL), row(BRANCH_W), row(BRANCH_W), row(BRANCH_W), row(BRANCH_W),
                  pl.BlockSpec((tm, GATE_P), lambda i: (i, COL_GATE // GATE_P)),
                  pl.BlockSpec((N_BRANCH, BRANCH_W, D_MODEL), lambda i: (0, 0, 0)),
                  pl.BlockSpec((D_MODEL, D_MODEL), lambda i: (0, 0))],
        out_specs=row(D_MODEL),
        out_shape=jax.ShapeDtypeStruct((n, D_MODEL), F32),
        compiler_params=_cparams(1),
        name="gated_merge_out_proj",
    )(x, *branches, p, w_branch_bf16, w_out_bf16)


FF_CHUNK = D_FF // 2


def _ffn_kernel(x_ref, nw_ref, wi_ref, wo_ref, out_ref):
    x = x_ref[...]
    h = (x * lax.rsqrt(jnp.mean(x * x, axis=-1, keepdims=True) + NORM_EPS) * nw_ref[...]).astype(BF16)
    acc = x
    for c in range(D_FF // FF_CHUNK):
        g = _dot(h, wi_ref[:, c * FF_CHUNK:(c + 1) * FF_CHUNK])
        u = _dot(h, wi_ref[:, D_FF + c * FF_CHUNK:D_FF + (c + 1) * FF_CHUNK])
        a = (g * jax.nn.sigmoid(g) * u).astype(BF16)
        acc = acc + _dot(a, wo_ref[c * FF_CHUNK:(c + 1) * FF_CHUNK, :])
    out_ref[...] = acc


def _ffn(x, norm_w, w_in_bf16, w_out_bf16, tm):
    n = x.shape[0]
    return pl.pallas_call(
        _ffn_kernel,
        grid=(n // tm,),
        in_specs=[pl.BlockSpec((tm, D_MODEL), lambda i: (i, 0)),
                  pl.BlockSpec((1, D_MODEL), lambda i: (0, 0)),
                  pl.BlockSpec((D_MODEL, 2 * D_FF), lambda i: (0, 0)),
                  pl.BlockSpec((D_FF, D_MODEL), lambda i: (0, 0))],
        out_specs=pl.BlockSpec((tm, D_MODEL), lambda i: (i, 0)),
        out_shape=jax.ShapeDtypeStruct((n, D_MODEL), F32),
        compiler_params=_cparams(1),
        name="swiglu_ffn",
    )(x, norm_w.reshape(1, D_MODEL), w_in_bf16, w_out_bf16)


def _att_prep_kernel(p_ref, cos_ref, slo_ref, shi_ref, qnw_ref, knw_ref, inw_ref, ones_ref,
                     q_ref, k_ref, v_ref, ki_ref, kb_ref, vt_ref, qib_ref, kib_ref, wit_ref):
    cos, slo, shi = cos_ref[...], slo_ref[...], shi_ref[...]
    ones_bd = ones_ref[...]
    half = ROT_DIM // 2

    def head_rms(x, w):
        ms = _seg_sum(x * x, ones_bd) * (1.0 / HEAD_DIM)
        return x * lax.rsqrt(ms + NORM_EPS) * w

    q = _rope(head_rms(p_ref[:, 0:256], qnw_ref[...]), cos, slo, shi, half)
    k = _rope(head_rms(p_ref[:, 256:512], knw_ref[...]), cos, slo, shi, half)
    v = p_ref[:, 512:768]
    qi = _rope(p_ref[:, 768:1024], cos, slo, shi, half)
    tail = p_ref[:, 1024:1152]
    lane = lax.broadcasted_iota(I32, tail.shape, 1)
    is_ki = lane < HEAD_DIM
    kiraw = jnp.where(is_ki, tail, 0.0)
    ms = jnp.sum(kiraw * kiraw, axis=-1, keepdims=True) * (1.0 / HEAD_DIM)
    kin = kiraw * lax.rsqrt(ms + NORM_EPS) * inw_ref[...]
    ki = _rope(kin, cos[:, :LANES], slo[:, :LANES], shi[:, :LANES], half)
    ki = jnp.where(is_ki, ki, 0.0)

    q_ref[...] = q * ATT_SCALE
    k_ref[...] = k
    v_ref[...] = v
    ki_ref[...] = ki
    kb_ref[...] = k.astype(BF16)
    vt_ref[0] = v.T.astype(BF16)
    qib_ref[...] = qi.astype(BF16)
    kib_ref[...] = ki.astype(BF16)
    wi = jnp.where(is_ki, 0.0, tail) * IDX_SCALE
    wit_ref[0] = pltpu.roll(wi, LANES - HEAD_DIM, 1).T[0:8, :]


def _att_prep(p, tables, q_norm_w, k_norm_w, idx_k_norm_w, ones_bd, n_seq, t_len, tm):
    n = p.shape[0]
    tpb = t_len // tm
    cos, slo, shi = tables
    tile4 = lambda w: jnp.tile(w.reshape(1, HEAD_DIM), (1, N_HEADS))
    inw = jnp.concatenate([idx_k_norm_w.reshape(1, HEAD_DIM), jnp.ones((1, LANES - HEAD_DIM), F32)], axis=1)
    row = lambda w: pl.BlockSpec((tm, w), lambda i: (i, 0))
    tab = pl.BlockSpec((tm, BRANCH_W), lambda i: (i % tpb, 0))
    full = lambda a, b: pl.BlockSpec((a, b), lambda i: (0, 0))
    return pl.pallas_call(
        _att_prep_kernel,
        grid=(n // tm,),
        in_specs=[pl.BlockSpec((tm, ATT_P_PAD), lambda i: (i, COL_ATT // ATT_P_PAD)),
                  tab, tab, tab, full(1, BRANCH_W), full(1, BRANCH_W), full(1, LANES),
                  full(BRANCH_W, BRANCH_W)],
        out_specs=[row(BRANCH_W), row(BRANCH_W), row(BRANCH_W), row(LANES), row(BRANCH_W),
                   pl.BlockSpec((1, BRANCH_W, tm), lambda i: (i // tpb, 0, i % tpb)),
                   row(BRANCH_W), row(LANES),
                   pl.BlockSpec((1, 8, tm), lambda i: (i // tpb, 0, i % tpb))],
        out_shape=[jax.ShapeDtypeStruct((n, BRANCH_W), F32),
                   jax.ShapeDtypeStruct((n, BRANCH_W), F32),
                   jax.ShapeDtypeStruct((n, BRANCH_W), F32),
                   jax.ShapeDtypeStruct((n, LANES), F32),
                   jax.ShapeDtypeStruct((n, BRANCH_W), BF16),
                   jax.ShapeDtypeStruct((n_seq, BRANCH_W, t_len), BF16),
                   jax.ShapeDtypeStruct((n, BRANCH_W), BF16),
                   jax.ShapeDtypeStruct((n, LANES), BF16),
                   jax.ShapeDtypeStruct((n_seq, 8, t_len), F32)],
        compiler_params=_cparams(1),
        name="dsa_prep",
    )(p, cos, slo, shi, tile4(q_norm_w), tile4(k_norm_w), inw, ones_bd)


QB = 128


def _float_key(x):
    b = lax.bitcast_convert_type(x, I32)
    b = jnp.where(b == INT_MIN, 0, b)
    return jnp.where(b < 0, b ^ jnp.int32(0x7FFFFFFF), b)


KT = 512
KT_MM = 1024
PROBES_PER_CHECK = 4
MAX_PROBES = 48
NO_COUNT = 2 ** 30


def _dsa_prompt_kernel(n_sel, q_ref, qib_ref, wit_ref, kb_ref, vt_ref, kib_ref, o_ref,
                       key_ref, qbd_ref, acc_ref, m_ref, l_ref):
    j = pl.program_id(1)
    nkt = (j + KT // QB) // (KT // QB)
    nkt_mm = (j + KT_MM // QB) // (KT_MM // QB)
    q0 = j * QB

    @pl.when(jnp.logical_and(pl.program_id(0) == 0, j == 0))
    def _():
        qbd_ref[...] = jnp.zeros_like(qbd_ref)

    qib = qib_ref[...]
    qis = jnp.concatenate([qib[:, h * HEAD_DIM:(h + 1) * HEAD_DIM] for h in range(N_HEADS)], axis=0)
    qt = q_ref[...].T.astype(BF16)
    for h in range(N_HEADS):
        qbd_ref[h * HEAD_DIM:(h + 1) * HEAD_DIM, h * QB:(h + 1) * QB] = qt[h * HEAD_DIM:(h + 1) * HEAD_DIM, :]
    wit = wit_ref[0]
    srow = lax.broadcasted_iota(I32, (QB, QB), 0)
    tcol = lax.broadcasted_iota(I32, (QB, QB), 1) + q0
    tq = lax.broadcasted_iota(I32, (1, QB), 1) + q0

    def score_tile(kt, carry):
        subs = range(KT_MM // QB)
        k0 = [pl.multiple_of(kt * KT_MM + u * QB, QB) for u in subs]
        s = [_dot_nt(kib_ref[0, pl.ds(k0[u], QB), :][:, :HEAD_DIM], qis) for u in subs]
        for u in subs:
            acc = jnp.zeros((QB, QB), F32)
            for h in range(N_HEADS):
                acc = acc + jnp.maximum(s[u][:, h * QB:(h + 1) * QB], 0.0) * wit[h:h + 1, :]
            key_ref[pl.ds(k0[u], QB), :] = jnp.where(srow + k0[u] <= tcol, _float_key(acc), jnp.int32(INT_MIN))
        return carry

    lax.fori_loop(0, nkt_mm, score_tile, 0)

    def count(pred):
        def body(kt, acc):
            k0 = pl.multiple_of(kt * KT, KT)
            m = pred(key_ref[pl.ds(k0, KT), :]).astype(I32)
            return acc + jnp.sum(m.reshape(KT // 8, 8, QB), axis=0)
        acc = lax.fori_loop(0, nkt, body, jnp.zeros((8, QB), I32))
        return jnp.sum(acc, axis=0, keepdims=True)

    def colmax_body(kt, acc):
        k0 = pl.multiple_of(kt * KT, KT)
        return jnp.maximum(acc, jnp.max(key_ref[pl.ds(k0, KT), :].reshape(KT // 8, 8, QB), axis=0))

    kmax = jnp.max(lax.fori_loop(0, nkt, colmax_body, jnp.full((8, QB), INT_MIN, I32)), axis=0, keepdims=True)
    kmax = jnp.minimum(kmax, jnp.int32(2 ** 31 - 2))
    hi0 = kmax + 1

    def unfinished(st):
        return jnp.logical_and(st[0] < MAX_PROBES, jnp.logical_not(st[5]))

    def probe_step(st):
        i, lo, hi, c_lo, done, _ = st
        for _ in range(PROBES_PER_CHECK):
            e = jnp.maximum(i, 2) + 21
            step = jnp.int32(1) << jnp.minimum(e, 30)
            base = jnp.where(hi <= 1, hi, hi0)
            down = jnp.where(jnp.logical_and(e <= 30, base >= INT_MIN + step), base - step, jnp.int32(INT_MIN))
            mid = (lo >> 1) + (hi >> 1) + ((lo | hi) & 1)
            x = jnp.where(c_lo == NO_COUNT, jnp.where(hi == 1, 0, down), mid)
            x = jnp.where(i == 0, kmax, jnp.where(i == 1, 1, x))
            c = count(lambda blk: blk >= x)
            live = done == 0
            up = jnp.logical_and(jnp.logical_and(live, c >= n_sel), i != 1)
            dn = jnp.logical_and(live, c < n_sel)
            lo = jnp.where(up, x, lo)
            c_lo = jnp.where(up, c, c_lo)
            hi = jnp.where(dn, jnp.minimum(x, hi), hi)
            done = jnp.where(jnp.logical_or(c_lo == n_sel, hi - 1 == lo), 1, done)
            i = i + 1
        return i, lo, hi, c_lo, done, jnp.min(done) > 0

    few = tq + 1 <= n_sel
    done0 = jnp.where(few, 1, 0)
    init = (jnp.int32(0), jnp.full((1, QB), INT_MIN, I32), hi0, jnp.where(few, n_sel, jnp.int32(NO_COUNT)),
            done0, jnp.min(done0) > 0)
    _, tau, _, c_lo, _, _ = lax.while_loop(unfinished, probe_step, init)
    all_exact = jnp.min(jnp.where(c_lo == n_sel, 1, 0)) > 0

    @pl.when(jnp.logical_not(all_exact))
    def _():
        cnt_gt = count(lambda blk: blk > tau)
        cnt_ge = count(lambda blk: blk >= tau)
        need = (n_sel - cnt_gt).astype(F32)
        excess = jnp.max(jnp.where(tau > INT_MIN, cnt_ge, 0)) > n_sel

        @pl.when(excess)
        def _():
            ltri = (lax.broadcasted_iota(I32, (QB, QB), 1) < srow).astype(BF16)

            def demote(kb, seen):
                k0 = pl.multiple_of(kb * QB, QB)
                blk = key_ref[pl.ds(k0, QB), :]
                tie = blk == tau
                tief = jnp.where(tie, 1.0, 0.0)
                rank = _dot(ltri, tief.astype(BF16)) + seen
                key_ref[pl.ds(k0, QB), :] = jnp.where(jnp.logical_and(tie, rank >= need),
                                                      jnp.int32(INT_MIN), blk)
                return seen + jnp.sum(tief, axis=0, keepdims=True)

            lax.fori_loop(0, j + 1, demote, jnp.zeros((1, QB), F32))

    thr = jnp.maximum(tau, INT_MIN + 1)
    acc_ref[...] = jnp.zeros_like(acc_ref)
    m_ref[...] = jnp.full_like(m_ref, NEG_BIG)
    l_ref[...] = jnp.zeros_like(l_ref)

    def attend(kt, carry):
        k0 = pl.multiple_of(kt * KT_MM, KT_MM)
        sel = key_ref[pl.ds(k0, KT_MM), :] >= thr
        lg = _dot(kb_ref[0, pl.ds(k0, KT_MM), :], qbd_ref[...])
        heads = range(N_HEADS)
        hs = [slice(h * HEAD_DIM, (h + 1) * HEAD_DIM) for h in heads]
        lh = [jnp.where(sel, lg[:, h * QB:(h + 1) * QB], NEG_BIG) for h in heads]
        m_old = [m_ref[h:h + 1, :] for h in heads]
        m_new = [jnp.maximum(m_old[h], jnp.max(lh[h], axis=0, keepdims=True)) for h in heads]
        p = [jnp.exp(lh[h] - m_new[h]) for h in heads]
        pv = [_dot(vt_ref[0, hs[h], pl.ds(k0, KT_MM)], p[h].astype(BF16)) for h in heads]
        alpha = [jnp.exp(m_old[h] - m_new[h]) for h in heads]
        for h in heads:
            l_ref[h:h + 1, :] = alpha[h] * l_ref[h:h + 1, :] + jnp.sum(p[h], axis=0, keepdims=True)
            m_ref[h:h + 1, :] = m_new[h]
            acc_ref[hs[h], :] = acc_ref[hs[h], :] * alpha[h] + pv[h]
        return carry

    lax.fori_loop(0, nkt_mm, attend, 0)

    outs = [acc_ref[h * HEAD_DIM:(h + 1) * HEAD_DIM, :] / l_ref[h:h + 1, :] for h in range(N_HEADS)]
    o_ref[...] = jnp.concatenate(outs, axis=0).T


def _dsa_prompt(q, qib, wit, kb, vt, kib, n_seq, t_len):
    assert t_len % KT_MM == 0 and KT_MM % KT == 0
    n_sel = min(TOPK_MAX, t_len // 4)
    nblk = t_len // QB
    return pl.pallas_call(
        functools.partial(_dsa_prompt_kernel, n_sel),
        grid=(n_seq, nblk),
        in_specs=[pl.BlockSpec((QB, BRANCH_W), lambda b, j: (b * nblk + j, 0)),
                  pl.BlockSpec((QB, BRANCH_W), lambda b, j: (b * nblk + j, 0)),
                  pl.BlockSpec((1, 8, QB), lambda b, j: (b, 0, j)),
                  pl.BlockSpec((1, t_len, BRANCH_W), lambda b, j: (b, 0, 0)),
                  pl.BlockSpec((1, BRANCH_W, t_len), lambda b, j: (b, 0, 0)),
                  pl.BlockSpec((1, t_len, LANES), lambda b, j: (b, 0, 0))],
        out_specs=pl.BlockSpec((QB, BRANCH_W), lambda b, j: (b * nblk + j, 0)),
        out_shape=jax.ShapeDtypeStruct((n_seq * t_len, BRANCH_W), F32),
        scratch_shapes=[pltpu.VMEM((t_len, QB), I32),
                        pltpu.VMEM((BRANCH_W, N_HEADS * QB), BF16),
                        pltpu.VMEM((BRANCH_W, QB), F32),
                        pltpu.VMEM((8, QB), F32),
                        pltpu.VMEM((8, QB), F32)],
        compiler_params=_cparams(2),
        name="dsa_prompt",
    )(q, qib, wit, kb.reshape(n_seq, t_len, BRANCH_W), vt, kib.reshape(n_seq, t_len, LANES))


RW_CHUNK = 128
RW_GROUP = 8


def _col_bcast(xt, i):
    lane = lax.broadcasted_iota(I32, (HEAD_DIM, LANES), 1)
    halves = []
    for pair in range(2):
        r0 = 2 * pair * HEAD_DIM
        c0 = jnp.broadcast_to(xt[r0:r0 + HEAD_DIM, i:i + 1], (HEAD_DIM, LANES))
        c1 = jnp.broadcast_to(xt[r0 + HEAD_DIM:r0 + 2 * HEAD_DIM, i:i + 1], (HEAD_DIM, LANES))
        halves.append(jnp.where(lane < HEAD_DIM, c0, c1))
    return jnp.concatenate(halves, axis=1)


def _group_cols(xt_ref, n_arr, gi):
    sh = lax.rem(LANES - gi * RW_GROUP, LANES)
    return [pltpu.roll(xt_ref[a], sh, 1) for a in range(n_arr)]


def _rwkv_prep(p, prev, mu, w0, w_wa, a0, w_g2, k_k, k_a, r_k, ones_bd):
    xl = p + (prev - p) * mu
    r, k, v = xl[:, 0:256], xl[:, 256:512], xl[:, 512:768]
    wa = xl[:, 768:896]
    lane = lax.broadcasted_iota(I32, wa.shape, 1)
    wa = jnp.where(lane < RWKV_DECAY_LORA, jnp.tanh(wa), wa)
    z = _dot(wa.astype(BF16), w_wa)
    nz = -(w0 + z[:, 0:256])
    w_log = -(jnp.maximum(nz, 0.0) + jnp.log1p(jnp.exp(-jnp.abs(nz)))) - 0.5
    log_decay = -jnp.exp(w_log)
    a = jax.nn.sigmoid(a0 + z[:, 256:512])
    g = _dot(jax.nn.sigmoid(xl[:, 896:1024]).astype(BF16), w_g2)
    kk = k * k_k
    kk = kk / jnp.sqrt(_seg_sum(kk * kk, ones_bd) + 1e-12)
    kp = k * (1.0 + (a - 1.0) * k_a)
    bonus = _seg_sum(r * kp * r_k, ones_bd) * v
    return r, log_decay, kp, v, kk, kk * a, g, bonus


def _rwkv_kernel(sequential, p_ref, prev_ref, st0_ref, mu_ref, w0_ref, wwa_ref, a0_ref, wg2_ref,
                 kk_ref, ka_ref, rk_ref, lnw_ref, lnb_ref, ones_ref,
                 o_ref, st_out_ref, carry_ref, st_ref, xt_ref, v_ref, y_ref):
    c = pl.program_id(1)
    p = p_ref[...]
    if sequential:
        @pl.when(c == 0)
        def _():
            carry_ref[...] = prev_ref[0]
            st_ref[...] = st0_ref[0]

        row = lax.broadcasted_iota(I32, p.shape, 0)
        prev = jnp.where(row == 0, carry_ref[...], pltpu.roll(p, 1, 0))
        carry_ref[...] = p[RW_CHUNK - 1:RW_CHUNK, :]
    else:
        prev = prev_ref[...]
    ones_bd = ones_ref[...]
    r, log_decay, kp, v, kk, kka, g, bonus = _rwkv_prep(
        p, prev, mu_ref[...], w0_ref[...], wwa_ref[...], a0_ref[...], wg2_ref[...],
        kk_ref[...], ka_ref[...], rk_ref[...], ones_bd)
    for a, arr in enumerate((r, jnp.exp(log_decay), kp, kk, kka)):
        xt_ref[a] = arr.T
    v_ref[...] = v

    def group(gi, st):
        rt, wt, kt, kkt, kkat = _group_cols(xt_ref, 5, gi)
        base = pl.multiple_of(gi * RW_GROUP, RW_GROUP)
        vrows = v_ref[pl.ds(base, RW_GROUP), :]
        ys = []
        for i in range(RW_GROUP):
            if not sequential:
                st = st0_ref[base + i]
            sa = jnp.sum(st * _col_bcast(kkt, i), axis=0, keepdims=True)
            st = st * _col_bcast(wt, i) - _col_bcast(kkat, i) * sa + _col_bcast(kt, i) * vrows[i:i + 1, :]
            ys.append(jnp.sum(st * _col_bcast(rt, i), axis=0, keepdims=True))
            if not sequential:
                st_out_ref[base + i] = st
        y_ref[pl.ds(base, RW_GROUP), :] = jnp.concatenate(ys, axis=0)
        return st

    if sequential:
        st = lax.fori_loop(0, RW_CHUNK // RW_GROUP, group, st_ref[...])
        st_ref[...] = st

        @pl.when(c == pl.num_programs(1) - 1)
        def _():
            st_out_ref[0] = st
    else:
        lax.fori_loop(0, RW_CHUNK // RW_GROUP, group, jnp.zeros((HEAD_DIM, BRANCH_W), F32))

    o = _head_group_norm(y_ref[...], ones_bd, lnw_ref[...], lnb_ref[...], RWKV_LN_EPS)
    o_ref[...] = (o + bonus) * g


def _rwkv(p, prev, st0, lw, ones_bd, n_seq, t_len, sequential):
    n = n_seq * t_len
    nchunk = t_len // RW_CHUNK if sequential else 1
    full2 = lambda a, b: pl.BlockSpec((a, b), lambda s, c: (0, 0))
    if sequential:
        prev_spec = pl.BlockSpec((1, 1, RWKV_P), lambda s, c: (s, 0, 0))
        st_spec = pl.BlockSpec((1, HEAD_DIM, BRANCH_W), lambda s, c: (s, 0, 0))
        grid = (n_seq, nchunk)
    else:
        assert n == RW_CHUNK
        prev_spec = pl.BlockSpec((RW_CHUNK, RWKV_P), lambda s, c: (0, 0))
        st_spec = pl.BlockSpec((RW_CHUNK, HEAD_DIM, BRANCH_W), lambda s, c: (0, 0, 0))
        grid = (1, 1)
    return pl.pallas_call(
        functools.partial(_rwkv_kernel, sequential),
        grid=grid,
        in_specs=[pl.BlockSpec((RW_CHUNK, RWKV_P), lambda s, c: (s * nchunk + c, COL_RWKV // RWKV_P)),
                  prev_spec, st_spec,
                  full2(1, RWKV_P), full2(1, BRANCH_W), full2(LANES, 2 * BRANCH_W), full2(1, BRANCH_W),
                  full2(RWKV_G_LORA, BRANCH_W), full2(1, BRANCH_W), full2(1, BRANCH_W), full2(1, BRANCH_W),
                  full2(1, BRANCH_W), full2(1, BRANCH_W), full2(BRANCH_W, BRANCH_W)],
        out_specs=[pl.BlockSpec((RW_CHUNK, BRANCH_W), lambda s, c: (s * nchunk + c, 0)), st_spec],
        out_shape=[jax.ShapeDtypeStruct((n, BRANCH_W), F32),
                   jax.ShapeDtypeStruct(st0.shape, F32)],
        scratch_shapes=[pltpu.VMEM((1, RWKV_P), F32),
                        pltpu.VMEM((HEAD_DIM, BRANCH_W), F32),
                        pltpu.VMEM((5, BRANCH_W, RW_CHUNK), F32),
                        pltpu.VMEM((RW_CHUNK, BRANCH_W), F32),
                        pltpu.VMEM((RW_CHUNK, BRANCH_W), F32)],
        compiler_params=_cparams(2),
        name="rwkv7_prompt" if sequential else "rwkv7_sample",
    )(p, prev, st0, lw["mu"], lw["w0"], lw["w_wa"], lw["a0"], lw["w_g2"], lw["k_k"], lw["k_a"],
      lw["r_k"], lw["ln_w"], lw["ln_b"], ones_bd)


RC = 64


def _mm(a, b):
    return _dot(a.astype(BF16), b.astype(BF16))


def _rwkv_chunk_kernel(p_ref, mu_ref, w0_ref, wwa_ref, a0_ref, wg2_ref, kk_ref, ka_ref, rk_ref,
                       lnw_ref, lnb_ref, ones_ref, o_ref, st_out_ref, carry_ref, z_ref):
    ns = p_ref.shape[0]
    seqs = range(ns)
    c = pl.program_id(0)

    @pl.when(c == 0)
    def _():
        carry_ref[...] = jnp.zeros_like(carry_ref)
        z_ref[...] = jnp.zeros_like(z_ref)

    rows = ns * RC
    p = p_ref[...].reshape(rows, RWKV_P)
    t_loc = lax.rem(lax.broadcasted_iota(I32, (rows, RWKV_P), 0), RC)
    seq_of_row = lax.broadcasted_iota(I32, (rows, RWKV_P), 0) // RC
    first = carry_ref[0]
    for s in range(1, ns):
        first = jnp.where(seq_of_row == s, carry_ref[s], first)
    prev = jnp.where(t_loc == 0, first, pltpu.roll(p, 1, 0))
    for s in seqs:
        carry_ref[s] = p[s * RC + RC - 1:(s + 1) * RC, :]
    ones_bd = ones_ref[...]
    r, log_decay, kp, v, kk, kka, g, bonus = _rwkv_prep(
        p, prev, mu_ref[...], w0_ref[...], wwa_ref[...], a0_ref[...], wg2_ref[...],
        kk_ref[...], ka_ref[...], rk_ref[...], ones_bd)

    trow = lax.rem(lax.broadcasted_iota(I32, (rows, BRANCH_W), 0), RC)
    cs = log_decay
    for sh in (1, 2, 4, 8, 16, 32):
        cs = cs + jnp.where(trow >= sh, pltpu.roll(cs, sh, 0), 0.0)
    pw = jnp.exp(cs)
    inv_pw = jnp.exp(-cs)
    a_m = -(kk * jnp.exp(cs - log_decay))
    r_m = r * pw
    b_m = kka * inv_pw
    k_m = kp * inv_pw
    sl = lambda x, s: x[s * RC:(s + 1) * RC]
    bkt = [jnp.concatenate([sl(b_m, s), sl(k_m, s)], axis=0).T for s in seqs]
    kbt = [jnp.concatenate([sl(k_m, s), sl(b_m, s)], axis=0).T for s in seqs]
    pwt = [jnp.concatenate([sl(pw, s), sl(pw, s)], axis=0).T for s in seqs]

    lane_head = lax.broadcasted_iota(I32, (RC, BRANCH_W), 1) // HEAD_DIM
    hm = [lane_head == h for h in range(N_HEADS)]
    xstack = [jnp.concatenate([jnp.where(hm[h], sl(a_m, s), 0.0) for h in range(N_HEADS)]
                              + [jnp.where(hm[h], sl(r_m, s), 0.0) for h in range(N_HEADS)], axis=0)
              for s in seqs]
    g1 = [_mm(xstack[s], bkt[s]) for s in seqs]
    g2 = [_mm(xstack[s], kbt[s]) for s in seqs]
    nst = N_HEADS * RC
    t_of_row = lax.rem(lax.broadcasted_iota(I32, (nst, LANES), 0), RC)
    s_of_lane = lax.broadcasted_iota(I32, (nst, LANES), 1)
    strict = s_of_lane < t_of_row
    incl = jnp.logical_and(s_of_lane <= t_of_row, s_of_lane < RC)
    l_st = [jnp.where(strict, g1[s][0:nst], 0.0) for s in seqs]
    m_st = [jnp.where(strict, g2[s][0:nst], 0.0) for s in seqs]
    n_st = [jnp.where(incl, g1[s][nst:2 * nst], 0.0) for s in seqs]
    q_st = [jnp.where(incl, g2[s][nst:2 * nst], 0.0) for s in seqs]

    eye = jnp.where(lax.broadcasted_iota(I32, (RC, LANES), 0) == lax.broadcasted_iota(I32, (RC, LANES), 1),
                    1.0, 0.0)
    lps = [l_st[s][h * RC:(h + 1) * RC] for s in seqs for h in range(N_HEADS)]
    ws = [eye + lp for lp in lps]
    for _ in range(5):
        lps = [_mm(lp[:, :RC], lp) for lp in lps]
        ws = [w_h + _mm(w_h[:, :RC], lp) for w_h, lp in zip(ws, lps)]

    def per_head(stacked):
        out = jnp.where(hm[0], stacked[0:RC], 0.0)
        for h in range(1, N_HEADS):
            out = out + jnp.where(hm[h], stacked[h * RC:(h + 1) * RC], 0.0)
        return out

    row_head = lax.broadcasted_iota(I32, (BRANCH_W, BRANCH_W), 0) // HEAD_DIM
    col_head = lax.broadcasted_iota(I32, (BRANCH_W, BRANCH_W), 1) // HEAD_DIM
    z0 = [z_ref[s] for s in seqs]
    zbd = [jnp.where(row_head == col_head, jnp.concatenate([z0[s]] * N_HEADS, axis=0), 0.0) for s in seqs]
    vs = [sl(v, s) for s in seqs]
    az = [_mm(sl(a_m, s), zbd[s]) for s in seqs]
    mv = [_mm(m_st[s][:, :RC], vs[s]) for s in seqs]
    rz = [_mm(sl(r_m, s), zbd[s]) for s in seqs]
    qv = [_mm(q_st[s][:, :RC], vs[s]) for s in seqs]
    rhs = [az[s] + per_head(mv[s]) for s in seqs]
    wr = [[_mm(ws[s * N_HEADS + h][:, :RC], rhs[s]) for h in range(N_HEADS)] for s in seqs]
    u = [per_head(jnp.concatenate(wr[s], axis=0)) for s in seqs]
    nu = [_mm(n_st[s][:, :RC], u[s]) for s in seqs]
    uv = [jnp.concatenate([u[s], vs[s]], axis=0) for s in seqs]
    upd = [[_mm(bkt[s][h * HEAD_DIM:(h + 1) * HEAD_DIM, :], uv[s]) for h in range(N_HEADS)] for s in seqs]
    y = jnp.concatenate([rz[s] + per_head(nu[s] + qv[s]) for s in seqs], axis=0)
    z_new = [_col_bcast(pwt[s], RC - 1) * (z0[s] + per_head(jnp.concatenate(upd[s], axis=0))) for s in seqs]
    for s in seqs:
        z_ref[s] = z_new[s]

    @pl.when(c == pl.num_programs(0) - 1)
    def _():
        for s in seqs:
            st_out_ref[s] = z_new[s]

    o = (_head_group_norm(y, ones_bd, lnw_ref[...], lnb_ref[...], RWKV_LN_EPS) + bonus) * g
    o_ref[...] = o.reshape(ns, RC, BRANCH_W)


def _rwkv_prompt(p, lw, ones_bd, n_seq, t_len):
    nchunk = t_len // RC
    full2 = lambda a, b: pl.BlockSpec((a, b), lambda c: (0, 0))
    st_spec = pl.BlockSpec((n_seq, HEAD_DIM, BRANCH_W), lambda c: (0, 0, 0))
    o, st = pl.pallas_call(
        _rwkv_chunk_kernel,
        grid=(nchunk,),
        in_specs=[pl.BlockSpec((n_seq, RC, RWKV_P), lambda c: (0, c, COL_RWKV // RWKV_P)),
                  full2(1, RWKV_P), full2(1, BRANCH_W), full2(LANES, 2 * BRANCH_W), full2(1, BRANCH_W),
                  full2(RWKV_G_LORA, BRANCH_W), full2(1, BRANCH_W), full2(1, BRANCH_W), full2(1, BRANCH_W),
                  full2(1, BRANCH_W), full2(1, BRANCH_W), full2(BRANCH_W, BRANCH_W)],
        out_specs=[pl.BlockSpec((n_seq, RC, BRANCH_W), lambda c: (0, c, 0)), st_spec],
        out_shape=[jax.ShapeDtypeStruct((n_seq, t_len, BRANCH_W), F32),
                   jax.ShapeDtypeStruct((n_seq, HEAD_DIM, BRANCH_W), F32)],
        scratch_shapes=[pltpu.VMEM((n_seq, 1, RWKV_P), F32),
                        pltpu.VMEM((n_seq, HEAD_DIM, BRANCH_W), F32)],
        compiler_params=_cparams(1),
        name="rwkv7_prompt",
    )(p.reshape(n_seq, t_len, P_PAD), lw["mu"], lw["w0"], lw["w_wa"], lw["a0"], lw["w_g2"], lw["k_k"], lw["k_a"],
      lw["r_k"], lw["ln_w"], lw["ln_b"], ones_bd)
    return o.reshape(n_seq * t_len, BRANCH_W), st


def _ret_prompt_kernel(p_ref, cos_ref, slo_ref, shi_ref, dmask_ref, qdec_ref, kdec_ref, sdec_ref,
                       gnw_ref, gnb_ref, ones_ref, o_ref, s_out_ref, s_ref):
    c = pl.program_id(1)

    @pl.when(c == 0)
    def _():
        s_ref[...] = jnp.zeros_like(s_ref)

    cos, slo, shi = cos_ref[...], slo_ref[...], shi_ref[...]
    half = HEAD_DIM // 2
    rq = _rope(p_ref[:, 0:256], cos, slo, shi, half)
    rk = _rope(p_ref[:, 256:512], cos, slo, shi, half) * RET_KSCALE
    rv = p_ref[:, 512:768]
    rg = p_ref[:, 768:1024]
    kdt = (rk * kdec_ref[...]).T.astype(BF16)
    qdec = qdec_ref[...]
    outs = []
    for h in range(N_HEADS):
        hs = slice(h * HEAD_DIM, (h + 1) * HEAD_DIM)
        qh, kh, vh = rq[:, hs].astype(BF16), rk[:, hs].astype(BF16), rv[:, hs].astype(BF16)
        att = _dot_nt(qh, kh) * dmask_ref[h]
        s_old = s_ref[hs, :]
        outs.append(_dot(att.astype(BF16), vh) + _dot(qh, s_old.astype(BF16)) * qdec[:, hs])
        s_ref[hs, :] = s_old * sdec_ref[hs, :] + _dot(kdt[hs, :], vh)
    o = jnp.concatenate(outs, axis=1)
    gn = _head_group_norm(o, ones_ref[...], gnw_ref[...], gnb_ref[...], GN_EPS)
    o_ref[...] = rg * jax.nn.sigmoid(rg) * gn

    @pl.when(c == pl.num_programs(1) - 1)
    def _():
        s_out_ref[0] = s_ref[...]


def _ret_prompt(p, tables, dec, gn_w, gn_b, ones_bd, n_seq, t_len):
    nchunk = t_len // RET_CHUNK
    cos, slo, shi = tables
    dmask, qdec, kdec, sdec = dec
    tab = pl.BlockSpec((RET_CHUNK, BRANCH_W), lambda s, c: (c, 0))
    full2 = lambda a, b: pl.BlockSpec((a, b), lambda s, c: (0, 0))
    return pl.pallas_call(
        _ret_prompt_kernel,
        grid=(n_seq, nchunk),
        in_specs=[pl.BlockSpec((RET_CHUNK, RET_P), lambda s, c: (s * nchunk + c, COL_RET // RET_P)),
                  tab, tab, tab,
                  pl.BlockSpec((N_HEADS, RET_CHUNK, RET_CHUNK), lambda s, c: (0, 0, 0)),
                  full2(RET_CHUNK, BRANCH_W), full2(RET_CHUNK, BRANCH_W), full2(BRANCH_W, HEAD_DIM),
                  full2(1, BRANCH_W), full2(1, BRANCH_W), full2(BRANCH_W, BRANCH_W)],
        out_specs=[pl.BlockSpec((RET_CHUNK, BRANCH_W), lambda s, c: (s * nchunk + c, 0)),
                   pl.BlockSpec((1, BRANCH_W, HEAD_DIM), lambda s, c: (s, 0, 0))],
        out_shape=[jax.ShapeDtypeStruct((n_seq * t_len, BRANCH_W), F32),
                   jax.ShapeDtypeStruct((n_seq, BRANCH_W, HEAD_DIM), F32)],
        scratch_shapes=[pltpu.VMEM((BRANCH_W, HEAD_DIM), F32)],
        compiler_params=_cparams(2),
        name="retention_prompt",
    )(p, cos, slo, shi, dmask, qdec, kdec, sdec, gn_w.reshape(1, BRANCH_W), gn_b.reshape(1, BRANCH_W),
      ones_bd)


def _ret_sample_kernel(p_ref, cos_ref, slo_ref, shi_ref, st0_ref, gamma_ref, gnw_ref, gnb_ref, ones_ref,
                       o_ref, st_out_ref, xt_ref, v_ref, y_ref):
    cos, slo, shi = cos_ref[...], slo_ref[...], shi_ref[...]
    half = HEAD_DIM // 2
    rq = _rope(p_ref[:, 0:256], cos, slo, shi, half)
    rk = _rope(p_ref[:, 256:512], cos, slo, shi, half) * RET_KSCALE
    rg = p_ref[:, 768:1024]
    xt_ref[0] = rq.T
    xt_ref[1] = rk.T
    v_ref[...] = p_ref[:, 512:768]
    gamma = gamma_ref[...]

    def group(gi, carry):
        qt, kt = _group_cols(xt_ref, 2, gi)
        base = pl.multiple_of(gi * RW_GROUP, RW_GROUP)
        vrows = v_ref[pl.ds(base, RW_GROUP), :]
        ys = []
        for i in range(RW_GROUP):
            st = st0_ref[base + i] * gamma + _col_bcast(kt, i) * vrows[i:i + 1, :]
            st_out_ref[base + i] = st
            ys.append(jnp.sum(st * _col_bcast(qt, i), axis=0, keepdims=True))
        y_ref[pl.ds(base, RW_GROUP), :] = jnp.concatenate(ys, axis=0)
        return carry

    lax.fori_loop(0, RW_CHUNK // RW_GROUP, group, 0)
    gn = _head_group_norm(y_ref[...], ones_ref[...], gnw_ref[...], gnb_ref[...], GN_EPS)
    o_ref[...] = rg * jax.nn.sigmoid(rg) * gn


def _ret_sample(p, tables, st0, gamma, gn_w, gn_b, ones_bd):
    n = st0.shape[0]
    assert n == RW_CHUNK
    cos, slo, shi = tables
    full2 = lambda a, b: pl.BlockSpec((a, b), lambda i: (0, 0))
    st_spec = pl.BlockSpec((n, HEAD_DIM, BRANCH_W), lambda i: (0, 0, 0))
    return pl.pallas_call(
        _ret_sample_kernel,
        grid=(1,),
        in_specs=[pl.BlockSpec((n, RET_P), lambda i: (0, COL_RET // RET_P)),
                  full2(n, BRANCH_W), full2(n, BRANCH_W), full2(n, BRANCH_W), st_spec,
                  full2(1, BRANCH_W), full2(1, BRANCH_W), full2(1, BRANCH_W), full2(BRANCH_W, BRANCH_W)],
        out_specs=[full2(n, BRANCH_W), st_spec],
        out_shape=[jax.ShapeDtypeStruct((n, BRANCH_W), F32), jax.ShapeDtypeStruct(st0.shape, F32)],
        scratch_shapes=[pltpu.VMEM((2, BRANCH_W, RW_CHUNK), F32),
                        pltpu.VMEM((RW_CHUNK, BRANCH_W), F32),
                        pltpu.VMEM((RW_CHUNK, BRANCH_W), F32)],
        compiler_params=_cparams(1),
        name="retention_sample",
    )(p, cos, slo, shi, st0, gamma, gn_w.reshape(1, BRANCH_W), gn_b.reshape(1, BRANCH_W), ones_bd)


POOL_HIST = 16


def _pool_windows(lane):
    win = jnp.full(lane.shape, float(POOL_WINDOWS[-1]), F32)
    for gi in range(len(POOL_WINDOWS) - 2, -1, -1):
        win = jnp.where(lane < (gi + 1) * HEAD_DIM, float(POOL_WINDOWS[gi]), win)
    return win


def _pool_select(sums, lane):
    out = sums[-1]
    for gi in range(len(POOL_WINDOWS) - 2, -1, -1):
        out = jnp.where(lane < (gi + 1) * HEAD_DIM, sums[gi], out)
    return out


def _pool_prompt_kernel(tm, u_ref, wp_ref, scale_ref, y_ref, nbuf_ref, ext_ref):
    t = pl.program_id(1)

    @pl.when(t == 0)
    def _():
        ext_ref[0:POOL_HIST, :] = jnp.zeros((POOL_HIST, BRANCH_W), F32)

    @pl.when(t > 0)
    def _():
        ext_ref[0:POOL_HIST, :] = ext_ref[tm:tm + POOL_HIST, :]

    u = u_ref[...]
    ext_ref[POOL_HIST:POOL_HIST + tm, :] = u
    sums, acc, i = [], u, 1
    for win in POOL_WINDOWS:
        while i < win:
            acc = acc + ext_ref[pl.ds(POOL_HIST - i, tm), :]
            i += 1
        sums.append(acc)
    lane = lax.broadcasted_iota(I32, u.shape, 1)
    pos = (lax.broadcasted_iota(I32, u.shape, 0) + t * tm).astype(F32)
    cnt = jnp.minimum(pos + 1.0, _pool_windows(lane))
    d = _pool_select(sums, lane) / cnt - u
    y_ref[...] = _dot(d.astype(BF16), wp_ref[...]) * scale_ref[...]

    @pl.when(t == pl.num_programs(1) - 1)
    def _():
        nbuf_ref[0] = ext_ref[tm:tm + POOL_HIST, :]


def _pool_prompt(p, wp_bd, scale, n_seq, t_len, tm):
    tpb = t_len // tm
    return pl.pallas_call(
        functools.partial(_pool_prompt_kernel, tm),
        grid=(n_seq, tpb),
        in_specs=[pl.BlockSpec((tm, BRANCH_W), lambda s, t: (s * tpb + t, COL_POOL // BRANCH_W)),
                  pl.BlockSpec((BRANCH_W, BRANCH_W), lambda s, t: (0, 0)),
                  pl.BlockSpec((1, BRANCH_W), lambda s, t: (0, 0))],
        out_specs=[pl.BlockSpec((tm, BRANCH_W), lambda s, t: (s * tpb + t, 0)),
                   pl.BlockSpec((1, POOL_HIST, BRANCH_W), lambda s, t: (s, 0, 0))],
        out_shape=[jax.ShapeDtypeStruct((n_seq * t_len, BRANCH_W), F32),
                   jax.ShapeDtypeStruct((n_seq, POOL_HIST, BRANCH_W), F32)],
        scratch_shapes=[pltpu.VMEM((tm + POOL_HIST, BRANCH_W), F32)],
        compiler_params=_cparams(2),
        name="pool_prompt",
    )(p, wp_bd, scale.reshape(1, BRANCH_W))


def _pool_sample_kernel(pos0, u_ref, buf_ref, wp_ref, scale_ref, y_ref, nbuf_ref):
    u = u_ref[...]
    sums, acc, i = [], u, 1
    for win in POOL_WINDOWS:
        while i < win:
            acc = acc + buf_ref[POOL_BUF - i]
            i += 1
        sums.append(acc)
    lane = lax.broadcasted_iota(I32, u.shape, 1)
    cnt = jnp.minimum(float(pos0) + 1.0, _pool_windows(lane))
    d = _pool_select(sums, lane) / cnt - u
    y_ref[...] = _dot(d.astype(BF16), wp_ref[...]) * scale_ref[...]
    for r in range(POOL_BUF - 1):
        nbuf_ref[r] = buf_ref[r + 1]
    nbuf_ref[POOL_BUF - 1] = u


def _pool_sample(p, buf_t, wp_bd, scale, pos0):
    n = buf_t.shape[1]
    full3 = pl.BlockSpec((POOL_BUF, n, BRANCH_W), lambda i: (0, 0, 0))
    return pl.pallas_call(
        functools.partial(_pool_sample_kernel, pos0),
        grid=(1,),
        in_specs=[pl.BlockSpec((n, BRANCH_W), lambda i: (0, COL_POOL // BRANCH_W)), full3,
                  pl.BlockSpec((BRANCH_W, BRANCH_W), lambda i: (0, 0)),
                  pl.BlockSpec((1, BRANCH_W), lambda i: (0, 0))],
        out_specs=[pl.BlockSpec((n, BRANCH_W), lambda i: (0, 0)), full3],
        out_shape=[jax.ShapeDtypeStruct((n, BRANCH_W), F32),
                   jax.ShapeDtypeStruct((POOL_BUF, n, BRANCH_W), F32)],
        compiler_params=_cparams(1),
        name="pool_sample",
    )(p, buf_t, wp_bd, scale.reshape(1, BRANCH_W))


def _page_copies(pt_ref, layer, seq, slot, hbm_ref, buf_ref, sem_ref, n_pages):
    return [pltpu.make_async_copy(hbm_ref.at[layer, pt_ref[seq, pg]], buf_ref.at[slot, pg], sem_ref.at[slot])
            for pg in range(n_pages)]


def _paged_prefetch(pt_ref, layer, pairs, n_pages):
    b = pl.program_id(0)
    nb = pl.num_programs(0)
    slot = lax.rem(b, 2)

    @pl.when(b == 0)
    def _():
        for hbm_ref, buf_ref, sem_ref in pairs:
            for cp in _page_copies(pt_ref, layer, 0, 0, hbm_ref, buf_ref, sem_ref, n_pages):
                cp.start()

    @pl.when(b + 1 < nb)
    def _():
        for hbm_ref, buf_ref, sem_ref in pairs:
            for cp in _page_copies(pt_ref, layer, b + 1, 1 - slot, hbm_ref, buf_ref, sem_ref, n_pages):
                cp.start()

    for hbm_ref, buf_ref, sem_ref in pairs:
        for cp in _page_copies(pt_ref, layer, b, slot, hbm_ref, buf_ref, sem_ref, n_pages):
            cp.wait()
    return slot


def _dsa_sample_score_kernel(layer, n_pages, pt_ref, qi_ref, wi_ref, cki_ref, out_ref, buf_ref, sem_ref):
    slot = _paged_prefetch(pt_ref, layer, [(cki_ref, buf_ref, sem_ref)], n_pages)
    qi4, wi4 = qi_ref[0], wi_ref[0]
    rows = []
    for pg in range(n_pages):
        s = jnp.maximum(_dot(qi4, buf_ref[slot, pg].astype(BF16)), 0.0)
        rows.append(jnp.sum(s * wi4, axis=0, keepdims=True))
    out_ref[0] = jnp.concatenate(rows, axis=1)


def _dsa_sample_scores(page_table, qi4, wi4, cache_kidx_t, layer):
    nb, n_pages = page_table.shape
    page = cache_kidx_t.shape[3]
    past = n_pages * page
    return pl.pallas_call(
        functools.partial(_dsa_sample_score_kernel, layer, n_pages),
        grid_spec=pltpu.PrefetchScalarGridSpec(
            num_scalar_prefetch=1,
            grid=(nb,),
            in_specs=[pl.BlockSpec((1, N_HEADS, HEAD_DIM), lambda b, pt: (b, 0, 0)),
                      pl.BlockSpec((1, N_HEADS, 1), lambda b, pt: (b, 0, 0)),
                      pl.BlockSpec(memory_space=pl.ANY)],
            out_specs=pl.BlockSpec((1, 1, past), lambda b, pt: (b, 0, 0)),
            scratch_shapes=[pltpu.VMEM((2, n_pages, HEAD_DIM, page), F32),
                            pltpu.SemaphoreType.DMA((2,))]),
        out_shape=jax.ShapeDtypeStruct((nb, 1, past), F32),
        compiler_params=_cparams(1),
        name="dsa_sample_scores",
    )(page_table, qi4, wi4, cache_kidx_t)


def _dsa_sample_select_kernel(n_sel, sc_ref, qib_ref, kib_ref, wi_ref, sel_ref, selself_ref):
    nb, past = sc_ref.shape
    qi = qib_ref[...].astype(F32)
    ki = kib_ref[...].astype(F32)[:, :HEAD_DIM]
    wi = wi_ref[...]
    s_self = jnp.zeros((nb, 1), F32)
    for h in range(N_HEADS):
        dot_h = jnp.sum(qi[:, h * HEAD_DIM:(h + 1) * HEAD_DIM] * ki, axis=-1, keepdims=True)
        s_self = s_self + jnp.maximum(dot_h, 0.0) * wi[:, h:h + 1]
    key = _float_key(sc_ref[...] + 0.0)
    key_self = _float_key(s_self)

    def count(pred):
        return (jnp.sum(pred(key).astype(I32), axis=-1, keepdims=True) + pred(key_self).astype(I32))

    def bit_step(i, tau):
        cand = tau + (jnp.int32(1) << (31 - i))
        return jnp.where(count(lambda x: x >= cand) >= n_sel, cand, tau)

    tau = lax.fori_loop(0, 32, bit_step, jnp.full((nb, 1), INT_MIN, I32))
    gt = key > tau
    tie = key == tau
    need = (n_sel - count(lambda x: x > tau)).astype(F32)
    utri = (lax.broadcasted_iota(I32, (LANES, LANES), 0)
            < lax.broadcasted_iota(I32, (LANES, LANES), 1)).astype(BF16)
    seen = jnp.zeros((nb, 1), F32)
    for c in range(past // LANES):
        cs = slice(c * LANES, (c + 1) * LANES)
        tief = jnp.where(tie[:, cs], 1.0, 0.0)
        rank = _dot(tief.astype(BF16), utri) + seen
        keep = jnp.logical_or(gt[:, cs], jnp.logical_and(tie[:, cs], rank < need))
        sel_ref[:, cs] = jnp.where(keep, 1.0, 0.0)
        seen = seen + jnp.sum(tief, axis=-1, keepdims=True)
    keep_self = jnp.logical_or(key_self > tau, jnp.logical_and(key_self == tau, seen < need))
    selself_ref[...] = jnp.where(keep_self, 1.0, 0.0)


def _dsa_sample_select(scores, qib, kib, wi, n_sel):
    nb, past = scores.shape
    full2 = lambda a, b: pl.BlockSpec((a, b), lambda i: (0, 0))
    return pl.pallas_call(
        functools.partial(_dsa_sample_select_kernel, n_sel),
        grid=(1,),
        in_specs=[full2(nb, past), full2(nb, BRANCH_W), full2(nb, LANES), full2(nb, 8)],
        out_specs=[full2(nb, past), full2(nb, 1)],
        out_shape=[jax.ShapeDtypeStruct((nb, past), F32), jax.ShapeDtypeStruct((nb, 1), F32)],
        compiler_params=_cparams(1),
        name="dsa_sample_select",
    )(scores, qib, kib, wi)


def _dsa_sample_attend_kernel(layer, n_pages, pt_ref, q_ref, sel_ref, selself_ref, kself_ref, vself_ref,
                              ck_ref, cv_ref, o_ref, kbuf_ref, vbuf_ref, ksem_ref, vsem_ref):
    slot = _paged_prefetch(pt_ref, layer, [(ck_ref, kbuf_ref, ksem_ref), (cv_ref, vbuf_ref, vsem_ref)], n_pages)
    head_of_lane = lax.broadcasted_iota(I32, (N_HEADS, BRANCH_W), 1) // HEAD_DIM
    own = head_of_lane == lax.broadcasted_iota(I32, (N_HEADS, BRANCH_W), 0)
    qrows = jnp.where(own, jnp.broadcast_to(q_ref[0], (N_HEADS, BRANCH_W)), 0.0).astype(BF16)
    lg = jnp.concatenate([_dot(qrows, kbuf_ref[slot, pg].astype(BF16)) for pg in range(n_pages)], axis=1)
    lg = jnp.where(sel_ref[0] > 0.0, lg, NEG_BIG)
    kself = kself_ref[0].astype(BF16).astype(F32)
    lself = jnp.sum(qrows.astype(F32) * kself, axis=-1, keepdims=True)
    lself = jnp.where(selself_ref[0] > 0.0, lself, NEG_BIG)
    m = jnp.maximum(jnp.max(lg, axis=-1, keepdims=True), lself)
    p = jnp.exp(lg - m)
    pself = jnp.exp(lself - m)
    denom = jnp.sum(p, axis=-1, keepdims=True) + pself
    p = p.astype(BF16)
    page = kbuf_ref.shape[3]
    o4 = pself * vself_ref[0]
    for pg in range(n_pages):
        o4 = o4 + _dot_nt(p[:, pg * page:(pg + 1) * page], vbuf_ref[slot, pg].astype(BF16))
    o_ref[0] = jnp.sum(jnp.where(own, o4 / denom, 0.0), axis=0, keepdims=True)


def _dsa_sample_attend(page_table, q, sel, selself, kself, vself, cache_k_t, cache_v_t, layer):
    nb, n_pages = page_table.shape
    page = cache_k_t.shape[3]
    past = n_pages * page
    row3 = lambda w: pl.BlockSpec((1, 1, w), lambda b, pt: (b, 0, 0))
    return pl.pallas_call(
        functools.partial(_dsa_sample_attend_kernel, layer, n_pages),
        grid_spec=pltpu.PrefetchScalarGridSpec(
            num_scalar_prefetch=1,
            grid=(nb,),
            in_specs=[row3(BRANCH_W), row3(past), row3(1), row3(BRANCH_W), row3(BRANCH_W),
                      pl.BlockSpec(memory_space=pl.ANY), pl.BlockSpec(memory_space=pl.ANY)],
            out_specs=row3(BRANCH_W),
            scratch_shapes=[pltpu.VMEM((2, n_pages, BRANCH_W, page), F32),
                            pltpu.VMEM((2, n_pages, BRANCH_W, page), F32),
                            pltpu.SemaphoreType.DMA((2,)),
                            pltpu.SemaphoreType.DMA((2,))]),
        out_shape=jax.ShapeDtypeStruct((nb, 1, BRANCH_W), F32),
        compiler_params=_cparams(1),
        name="dsa_sample_attend",
    )(page_table, q.reshape(nb, 1, BRANCH_W), sel.reshape(nb, 1, past), selself.reshape(nb, 1, 1),
      kself.reshape(nb, 1, BRANCH_W), vself.reshape(nb, 1, BRANCH_W), cache_k_t, cache_v_t)


def _rope_tables(pos, rot_dim, theta):
    half = rot_dim // 2
    inv = theta ** (-jnp.arange(half, dtype=F32) / half)
    ang = pos.astype(F32)[:, None] * inv[None, :]
    cos, sin = jnp.cos(ang), jnp.sin(ang)
    n = pos.shape[0]
    zh = jnp.zeros((n, half), F32)
    zr = jnp.zeros((n, HEAD_DIM - rot_dim), F32)
    c = jnp.concatenate([cos, cos, jnp.ones((n, HEAD_DIM - rot_dim), F32)], axis=1)
    lo = jnp.concatenate([-sin, zh, zr], axis=1)
    hi = jnp.concatenate([zh, sin, zr], axis=1)
    return tuple(jnp.tile(t, (1, N_HEADS)) for t in (c, lo, hi))


def _ret_decay(chunk):
    log_g = jnp.log1p(-jnp.exp2(-5.0 - jnp.arange(N_HEADS, dtype=F32)))
    n = jnp.arange(chunk, dtype=F32)
    diff = n[:, None] - n[None, :]
    dmask = jnp.where(diff >= 0, jnp.exp(log_g[:, None, None] * jnp.maximum(diff, 0.0)), 0.0)
    q_dec = jnp.exp(log_g[:, None] * (n + 1.0)).T
    k_dec = jnp.exp(log_g[:, None] * (chunk - 1.0 - n))
    s_dec = jnp.exp(log_g * chunk)
    qdec = jnp.repeat(q_dec, HEAD_DIM, axis=1)
    kdec = jnp.repeat(k_dec.T, HEAD_DIM, axis=1)
    sdec_rows = jnp.repeat(s_dec, HEAD_DIM)
    return dmask, qdec, kdec, sdec_rows


def _block_diag(blocks):
    n = len(blocks)
    rows = []
    for i, blk in enumerate(blocks):
        rows.append(jnp.concatenate([blk if j == i else jnp.zeros((blk.shape[0], blocks[j].shape[1]), blk.dtype)
                                     for j in range(n)], axis=1))
    return jnp.concatenate(rows, axis=0)


def kernel(x_prompt, x_sample, cache_k, cache_v, cache_kidx, state_wkv, state_shift, state_ret, state_pool, page_table, norm1_w, w_in, rwkv_mu, rwkv_w0, rwkv_w_w2, rwkv_a0, rwkv_w_a2, rwkv_w_g2, rwkv_k_k, rwkv_k_a, rwkv_r_k, rwkv_ln_w, rwkv_ln_b, q_norm_w, k_norm_w, idx_k_norm_w, ret_gn_w, ret_gn_b, pool_w, pool_scale, w_branch, w_out, norm2_w, w_ffn_in, w_ffn_out):
    n_seq, t_len, _ = x_prompt.shape
    nb = x_sample.shape[0]
    assert x_sample.shape[1] == 1
    page = cache_k.shape[2]
    past = page_table.shape[1] * page
    n_sel_sample = min(TOPK_MAX, (past + 1) // 4)
    tm_p = min(512, t_len)
    tm_m = min(256, t_len)

    ones_bd = _block_diag([jnp.ones((HEAD_DIM, HEAD_DIM), BF16)] * N_HEADS)
    pos_p = jnp.arange(t_len)
    pos_s = jnp.full((nb,), past)
    att_tab_p, att_tab_s = _rope_tables(pos_p, ROT_DIM, ROPE_THETA), _rope_tables(pos_s, ROT_DIM, ROPE_THETA)
    ret_tab_p, ret_tab_s = _rope_tables(pos_p, HEAD_DIM, RET_THETA), _rope_tables(pos_s, HEAD_DIM, RET_THETA)
    dmask, qdec, kdec, sdec_rows = _ret_decay(math.gcd(t_len, RET_CHUNK))
    dec_p = (dmask, qdec, kdec, jnp.broadcast_to(sdec_rows[:, None], (BRANCH_W, HEAD_DIM)))
    gamma_s = _ret_decay(1)[3].reshape(1, BRANCH_W)

    n_pool = cache_k.shape[1]
    cki_t = cache_kidx.transpose(0, 1, 3, 2)
    ck_t = cache_k.transpose(0, 1, 3, 4, 2).reshape(-1, n_pool, BRANCH_W, page)
    cv_t = cache_v.transpose(0, 1, 3, 4, 2).reshape(-1, n_pool, BRANCH_W, page)

    xp = x_prompt.reshape(n_seq * t_len, D_MODEL)
    xs = x_sample.reshape(nb, D_MODEL)
    zero_shift = jnp.zeros((n_seq, 1, RWKV_P), F32)
    zero_wkv = jnp.zeros((n_seq, HEAD_DIM, BRANCH_W), F32)
    prompt_new = [[] for _ in range(7)]
    sample_new = [[] for _ in range(7)]
    o_att, o_ret, o_pool = RWKV_P, RWKV_P + ATT_P, RWKV_P + ATT_P + RET_P
    o_gate = o_pool + BRANCH_W
    row1 = lambda a: a.reshape(1, -1)
    w_in_t = w_in.transpose(2, 0, 1)

    for l in range(DEPTH):
        w = w_in_t[:, l, :]
        w_pad = jnp.concatenate(
            [w[o_gate:], w[:RWKV_P], w[o_ret:o_pool], w[o_pool:o_gate], w[o_att:o_ret],
             jnp.zeros((ATT_P_PAD - ATT_P, D_MODEL), F32)], axis=0).astype(BF16)
        lw = dict(mu=row1(rwkv_mu[l]), w0=row1(rwkv_w0[l]),
                  w_wa=_block_diag([rwkv_w_w2[l], rwkv_w_a2[l]]).astype(BF16),
                  a0=row1(rwkv_a0[l]), w_g2=rwkv_w_g2[l].astype(BF16), k_k=row1(rwkv_k_k[l]),
                  k_a=row1(rwkv_k_a[l]), r_k=row1(rwkv_r_k[l]), ln_w=row1(rwkv_ln_w[l]), ln_b=row1(rwkv_ln_b[l]))
        wp_bd = _block_diag([pool_w[l, g] for g in range(N_HEADS)]).astype(BF16)
        wb = w_branch[l].astype(BF16)
        wo = w_out[l].astype(BF16)
        wfi = w_ffn_in[l].astype(BF16)
        wfo = w_ffn_out[l].astype(BF16)

        p = _norm_matmul(xp, norm1_w[l], w_pad, tm_p, P_PAD // 5)
        o_a, wkv = _rwkv_prompt(p, lw, ones_bd, n_seq, t_len)
        q, k, v, ki, kb, vt, qib, kib, wit = _att_prep(p, att_tab_p, q_norm_w[l], k_norm_w[l], idx_k_norm_w[l],
                                                     ones_bd, n_seq, t_len, tm_p)
        o_b = _dsa_prompt(q, qib, wit, kb, vt, kib, n_seq, t_len)
        o_c, ret = _ret_prompt(p, ret_tab_p, dec_p, ret_gn_w[l], ret_gn_b[l], ones_bd, n_seq, t_len)
        o_d, pbuf = _pool_prompt(p, wp_bd, pool_scale[l], n_seq, t_len, tm_p)
        x1 = _merge(xp, (o_a, o_b, o_c, o_d), p, wb, wo, tm_m)
        xp = _ffn(x1, norm2_w[l], wfi, wfo, tm_m)
        st = (k.reshape(n_seq, t_len, N_HEADS, HEAD_DIM), v.reshape(n_seq, t_len, N_HEADS, HEAD_DIM),
              ki[:, :HEAD_DIM].reshape(n_seq, t_len, HEAD_DIM),
              wkv.reshape(n_seq, HEAD_DIM, N_HEADS, HEAD_DIM).transpose(0, 2, 3, 1),
              p.reshape(n_seq, t_len, P_PAD)[:, -1, COL_RWKV:COL_RWKV + RWKV_P],
              ret.reshape(n_seq, N_HEADS, HEAD_DIM, HEAD_DIM),
              pbuf[:, POOL_HIST - POOL_BUF:, :])
        for lst, s in zip(prompt_new, st):
            lst.append(s)

        ps = _norm_matmul(xs, norm1_w[l], w_pad, nb, P_PAD // 5)
        wkv0 = state_wkv[l].transpose(0, 3, 1, 2).reshape(nb, HEAD_DIM, BRANCH_W)
        o_a, wkv = _rwkv(ps, state_shift[l], wkv0, lw, ones_bd, 1, nb, False)
        q, k, v, ki, kb, vt, qib, kib, wit = _att_prep(ps, att_tab_s, q_norm_w[l], k_norm_w[l], idx_k_norm_w[l],
                                                     ones_bd, 1, nb, nb)
        wi_rows = wit[0].T
        scores = _dsa_sample_scores(page_table, qib.reshape(nb, N_HEADS, HEAD_DIM),
                                    wi_rows[:, :N_HEADS].reshape(nb, N_HEADS, 1), cki_t, l)
        sel, selself = _dsa_sample_select(scores.reshape(nb, past), qib, kib, wi_rows, n_sel_sample)
        o_b = _dsa_sample_attend(page_table, q, sel, selself, k, v, ck_t, cv_t, l).reshape(nb, BRANCH_W)
        ret0 = state_ret[l].transpose(0, 2, 1, 3).reshape(nb, HEAD_DIM, BRANCH_W)
        o_c, ret = _ret_sample(ps, ret_tab_s, ret0, gamma_s, ret_gn_w[l], ret_gn_b[l], ones_bd)
        o_d, pbuf = _pool_sample(ps, state_pool[l].transpose(1, 0, 2), wp_bd, pool_scale[l], past)
        x1 = _merge(xs, (o_a, o_b, o_c, o_d), ps, wb, wo, nb)
        xs = _ffn(x1, norm2_w[l], wfi, wfo, nb)
        st = (k.reshape(nb, 1, N_HEADS, HEAD_DIM), v.reshape(nb, 1, N_HEADS, HEAD_DIM),
              ki[:, :HEAD_DIM].reshape(nb, 1, HEAD_DIM),
              wkv.reshape(nb, HEAD_DIM, N_HEADS, HEAD_DIM).transpose(0, 2, 3, 1),
              ps[:, COL_RWKV:COL_RWKV + RWKV_P],
              ret.reshape(nb, HEAD_DIM, N_HEADS, HEAD_DIM).transpose(0, 2, 1, 3),
              pbuf.transpose(1, 0, 2))
        for lst, s in zip(sample_new, st):
            lst.append(s)

    outs_p = [jnp.stack(a) for a in prompt_new]
    outs_s = [jnp.stack(a) for a in sample_new]
    return (xp.reshape(n_seq, t_len, D_MODEL), xs.reshape(nb, 1, D_MODEL), *outs_p, *outs_s)
```

```python
import functools
import math

import jax
import jax.numpy as jnp
from jax import lax
from jax.experimental import pallas as pl
from jax.experimental.pallas import tpu as pltpu

F32 = jnp.float32
BF16 = jnp.bfloat16
I32 = jnp.int32

D_MODEL = 1024
DEPTH = 4
HEAD_DIM = 64
N_HEADS = 4
BRANCH_W = N_HEADS * HEAD_DIM
N_BRANCH = 4
RWKV_DECAY_LORA = 64
RWKV_A_LORA = 64
RWKV_G_LORA = 128
RWKV_P = 3 * BRANCH_W + RWKV_DECAY_LORA + RWKV_A_LORA + RWKV_G_LORA
RWKV_LN_EPS = 64e-5
ATT_P = 3 * BRANCH_W + N_HEADS * HEAD_DIM + HEAD_DIM + N_HEADS
ATT_P_PAD = 1280
TOPK_MAX = 256
ROPE_THETA = 500000.0
ROT_DIM = HEAD_DIM // 4
ATT_SCALE = HEAD_DIM ** -0.5
IDX_SCALE = (N_HEADS * HEAD_DIM) ** -0.5
RET_P = 4 * BRANCH_W
RET_CHUNK = 128
RET_THETA = 10000.0
RET_KSCALE = HEAD_DIM ** -0.5
POOL_WINDOWS = (2, 4, 8, 16)
POOL_BUF = 15
GATE_P = N_BRANCH * D_MODEL
D_FF = -(-8 * D_MODEL // 768) * 256
NORM_EPS = 1e-6
GN_EPS = 1e-5

COL_GATE = 0
COL_RWKV = GATE_P
COL_RET = COL_RWKV + RWKV_P
COL_POOL = COL_RET + RET_P
COL_ATT = COL_POOL + BRANCH_W
P_PAD = COL_ATT + ATT_P_PAD

LANES = 128
INT_MIN = -2 ** 31
NEG_BIG = -1e30
VMEM_LIMIT = 56 * 1024 * 1024


def _cparams(n_axes):
    return pltpu.CompilerParams(dimension_semantics=("arbitrary",) * n_axes,
                                vmem_limit_bytes=VMEM_LIMIT)


def _dot(a, b):
    return jnp.dot(a, b, preferred_element_type=F32)


def _dot_nt(a, b):
    return lax.dot_general(a, b, (((1,), (1,)), ((), ())), preferred_element_type=F32)


def _seg_sum(x, ones_bd):
    hi = x.astype(BF16)
    lo = (x - hi.astype(F32)).astype(BF16)
    return _dot(hi, ones_bd) + _dot(lo, ones_bd)


def _head_group_norm(x, ones_bd, w, b, eps):
    mean = _seg_sum(x, ones_bd) * (1.0 / HEAD_DIM)
    xc = x - mean
    var = _seg_sum(xc * xc, ones_bd) * (1.0 / HEAD_DIM)
    return xc * lax.rsqrt(var + eps) * w + b


def _rope(x, cos, sin_lo, sin_hi, half):
    n = x.shape[-1]
    return x * cos + pltpu.roll(x, n - half, 1) * sin_lo + pltpu.roll(x, half, 1) * sin_hi


def _norm_matmul_kernel(x_ref, nw_ref, w_ref, o_ref, h_ref):
    @pl.when(pl.program_id(1) == 0)
    def _():
        x = x_ref[...]
        y = x * lax.rsqrt(jnp.mean(x * x, axis=-1, keepdims=True) + NORM_EPS)
        h_ref[...] = (y * nw_ref[...]).astype(BF16)

    o_ref[...] = _dot_nt(h_ref[...], w_ref[...])


def _norm_matmul(x, norm_w, w_t_bf16, tm, tn):
    n, d = x.shape
    p = w_t_bf16.shape[0]
    return pl.pallas_call(
        _norm_matmul_kernel,
        grid=(n // tm, p // tn),
        in_specs=[pl.BlockSpec((tm, d), lambda i, j: (i, 0)),
                  pl.BlockSpec((1, d), lambda i, j: (0, 0)),
                  pl.BlockSpec((tn, d), lambda i, j: (j, 0))],
        out_specs=pl.BlockSpec((tm, tn), lambda i, j: (i, j)),
        out_shape=jax.ShapeDtypeStruct((n, p), F32),
        scratch_shapes=[pltpu.VMEM((tm, d), BF16)],
        compiler_params=_cparams(2),
        name="norm_in_proj",
    )(x, norm_w.reshape(1, d), w_t_bf16)


def _merge_kernel(x_ref, oa_ref, ob_ref, oc_ref, od_ref, gate_ref, wb_ref, wo_ref, out_ref):
    acc = None
    for b, o_ref in enumerate((oa_ref, ob_ref, oc_ref, od_ref)):
        up = _dot(o_ref[...].astype(BF16), wb_ref[b])
        g = jax.nn.sigmoid(gate_ref[:, b * D_MODEL:(b + 1) * D_MODEL])
        acc = g * up if acc is None else acc + g * up
    out_ref[...] = x_ref[...] + _dot(acc.astype(BF16), wo_ref[...])


def _merge(x, branches, p, w_branch_bf16, w_out_bf16, tm):
    n = x.shape[0]
    row = lambda w: pl.BlockSpec((tm, w), lambda i: (i, 0))
    return pl.pallas_call(
        _merge_kernel,
        grid=(n // tm,),
        in_specs=[row(D_MODEL), row(BRANCH_W), row(BRANCH_W), row(BRANCH_W), row(BRANCH_W),
                  pl.BlockSpec((tm, GATE_P), lambda i: (i, COL_GATE // GATE_P)),
                  pl.BlockSpec((N_BRANCH, BRANCH_W, D_MODEL), lambda i: (0, 0, 0)),
                  pl.BlockSpec((D_MODEL, D_MODEL), lambda i: (0, 0))],
        out_specs=row(D_MODEL),
        out_shape=jax.ShapeDtypeStruct((n, D_MODEL), F32),
        compiler_params=_cparams(1),
        name="gated_merge_out_proj",
    )(x, *branches, p, w_branch_bf16, w_out_bf16)


FF_CHUNK = D_FF // 2


def _ffn_kernel(x_ref, nw_ref, wi_ref, wo_ref, out_ref):
    x = x_ref[...]
    h = (x * lax.rsqrt(jnp.mean(x * x, axis=-1, keepdims=True) + NORM_EPS) * nw_ref[...]).astype(BF16)
    acc = x
    for c in range(D_FF // FF_CHUNK):
        g = _dot(h, wi_ref[:, c * FF_CHUNK:(c + 1) * FF_CHUNK])
        u = _dot(h, wi_ref[:, D_FF + c * FF_CHUNK:D_FF + (c + 1) * FF_CHUNK])
        a = (g * jax.nn.sigmoid(g) * u).astype(BF16)
        acc = acc + _dot(a, wo_ref[c * FF_CHUNK:(c + 1) * FF_CHUNK, :])
    out_ref[...] = acc


def _ffn(x, norm_w, w_in_bf16, w_out_bf16, tm):
    n = x.shape[0]
    return pl.pallas_call(
        _ffn_kernel,
        grid=(n // tm,),
        in_specs=[pl.BlockSpec((tm, D_MODEL), lambda i: (i, 0)),
                  pl.BlockSpec((1, D_MODEL), lambda i: (0, 0)),
                  pl.BlockSpec((D_MODEL, 2 * D_FF), lambda i: (0, 0)),
                  pl.BlockSpec((D_FF, D_MODEL), lambda i: (0, 0))],
        out_specs=pl.BlockSpec((tm, D_MODEL), lambda i: (i, 0)),
        out_shape=jax.ShapeDtypeStruct((n, D_MODEL), F32),
        compiler_params=_cparams(1),
        name="swiglu_ffn",
    )(x, norm_w.reshape(1, D_MODEL), w_in_bf16, w_out_bf16)


def _att_prep_kernel(p_ref, cos_ref, slo_ref, shi_ref, qnw_ref, knw_ref, inw_ref, ones_ref,
                     q_ref, k_ref, v_ref, ki_ref, kb_ref, vt_ref, qib_ref, kib_ref, wit_ref):
    cos, slo, shi = cos_ref[...], slo_ref[...], shi_ref[...]
    ones_bd = ones_ref[...]
    half = ROT_DIM // 2

    def head_rms(x, w):
        ms = _seg_sum(x * x, ones_bd) * (1.0 / HEAD_DIM)
        return x * lax.rsqrt(ms + NORM_EPS) * w

    q = _rope(head_rms(p_ref[:, 0:256], qnw_ref[...]), cos, slo, shi, half)
    k = _rope(head_rms(p_ref[:, 256:512], knw_ref[...]), cos, slo, shi, half)
    v = p_ref[:, 512:768]
    qi = _rope(p_ref[:, 768:1024], cos, slo, shi, half)
    tail = p_ref[:, 1024:1152]
    lane = lax.broadcasted_iota(I32, tail.shape, 1)
    is_ki = lane < HEAD_DIM
    kiraw = jnp.where(is_ki, tail, 0.0)
    ms = jnp.sum(kiraw * kiraw, axis=-1, keepdims=True) * (1.0 / HEAD_DIM)
    kin = kiraw * lax.rsqrt(ms + NORM_EPS) * inw_ref[...]
    ki = _rope(kin, cos[:, :LANES], slo[:, :LANES], shi[:, :LANES], half)
    ki = jnp.where(is_ki, ki, 0.0)

    q_ref[...] = q * ATT_SCALE
    k_ref[...] = k
    v_ref[...] = v
    ki_ref[...] = ki
    kb_ref[...] = k.astype(BF16)
    vt_ref[0] = v.T.astype(BF16)
    qib_ref[...] = qi.astype(BF16)
    kib_ref[...] = ki.astype(BF16)
    wi = jnp.where(is_ki, 0.0, tail) * IDX_SCALE
    wit_ref[0] = pltpu.roll(wi, LANES - HEAD_DIM, 1).T[0:8, :]


def _att_prep(p, tables, q_norm_w, k_norm_w, idx_k_norm_w, ones_bd, n_seq, t_len, tm):
    n = p.shape[0]
    tpb = t_len // tm
    cos, slo, shi = tables
    tile4 = lambda w: jnp.tile(w.reshape(1, HEAD_DIM), (1, N_HEADS))
    inw = jnp.concatenate([idx_k_norm_w.reshape(1, HEAD_DIM), jnp.ones((1, LANES - HEAD_DIM), F32)], axis=1)
    row = lambda w: pl.BlockSpec((tm, w), lambda i: (i, 0))
    tab = pl.BlockSpec((tm, BRANCH_W), lambda i: (i % tpb, 0))
    full = lambda a, b: pl.BlockSpec((a, b), lambda i: (0, 0))
    return pl.pallas_call(
        _att_prep_kernel,
        grid=(n // tm,),
        in_specs=[pl.BlockSpec((tm, ATT_P_PAD), lambda i: (i, COL_ATT // ATT_P_PAD)),
                  tab, tab, tab, full(1, BRANCH_W), full(1, BRANCH_W), full(1, LANES),
                  full(BRANCH_W, BRANCH_W)],
        out_specs=[row(BRANCH_W), row(BRANCH_W), row(BRANCH_W), row(LANES), row(BRANCH_W),
                   pl.BlockSpec((1, BRANCH_W, tm), lambda i: (i // tpb, 0, i % tpb)),
                   row(BRANCH_W), row(LANES),
                   pl.BlockSpec((1, 8, tm), lambda i: (i // tpb, 0, i % tpb))],
        out_shape=[jax.ShapeDtypeStruct((n, BRANCH_W), F32),
                   jax.ShapeDtypeStruct((n, BRANCH_W), F32),
                   jax.ShapeDtypeStruct((n, BRANCH_W), F32),
                   jax.ShapeDtypeStruct((n, LANES), F32),
                   jax.ShapeDtypeStruct((n, BRANCH_W), BF16),
                   jax.ShapeDtypeStruct((n_seq, BRANCH_W, t_len), BF16),
                   jax.ShapeDtypeStruct((n, BRANCH_W), BF16),
                   jax.ShapeDtypeStruct((n, LANES), BF16),
                   jax.ShapeDtypeStruct((n_seq, 8, t_len), F32)],
        compiler_params=_cparams(1),
        name="dsa_prep",
    )(p, cos, slo, shi, tile4(q_norm_w), tile4(k_norm_w), inw, ones_bd)


QB = 128


def _float_key(x):
    b = lax.bitcast_convert_type(x, I32)
    b = jnp.where(b == INT_MIN, 0, b)
    return jnp.where(b < 0, b ^ jnp.int32(0x7FFFFFFF), b)


KT = 512
KT_MM = 1024
PROBES_PER_CHECK = 4
MAX_PROBES = 48
NO_COUNT = 2 ** 30


def _dsa_prompt_kernel(n_sel, q_ref, qib_ref, wit_ref, kb_ref, vt_ref, kib_ref, o_ref,
                       key_ref, qbd_ref, acc_ref, m_ref, l_ref):
    j = pl.program_id(1)
    nkt = (j + KT // QB) // (KT // QB)
    nkt_mm = (j + KT_MM // QB) // (KT_MM // QB)
    q0 = j * QB

    @pl.when(jnp.logical_and(pl.program_id(0) == 0, j == 0))
    def _():
        qbd_ref[...] = jnp.zeros_like(qbd_ref)

    qib = qib_ref[...]
    qis = jnp.concatenate([qib[:, h * HEAD_DIM:(h + 1) * HEAD_DIM] for h in range(N_HEADS)], axis=0)
    qt = q_ref[...].T.astype(BF16)
    for h in range(N_HEADS):
        qbd_ref[h * HEAD_DIM:(h + 1) * HEAD_DIM, h * QB:(h + 1) * QB] = qt[h * HEAD_DIM:(h + 1) * HEAD_DIM, :]
    wit = wit_ref[0]
    srow = lax.broadcasted_iota(I32, (QB, QB), 0)
    tcol = lax.broadcasted_iota(I32, (QB, QB), 1) + q0
    tq = lax.broadcasted_iota(I32, (1, QB), 1) + q0

    def score_tile(kt, carry):
        subs = range(KT_MM // QB)
        k0 = [pl.multiple_of(kt * KT_MM + u * QB, QB) for u in subs]
        s = [_dot_nt(kib_ref[0, pl.ds(k0[u], QB), :][:, :HEAD_DIM], qis) for u in subs]
        for u in subs:
            acc = jnp.zeros((QB, QB), F32)
            for h in range(N_HEADS):
                acc = acc + jnp.maximum(s[u][:, h * QB:(h + 1) * QB], 0.0) * wit[h:h + 1, :]
            key_ref[pl.ds(k0[u], QB), :] = jnp.where(srow + k0[u] <= tcol, _float_key(acc), jnp.int32(INT_MIN))
        return carry

    lax.fori_loop(0, nkt_mm, score_tile, 0)

    def count(pred):
        def body(kt, acc):
            k0 = pl.multiple_of(kt * KT, KT)
            m = pred(key_ref[pl.ds(k0, KT), :]).astype(I32)
            return acc + jnp.sum(m.reshape(KT // 8, 8, QB), axis=0)
        acc = lax.fori_loop(0, nkt, body, jnp.zeros((8, QB), I32))
        return jnp.sum(acc, axis=0, keepdims=True)

    def colmax_body(kt, acc):
        k0 = pl.multiple_of(kt * KT, KT)
        return jnp.maximum(acc, jnp.max(key_ref[pl.ds(k0, KT), :].reshape(KT // 8, 8, QB), axis=0))

    kmax = jnp.max(lax.fori_loop(0, nkt, colmax_body, jnp.full((8, QB), INT_MIN, I32)), axis=0, keepdims=True)
    kmax = jnp.minimum(kmax, jnp.int32(2 ** 31 - 2))
    hi0 = kmax + 1

    def unfinished(st):
        return jnp.logical_and(st[0] < MAX_PROBES, jnp.logical_not(st[6]))

    def probe_step(st):
        i, lo, hi, c_lo, c_hi, done, _ = st
        for _ in range(PROBES_PER_CHECK):
            e = jnp.maximum(i, 2) + 21
            step = jnp.int32(1) << jnp.minimum(e, 30)
            base = jnp.where(hi <= 1, hi, hi0)
            down = jnp.where(jnp.logical_and(e <= 30, base >= INT_MIN + step), base - step, jnp.int32(INT_MIN))
            mid = (lo >> 1) + (hi >> 1) + ((lo | hi) & 1)
            x = jnp.where(c_lo == NO_COUNT, jnp.where(hi == 1, 0, down), mid)
            x = jnp.where(i == 0, kmax, jnp.where(i == 1, 1, x))
            c = count(lambda blk: blk >= x)
            live = done == 0
            up = jnp.logical_and(jnp.logical_and(live, c >= n_sel), i != 1)
            dn = jnp.logical_and(jnp.logical_and(live, c < n_sel), x < hi)
            lo = jnp.where(up, x, lo)
            c_lo = jnp.where(up, c, c_lo)
            hi = jnp.where(dn, x, hi)
            c_hi = jnp.where(dn, c, c_hi)
            done = jnp.where(jnp.logical_or(c_lo == n_sel, hi - 1 == lo), 1, done)
            i = i + 1
        return i, lo, hi, c_lo, c_hi, done, jnp.min(done) > 0

    few = tq + 1 <= n_sel
    done0 = jnp.where(few, 1, 0)
    init = (jnp.int32(0), jnp.full((1, QB), INT_MIN, I32), hi0, jnp.where(few, n_sel, jnp.int32(NO_COUNT)),
            jnp.zeros((1, QB), I32), done0, jnp.min(done0) > 0)
    _, tau, _, c_lo, c_hi, _, _ = lax.while_loop(unfinished, probe_step, init)

    tied = c_lo != n_sel

    @pl.when(jnp.max(jnp.where(tied, 1, 0)) > 0)
    def _():
        need = jnp.where(tied, n_sel - c_hi, jnp.int32(NO_COUNT)).astype(F32)
        ltri = (lax.broadcasted_iota(I32, (QB, QB), 1) < srow).astype(BF16)

        def demote_tile(kt, seen):
            subs = range(KT_MM // QB)
            k0 = [pl.multiple_of(kt * KT_MM + u * QB, QB) for u in subs]
            blk = [key_ref[pl.ds(k0[u], QB), :] for u in subs]
            tief = [jnp.where(blk[u] == tau, 1.0, 0.0) for u in subs]
            before = [_dot(ltri, tief[u].astype(BF16)) for u in subs]
            for u in subs:
                drop = jnp.logical_and(blk[u] == tau, before[u] + seen >= need)
                key_ref[pl.ds(k0[u], QB), :] = jnp.where(drop, jnp.int32(INT_MIN), blk[u])
                seen = seen + jnp.sum(tief[u], axis=0, keepdims=True)
            return seen

        lax.fori_loop(0, nkt_mm, demote_tile, jnp.zeros((1, QB), F32))

    thr = jnp.maximum(tau, INT_MIN + 1)
    acc_ref[...] = jnp.zeros_like(acc_ref)
    m_ref[...] = jnp.full_like(m_ref, NEG_BIG)
    l_ref[...] = jnp.zeros_like(l_ref)

    def attend(kt, carry):
        k0 = pl.multiple_of(kt * KT_MM, KT_MM)
        sel = key_ref[pl.ds(k0, KT_MM), :] >= thr
        lg = _dot(kb_ref[0, pl.ds(k0, KT_MM), :], qbd_ref[...])
        heads = range(N_HEADS)
        hs = [slice(h * HEAD_DIM, (h + 1) * HEAD_DIM) for h in heads]
        lh = [jnp.where(sel, lg[:, h * QB:(h + 1) * QB], NEG_BIG) for h in heads]
        m_old = [m_ref[h:h + 1, :] for h in heads]
        m_new = [jnp.maximum(m_old[h], jnp.max(lh[h], axis=0, keepdims=True)) for h in heads]
        p = [jnp.exp(lh[h] - m_new[h]) for h in heads]
        pv = [_dot(vt_ref[0, hs[h], pl.ds(k0, KT_MM)], p[h].astype(BF16)) for h in heads]
        alpha = [jnp.exp(m_old[h] - m_new[h]) for h in heads]
        for h in heads:
            l_ref[h:h + 1, :] = alpha[h] * l_ref[h:h + 1, :] + jnp.sum(p[h], axis=0, keepdims=True)
            m_ref[h:h + 1, :] = m_new[h]
            acc_ref[hs[h], :] = acc_ref[hs[h], :] * alpha[h] + pv[h]
        return carry

    lax.fori_loop(0, nkt_mm, attend, 0)

    outs = [acc_ref[h * HEAD_DIM:(h + 1) * HEAD_DIM, :] / l_ref[h:h + 1, :] for h in range(N_HEADS)]
    o_ref[...] = jnp.concatenate(outs, axis=0).T


def _dsa_prompt(q, qib, wit, kb, vt, kib, n_seq, t_len):
    assert t_len % KT_MM == 0 and KT_MM % KT == 0
    n_sel = min(TOPK_MAX, t_len // 4)
    nblk = t_len // QB
    return pl.pallas_call(
        functools.partial(_dsa_prompt_kernel, n_sel),
        grid=(n_seq, nblk),
        in_specs=[pl.BlockSpec((QB, BRANCH_W), lambda b, j: (b * nblk + j, 0)),
                  pl.BlockSpec((QB, BRANCH_W), lambda b, j: (b * nblk + j, 0)),
                  pl.BlockSpec((1, 8, QB), lambda b, j: (b, 0, j)),
                  pl.BlockSpec((1, t_len, BRANCH_W), lambda b, j: (b, 0, 0)),
                  pl.BlockSpec((1, BRANCH_W, t_len), lambda b, j: (b, 0, 0)),
                  pl.BlockSpec((1, t_len, LANES), lambda b, j: (b, 0, 0))],
        out_specs=pl.BlockSpec((QB, BRANCH_W), lambda b, j: (b * nblk + j, 0)),
        out_shape=jax.ShapeDtypeStruct((n_seq * t_len, BRANCH_W), F32),
        scratch_shapes=[pltpu.VMEM((t_len, QB), I32),
                        pltpu.VMEM((BRANCH_W, N_HEADS * QB), BF16),
                        pltpu.VMEM((BRANCH_W, QB), F32),
                        pltpu.VMEM((8, QB), F32),
                        pltpu.VMEM((8, QB), F32)],
        compiler_params=_cparams(2),
        name="dsa_prompt",
    )(q, qib, wit, kb.reshape(n_seq, t_len, BRANCH_W), vt, kib.reshape(n_seq, t_len, LANES))


RW_CHUNK = 128
RW_GROUP = 8


def _col_bcast(xt, i):
    lane = lax.broadcasted_iota(I32, (HEAD_DIM, LANES), 1)
    halves = []
    for pair in range(2):
        r0 = 2 * pair * HEAD_DIM
        c0 = jnp.broadcast_to(xt[r0:r0 + HEAD_DIM, i:i + 1], (HEAD_DIM, LANES))
        c1 = jnp.broadcast_to(xt[r0 + HEAD_DIM:r0 + 2 * HEAD_DIM, i:i + 1], (HEAD_DIM, LANES))
        halves.append(jnp.where(lane < HEAD_DIM, c0, c1))
    return jnp.concatenate(halves, axis=1)


def _group_cols(xt_ref, n_arr, gi):
    sh = lax.rem(LANES - gi * RW_GROUP, LANES)
    return [pltpu.roll(xt_ref[a], sh, 1) for a in range(n_arr)]


def _rwkv_prep(p, prev, mu, w0, w_wa, a0, w_g2, k_k, k_a, r_k, ones_bd):
    xl = p + (prev - p) * mu
    r, k, v = xl[:, 0:256], xl[:, 256:512], xl[:, 512:768]
    wa = xl[:, 768:896]
    lane = lax.broadcasted_iota(I32, wa.shape, 1)
    wa = jnp.where(lane < RWKV_DECAY_LORA, jnp.tanh(wa), wa)
    z = _dot(wa.astype(BF16), w_wa)
    nz = -(w0 + z[:, 0:256])
    w_log = -(jnp.maximum(nz, 0.0) + jnp.log1p(jnp.exp(-jnp.abs(nz)))) - 0.5
    log_decay = -jnp.exp(w_log)
    a = jax.nn.sigmoid(a0 + z[:, 256:512])
    g = _dot(jax.nn.sigmoid(xl[:, 896:1024]).astype(BF16), w_g2)
    kk = k * k_k
    kk = kk / jnp.sqrt(_seg_sum(kk * kk, ones_bd) + 1e-12)
    kp = k * (1.0 + (a - 1.0) * k_a)
    bonus = _seg_sum(r * kp * r_k, ones_bd) * v
    return r, log_decay, kp, v, kk, kk * a, g, bonus


def _rwkv_sample_kernel(p_ref, prev_ref, st0_ref, mu_ref, w0_ref, wwa_ref, a0_ref, wg2_ref,
                        kk_ref, ka_ref, rk_ref, lnw_ref, lnb_ref, ones_ref,
                        o_ref, st_out_ref, xt_ref, v_ref, y_ref):
    ones_bd = ones_ref[...]
    r, log_decay, kp, v, kk, kka, g, bonus = _rwkv_prep(
        p_ref[...], prev_ref[...], mu_ref[...], w0_ref[...], wwa_ref[...], a0_ref[...], wg2_ref[...],
        kk_ref[...], ka_ref[...], rk_ref[...], ones_bd)
    for a, arr in enumerate((r, jnp.exp(log_decay), kp, kk, kka)):
        xt_ref[a] = arr.T
    v_ref[...] = v

    def group(gi, carry):
        rt, wt, kt, kkt, kkat = _group_cols(xt_ref, 5, gi)
        base = pl.multiple_of(gi * RW_GROUP, RW_GROUP)
        vrows = v_ref[pl.ds(base, RW_GROUP), :]
        ys = []
        for i in range(RW_GROUP):
            st = st0_ref[base + i]
            sa = jnp.sum(st * _col_bcast(kkt, i), axis=0, keepdims=True)
            st = st * _col_bcast(wt, i) - _col_bcast(kkat, i) * sa + _col_bcast(kt, i) * vrows[i:i + 1, :]
            ys.append(jnp.sum(st * _col_bcast(rt, i), axis=0, keepdims=True))
            st_out_ref[base + i] = st
        y_ref[pl.ds(base, RW_GROUP), :] = jnp.concatenate(ys, axis=0)
        return carry

    lax.fori_loop(0, RW_CHUNK // RW_GROUP, group, 0)
    o = _head_group_norm(y_ref[...], ones_bd, lnw_ref[...], lnb_ref[...], RWKV_LN_EPS)
    o_ref[...] = (o + bonus) * g


def _rwkv_sample(p, prev, st0, lw, ones_bd):
    n = st0.shape[0]
    assert n == RW_CHUNK
    full2 = lambda a, b: pl.BlockSpec((a, b), lambda i: (0, 0))
    st_spec = pl.BlockSpec((n, HEAD_DIM, BRANCH_W), lambda i: (0, 0, 0))
    return pl.pallas_call(
        _rwkv_sample_kernel,
        grid=(1,),
        in_specs=[pl.BlockSpec((n, RWKV_P), lambda i: (0, COL_RWKV // RWKV_P)),
                  full2(n, RWKV_P), st_spec,
                  full2(1, RWKV_P), full2(1, BRANCH_W), full2(LANES, 2 * BRANCH_W), full2(1, BRANCH_W),
                  full2(RWKV_G_LORA, BRANCH_W), full2(1, BRANCH_W), full2(1, BRANCH_W), full2(1, BRANCH_W),
                  full2(1, BRANCH_W), full2(1, BRANCH_W), full2(BRANCH_W, BRANCH_W)],
        out_specs=[full2(n, BRANCH_W), st_spec],
        out_shape=[jax.ShapeDtypeStruct((n, BRANCH_W), F32),
                   jax.ShapeDtypeStruct(st0.shape, F32)],
        scratch_shapes=[pltpu.VMEM((5, BRANCH_W, RW_CHUNK), F32),
                        pltpu.VMEM((RW_CHUNK, BRANCH_W), F32),
                        pltpu.VMEM((RW_CHUNK, BRANCH_W), F32)],
        compiler_params=_cparams(1),
        name="rwkv7_sample",
    )(p, prev, st0, lw["mu"], lw["w0"], lw["w_wa"], lw["a0"], lw["w_g2"], lw["k_k"], lw["k_a"],
      lw["r_k"], lw["ln_w"], lw["ln_b"], ones_bd)


RC = 64


def _mm(a, b):
    return _dot(a.astype(BF16), b.astype(BF16))


def _rwkv_chunk_kernel(p_ref, mu_ref, w0_ref, wwa_ref, a0_ref, wg2_ref, kk_ref, ka_ref, rk_ref,
                       lnw_ref, lnb_ref, ones_ref, o_ref, st_out_ref, carry_ref, z_ref):
    ns = p_ref.shape[0]
    seqs = range(ns)
    c = pl.program_id(0)

    @pl.when(c == 0)
    def _():
        carry_ref[...] = jnp.zeros_like(carry_ref)
        z_ref[...] = jnp.zeros_like(z_ref)

    rows = ns * RC
    p = p_ref[...].reshape(rows, RWKV_P)
    t_loc = lax.rem(lax.broadcasted_iota(I32, (rows, RWKV_P), 0), RC)
    seq_of_row = lax.broadcasted_iota(I32, (rows, RWKV_P), 0) // RC
    first = carry_ref[0]
    for s in range(1, ns):
        first = jnp.where(seq_of_row == s, carry_ref[s], first)
    prev = jnp.where(t_loc == 0, first, pltpu.roll(p, 1, 0))
    for s in seqs:
        carry_ref[s] = p[s * RC + RC - 1:(s + 1) * RC, :]
    ones_bd = ones_ref[...]
    r, log_decay, kp, v, kk, kka, g, bonus = _rwkv_prep(
        p, prev, mu_ref[...], w0_ref[...], wwa_ref[...], a0_ref[...], wg2_ref[...],
        kk_ref[...], ka_ref[...], rk_ref[...], ones_bd)

    trow = lax.rem(lax.broadcasted_iota(I32, (rows, BRANCH_W), 0), RC)
    cs = log_decay
    for sh in (1, 2, 4, 8, 16, 32):
        cs = cs + jnp.where(trow >= sh, pltpu.roll(cs, sh, 0), 0.0)
    pw = jnp.exp(cs)
    inv_pw = jnp.exp(-cs)
    a_m = -(kk * jnp.exp(cs - log_decay))
    r_m = r * pw
    b_m = kka * inv_pw
    k_m = kp * inv_pw
    sl = lambda x, s: x[s * RC:(s + 1) * RC]
    bkt = [jnp.concatenate([sl(b_m, s), sl(k_m, s)], axis=0).T for s in seqs]
    kbt = [jnp.concatenate([sl(k_m, s), sl(b_m, s)], axis=0).T for s in seqs]
    pwt = [jnp.concatenate([sl(pw, s), sl(pw, s)], axis=0).T for s in seqs]

    lane_head = lax.broadcasted_iota(I32, (RC, BRANCH_W), 1) // HEAD_DIM
    hm = [lane_head == h for h in range(N_HEADS)]
    xstack = [jnp.concatenate([jnp.where(hm[h], sl(a_m, s), 0.0) for h in range(N_HEADS)]
                              + [jnp.where(hm[h], sl(r_m, s), 0.0) for h in range(N_HEADS)], axis=0)
              for s in seqs]
    g1 = [_mm(xstack[s], bkt[s]) for s in seqs]
    g2 = [_mm(xstack[s], kbt[s]) for s in seqs]
    nst = N_HEADS * RC
    t_of_row = lax.rem(lax.broadcasted_iota(I32, (nst, LANES), 0), RC)
    s_of_lane = lax.broadcasted_iota(I32, (nst, LANES), 1)
    strict = s_of_lane < t_of_row
    incl = jnp.logical_and(s_of_lane <= t_of_row, s_of_lane < RC)
    l_st = [jnp.where(strict, g1[s][0:nst], 0.0) for s in seqs]
    m_st = [jnp.where(strict, g2[s][0:nst], 0.0) for s in seqs]
    n_st = [jnp.where(incl, g1[s][nst:2 * nst], 0.0) for s in seqs]
    q_st = [jnp.where(incl, g2[s][nst:2 * nst], 0.0) for s in seqs]

    eye = jnp.where(lax.broadcasted_iota(I32, (RC, LANES), 0) == lax.broadcasted_iota(I32, (RC, LANES), 1),
                    1.0, 0.0)
    lps = [l_st[s][h * RC:(h + 1) * RC] for s in seqs for h in range(N_HEADS)]
    ws = [eye + lp for lp in lps]
    for _ in range(5):
        lps = [_mm(lp[:, :RC], lp) for lp in lps]
        ws = [w_h + _mm(w_h[:, :RC], lp) for w_h, lp in zip(ws, lps)]

    def per_head(stacked):
        out = jnp.where(hm[0], stacked[0:RC], 0.0)
        for h in range(1, N_HEADS):
            out = out + jnp.where(hm[h], stacked[h * RC:(h + 1) * RC], 0.0)
        return out

    row_head = lax.broadcasted_iota(I32, (BRANCH_W, BRANCH_W), 0) // HEAD_DIM
    col_head = lax.broadcasted_iota(I32, (BRANCH_W, BRANCH_W), 1) // HEAD_DIM
    z0 = [z_ref[s] for s in seqs]
    zbd = [jnp.where(row_head == col_head, jnp.concatenate([z0[s]] * N_HEADS, axis=0), 0.0) for s in seqs]
    vs = [sl(v, s) for s in seqs]
    az = [_mm(sl(a_m, s), zbd[s]) for s in seqs]
    mv = [_mm(m_st[s][:, :RC], vs[s]) for s in seqs]
    rz = [_mm(sl(r_m, s), zbd[s]) for s in seqs]
    qv = [_mm(q_st[s][:, :RC], vs[s]) for s in seqs]
    rhs = [az[s] + per_head(mv[s]) for s in seqs]
    wr = [[_mm(ws[s * N_HEADS + h][:, :RC], rhs[s]) for h in range(N_HEADS)] for s in seqs]
    u = [per_head(jnp.concatenate(wr[s], axis=0)) for s in seqs]
    nu = [_mm(n_st[s][:, :RC], u[s]) for s in seqs]
    uv = [jnp.concatenate([u[s], vs[s]], axis=0) for s in seqs]
    upd = [[_mm(bkt[s][h * HEAD_DIM:(h + 1) * HEAD_DIM, :], uv[s]) for h in range(N_HEADS)] for s in seqs]
    y = jnp.concatenate([rz[s] + per_head(nu[s] + qv[s]) for s in seqs], axis=0)
    z_new = [_col_bcast(pwt[s], RC - 1) * (z0[s] + per_head(jnp.concatenate(upd[s], axis=0))) for s in seqs]
    for s in seqs:
        z_ref[s] = z_new[s]

    @pl.when(c == pl.num_programs(0) - 1)
    def _():
        for s in seqs:
            st_out_ref[s] = z_new[s]

    o = (_head_group_norm(y, ones_bd, lnw_ref[...], lnb_ref[...], RWKV_LN_EPS) + bonus) * g
    o_ref[...] = o.reshape(ns, RC, BRANCH_W)


def _rwkv_prompt(p, lw, ones_bd, n_seq, t_len):
    nchunk = t_len // RC
    full2 = lambda a, b: pl.BlockSpec((a, b), lambda c: (0, 0))
    st_spec = pl.BlockSpec((n_seq, HEAD_DIM, BRANCH_W), lambda c: (0, 0, 0))
    o, st = pl.pallas_call(
        _rwkv_chunk_kernel,
        grid=(nchunk,),
        in_specs=[pl.BlockSpec((n_seq, RC, RWKV_P), lambda c: (0, c, COL_RWKV // RWKV_P)),
                  full2(1, RWKV_P), full2(1, BRANCH_W), full2(LANES, 2 * BRANCH_W), full2(1, BRANCH_W),
                  full2(RWKV_G_LORA, BRANCH_W), full2(1, BRANCH_W), full2(1, BRANCH_W), full2(1, BRANCH_W),
                  full2(1, BRANCH_W), full2(1, BRANCH_W), full2(BRANCH_W, BRANCH_W)],
        out_specs=[pl.BlockSpec((n_seq, RC, BRANCH_W), lambda c: (0, c, 0)), st_spec],
        out_shape=[jax.ShapeDtypeStruct((n_seq, t_len, BRANCH_W), F32),
                   jax.ShapeDtypeStruct((n_seq, HEAD_DIM, BRANCH_W), F32)],
        scratch_shapes=[pltpu.VMEM((n_seq, 1, RWKV_P), F32),
                        pltpu.VMEM((n_seq, HEAD_DIM, BRANCH_W), F32)],
        compiler_params=_cparams(1),
        name="rwkv7_prompt",
    )(p.reshape(n_seq, t_len, P_PAD), lw["mu"], lw["w0"], lw["w_wa"], lw["a0"], lw["w_g2"], lw["k_k"], lw["k_a"],
      lw["r_k"], lw["ln_w"], lw["ln_b"], ones_bd)
    return o.reshape(n_seq * t_len, BRANCH_W), st


def _ret_prompt_kernel(p_ref, cos_ref, slo_ref, shi_ref, dmask_ref, qdec_ref, kdec_ref, sdec_ref,
                       gnw_ref, gnb_ref, ones_ref, o_ref, s_out_ref, s_ref):
    c = pl.program_id(1)

    @pl.when(c == 0)
    def _():
        s_ref[...] = jnp.zeros_like(s_ref)

    cos, slo, shi = cos_ref[...], slo_ref[...], shi_ref[...]
    half = HEAD_DIM // 2
    rq = _rope(p_ref[:, 0:256], cos, slo, shi, half)
    rk = _rope(p_ref[:, 256:512], cos, slo, shi, half) * RET_KSCALE
    rv = p_ref[:, 512:768]
    rg = p_ref[:, 768:1024]
    kdt = (rk * kdec_ref[...]).T.astype(BF16)
    qdec = qdec_ref[...]
    heads = range(N_HEADS)
    hs = [slice(h * HEAD_DIM, (h + 1) * HEAD_DIM) for h in heads]
    qh = [rq[:, hs[h]].astype(BF16) for h in heads]
    kh = [rk[:, hs[h]].astype(BF16) for h in heads]
    vh = [rv[:, hs[h]].astype(BF16) for h in heads]
    s_old = [s_ref[hs[h], :] for h in heads]
    att = [_dot_nt(qh[h], kh[h]) for h in heads]
    inter = [_dot(qh[h], s_old[h].astype(BF16)) for h in heads]
    upd = [_dot(kdt[hs[h], :], vh[h]) for h in heads]
    intra = [_dot((att[h] * dmask_ref[h]).astype(BF16), vh[h]) for h in heads]
    for h in heads:
        s_ref[hs[h], :] = s_old[h] * sdec_ref[hs[h], :] + upd[h]
    o = jnp.concatenate([intra[h] + inter[h] * qdec[:, hs[h]] for h in heads], axis=1)
    gn = _head_group_norm(o, ones_ref[...], gnw_ref[...], gnb_ref[...], GN_EPS)
    o_ref[...] = rg * jax.nn.sigmoid(rg) * gn

    @pl.when(c == pl.num_programs(1) - 1)
    def _():
        s_out_ref[0] = s_ref[...]


def _ret_prompt(p, tables, dec, gn_w, gn_b, ones_bd, n_seq, t_len):
    nchunk = t_len // RET_CHUNK
    cos, slo, shi = tables
    dmask, qdec, kdec, sdec = dec
    tab = pl.BlockSpec((RET_CHUNK, BRANCH_W), lambda s, c: (c, 0))
    full2 = lambda a, b: pl.BlockSpec((a, b), lambda s, c: (0, 0))
    return pl.pallas_call(
        _ret_prompt_kernel,
        grid=(n_seq, nchunk),
        in_specs=[pl.BlockSpec((RET_CHUNK, RET_P), lambda s, c: (s * nchunk + c, COL_RET // RET_P)),
                  tab, tab, tab,
                  pl.BlockSpec((N_HEADS, RET_CHUNK, RET_CHUNK), lambda s, c: (0, 0, 0)),
                  full2(RET_CHUNK, BRANCH_W), full2(RET_CHUNK, BRANCH_W), full2(BRANCH_W, HEAD_DIM),
                  full2(1, BRANCH_W), full2(1, BRANCH_W), full2(BRANCH_W, BRANCH_W)],
        out_specs=[pl.BlockSpec((RET_CHUNK, BRANCH_W), lambda s, c: (s * nchunk + c, 0)),
                   pl.BlockSpec((1, BRANCH_W, HEAD_DIM), lambda s, c: (s, 0, 0))],
        out_shape=[jax.ShapeDtypeStruct((n_seq * t_len, BRANCH_W), F32),
                   jax.ShapeDtypeStruct((n_seq, BRANCH_W, HEAD_DIM), F32)],
        scratch_shapes=[pltpu.VMEM((BRANCH_W, HEAD_DIM), F32)],
        compiler_params=_cparams(2),
        name="retention_prompt",
    )(p, cos, slo, shi, dmask, qdec, kdec, sdec, gn_w.reshape(1, BRANCH_W), gn_b.reshape(1, BRANCH_W),
      ones_bd)


def _ret_sample_kernel(p_ref, cos_ref, slo_ref, shi_ref, st0_ref, gamma_ref, gnw_ref, gnb_ref, ones_ref,
                       o_ref, st_out_ref, xt_ref, v_ref, y_ref):
    cos, slo, shi = cos_ref[...], slo_ref[...], shi_ref[...]
    half = HEAD_DIM // 2
    rq = _rope(p_ref[:, 0:256], cos, slo, shi, half)
    rk = _rope(p_ref[:, 256:512], cos, slo, shi, half) * RET_KSCALE
    rg = p_ref[:, 768:1024]
    xt_ref[0] = rq.T
    xt_ref[1] = rk.T
    v_ref[...] = p_ref[:, 512:768]
    gamma = gamma_ref[...]

    def group(gi, carry):
        qt, kt = _group_cols(xt_ref, 2, gi)
        base = pl.multiple_of(gi * RW_GROUP, RW_GROUP)
        vrows = v_ref[pl.ds(base, RW_GROUP), :]
        ys = []
        for i in range(RW_GROUP):
            st = st0_ref[base + i] * gamma + _col_bcast(kt, i) * vrows[i:i + 1, :]
            st_out_ref[base + i] = st
            ys.append(jnp.sum(st * _col_bcast(qt, i), axis=0, keepdims=True))
        y_ref[pl.ds(base, RW_GROUP), :] = jnp.concatenate(ys, axis=0)
        return carry

    lax.fori_loop(0, RW_CHUNK // RW_GROUP, group, 0)
    gn = _head_group_norm(y_ref[...], ones_ref[...], gnw_ref[...], gnb_ref[...], GN_EPS)
    o_ref[...] = rg * jax.nn.sigmoid(rg) * gn


def _ret_sample(p, tables, st0, gamma, gn_w, gn_b, ones_bd):
    n = st0.shape[0]
    assert n == RW_CHUNK
    cos, slo, shi = tables
    full2 = lambda a, b: pl.BlockSpec((a, b), lambda i: (0, 0))
    st_spec = pl.BlockSpec((n, HEAD_DIM, BRANCH_W), lambda i: (0, 0, 0))
    return pl.pallas_call(
        _ret_sample_kernel,
        grid=(1,),
        in_specs=[pl.BlockSpec((n, RET_P), lambda i: (0, COL_RET // RET_P)),
                  full2(n, BRANCH_W), full2(n, BRANCH_W), full2(n, BRANCH_W), st_spec,
                  full2(1, BRANCH_W), full2(1, BRANCH_W), full2(1, BRANCH_W), full2(BRANCH_W, BRANCH_W)],
        out_specs=[full2(n, BRANCH_W), st_spec],
        out_shape=[jax.ShapeDtypeStruct((n, BRANCH_W), F32), jax.ShapeDtypeStruct(st0.shape, F32)],
        scratch_shapes=[pltpu.VMEM((2, BRANCH_W, RW_CHUNK), F32),
                        pltpu.VMEM((RW_CHUNK, BRANCH_W), F32),
                        pltpu.VMEM((RW_CHUNK, BRANCH_W), F32)],
        compiler_params=_cparams(1),
        name="retention_sample",
    )(p, cos, slo, shi, st0, gamma, gn_w.reshape(1, BRANCH_W), gn_b.reshape(1, BRANCH_W), ones_bd)


POOL_HIST = 16


def _pool_windows(lane):
    win = jnp.full(lane.shape, float(POOL_WINDOWS[-1]), F32)
    for gi in range(len(POOL_WINDOWS) - 2, -1, -1):
        win = jnp.where(lane < (gi + 1) * HEAD_DIM, float(POOL_WINDOWS[gi]), win)
    return win


def _pool_select(sums, lane):
    out = sums[-1]
    for gi in range(len(POOL_WINDOWS) - 2, -1, -1):
        out = jnp.where(lane < (gi + 1) * HEAD_DIM, sums[gi], out)
    return out


def _pool_prompt_kernel(tm, u_ref, wp_ref, scale_ref, y_ref, nbuf_ref, ext_ref):
    t = pl.program_id(1)

    @pl.when(t == 0)
    def _():
        ext_ref[0:POOL_HIST, :] = jnp.zeros((POOL_HIST, BRANCH_W), F32)

    @pl.when(t > 0)
    def _():
        ext_ref[0:POOL_HIST, :] = ext_ref[tm:tm + POOL_HIST, :]

    u = u_ref[...]
    ext_ref[POOL_HIST:POOL_HIST + tm, :] = u
    sums, acc, i = [], u, 1
    for win in POOL_WINDOWS:
        while i < win:
            acc = acc + ext_ref[pl.ds(POOL_HIST - i, tm), :]
            i += 1
        sums.append(acc)
    lane = lax.broadcasted_iota(I32, u.shape, 1)
    pos = (lax.broadcasted_iota(I32, u.shape, 0) + t * tm).astype(F32)
    cnt = jnp.minimum(pos + 1.0, _pool_windows(lane))
    d = _pool_select(sums, lane) / cnt - u
    y_ref[...] = _dot(d.astype(BF16), wp_ref[...]) * scale_ref[...]

    @pl.when(t == pl.num_programs(1) - 1)
    def _():
        nbuf_ref[0] = ext_ref[tm:tm + POOL_HIST, :]


def _pool_prompt(p, wp_bd, scale, n_seq, t_len, tm):
    tpb = t_len // tm
    return pl.pallas_call(
        functools.partial(_pool_prompt_kernel, tm),
        grid=(n_seq, tpb),
        in_specs=[pl.BlockSpec((tm, BRANCH_W), lambda s, t: (s * tpb + t, COL_POOL // BRANCH_W)),
                  pl.BlockSpec((BRANCH_W, BRANCH_W), lambda s, t: (0, 0)),
                  pl.BlockSpec((1, BRANCH_W), lambda s, t: (0, 0))],
        out_specs=[pl.BlockSpec((tm, BRANCH_W), lambda s, t: (s * tpb + t, 0)),
                   pl.BlockSpec((1, POOL_HIST, BRANCH_W), lambda s, t: (s, 0, 0))],
        out_shape=[jax.ShapeDtypeStruct((n_seq * t_len, BRANCH_W), F32),
                   jax.ShapeDtypeStruct((n_seq, POOL_HIST, BRANCH_W), F32)],
        scratch_shapes=[pltpu.VMEM((tm + POOL_HIST, BRANCH_W), F32)],
        compiler_params=_cparams(2),
        name="pool_prompt",
    )(p, wp_bd, scale.reshape(1, BRANCH_W))


def _pool_sample_kernel(pos0, u_ref, buf_ref, wp_ref, scale_ref, y_ref, nbuf_ref):
    u = u_ref[...]
    sums, acc, i = [], u, 1
    for win in POOL_WINDOWS:
        while i < win:
            acc = acc + buf_ref[POOL_BUF - i]
            i += 1
        sums.append(acc)
    lane = lax.broadcasted_iota(I32, u.shape, 1)
    cnt = jnp.minimum(float(pos0) + 1.0, _pool_windows(lane))
    d = _pool_select(sums, lane) / cnt - u
    y_ref[...] = _dot(d.astype(BF16), wp_ref[...]) * scale_ref[...]
    for r in range(POOL_BUF - 1):
        nbuf_ref[r] = buf_ref[r + 1]
    nbuf_ref[POOL_BUF - 1] = u


def _pool_sample(p, buf_t, wp_bd, scale, pos0):
    n = buf_t.shape[1]
    full3 = pl.BlockSpec((POOL_BUF, n, BRANCH_W), lambda i: (0, 0, 0))
    return pl.pallas_call(
        functools.partial(_pool_sample_kernel, pos0),
        grid=(1,),
        in_specs=[pl.BlockSpec((n, BRANCH_W), lambda i: (0, COL_POOL // BRANCH_W)), full3,
                  pl.BlockSpec((BRANCH_W, BRANCH_W), lambda i: (0, 0)),
                  pl.BlockSpec((1, BRANCH_W), lambda i: (0, 0))],
        out_specs=[pl.BlockSpec((n, BRANCH_W), lambda i: (0, 0)), full3],
        out_shape=[jax.ShapeDtypeStruct((n, BRANCH_W), F32),
                   jax.ShapeDtypeStruct((POOL_BUF, n, BRANCH_W), F32)],
        compiler_params=_cparams(1),
        name="pool_sample",
    )(p, buf_t, wp_bd, scale.reshape(1, BRANCH_W))


def _page_copies(pt_ref, layer, seq, slot, hbm_ref, buf_ref, sem_ref, n_pages):
    return [pltpu.make_async_copy(hbm_ref.at[layer, pt_ref[seq, pg]], buf_ref.at[slot, pg], sem_ref.at[slot])
            for pg in range(n_pages)]


def _paged_prefetch(pt_ref, layer, pairs, n_pages):
    b = pl.program_id(0)
    nb = pl.num_programs(0)
    slot = lax.rem(b, 2)

    @pl.when(b == 0)
    def _():
        for hbm_ref, buf_ref, sem_ref in pairs:
            for cp in _page_copies(pt_ref, layer, 0, 0, hbm_ref, buf_ref, sem_ref, n_pages):
                cp.start()

    @pl.when(b + 1 < nb)
    def _():
        for hbm_ref, buf_ref, sem_ref in pairs:
            for cp in _page_copies(pt_ref, layer, b + 1, 1 - slot, hbm_ref, buf_ref, sem_ref, n_pages):
                cp.start()

    for hbm_ref, buf_ref, sem_ref in pairs:
        for cp in _page_copies(pt_ref, layer, b, slot, hbm_ref, buf_ref, sem_ref, n_pages):
            cp.wait()
    return slot


def _dsa_sample_score_kernel(layer, n_pages, pt_ref, qi_ref, wi_ref, cki_ref, out_ref, buf_ref, sem_ref):
    slot = _paged_prefetch(pt_ref, layer, [(cki_ref, buf_ref, sem_ref)], n_pages)
    qi4, wi4 = qi_ref[0], wi_ref[0]
    rows = []
    for pg in range(n_pages):
        s = jnp.maximum(_dot(qi4, buf_ref[slot, pg].astype(BF16)), 0.0)
        rows.append(jnp.sum(s * wi4, axis=0, keepdims=True))
    out_ref[0] = jnp.concatenate(rows, axis=1)


def _dsa_sample_scores(page_table, qi4, wi4, cache_kidx_t, layer):
    nb, n_pages = page_table.shape
    page = cache_kidx_t.shape[3]
    past = n_pages * page
    return pl.pallas_call(
        functools.partial(_dsa_sample_score_kernel, layer, n_pages),
        grid_spec=pltpu.PrefetchScalarGridSpec(
            num_scalar_prefetch=1,
            grid=(nb,),
            in_specs=[pl.BlockSpec((1, N_HEADS, HEAD_DIM), lambda b, pt: (b, 0, 0)),
                      pl.BlockSpec((1, N_HEADS, 1), lambda b, pt: (b, 0, 0)),
                      pl.BlockSpec(memory_space=pl.ANY)],
            out_specs=pl.BlockSpec((1, 1, past), lambda b, pt: (b, 0, 0)),
            scratch_shapes=[pltpu.VMEM((2, n_pages, HEAD_DIM, page), F32),
                            pltpu.SemaphoreType.DMA((2,))]),
        out_shape=jax.ShapeDtypeStruct((nb, 1, past), F32),
        compiler_params=_cparams(1),
        name="dsa_sample_scores",
    )(page_table, qi4, wi4, cache_kidx_t)


def _dsa_sample_select_kernel(n_sel, sc_ref, qib_ref, kib_ref, wi_ref, sel_ref, selself_ref):
    nb, past = sc_ref.shape
    qi = qib_ref[...].astype(F32)
    ki = kib_ref[...].astype(F32)[:, :HEAD_DIM]
    wi = wi_ref[...]
    s_self = jnp.zeros((nb, 1), F32)
    for h in range(N_HEADS):
        dot_h = jnp.sum(qi[:, h * HEAD_DIM:(h + 1) * HEAD_DIM] * ki, axis=-1, keepdims=True)
        s_self = s_self + jnp.maximum(dot_h, 0.0) * wi[:, h:h + 1]
    key = _float_key(sc_ref[...] + 0.0)
    key_self = _float_key(s_self)

    def count(pred):
        return (jnp.sum(pred(key).astype(I32), axis=-1, keepdims=True) + pred(key_self).astype(I32))

    def bit_step(i, tau):
        cand = tau + (jnp.int32(1) << (31 - i))
        return jnp.where(count(lambda x: x >= cand) >= n_sel, cand, tau)

    tau = lax.fori_loop(0, 32, bit_step, jnp.full((nb, 1), INT_MIN, I32))
    gt = key > tau
    tie = key == tau
    need = (n_sel - count(lambda x: x > tau)).astype(F32)
    utri = (lax.broadcasted_iota(I32, (LANES, LANES), 0)
            < lax.broadcasted_iota(I32, (LANES, LANES), 1)).astype(BF16)
    seen = jnp.zeros((nb, 1), F32)
    for c in range(past // LANES):
        cs = slice(c * LANES, (c + 1) * LANES)
        tief = jnp.where(tie[:, cs], 1.0, 0.0)
        rank = _dot(tief.astype(BF16), utri) + seen
        keep = jnp.logical_or(gt[:, cs], jnp.logical_and(tie[:, cs], rank < need))
        sel_ref[:, cs] = jnp.where(keep, 1.0, 0.0)
        seen = seen + jnp.sum(tief, axis=-1, keepdims=True)
    keep_self = jnp.logical_or(key_self > tau, jnp.logical_and(key_self == tau, seen < need))
    selself_ref[...] = jnp.where(keep_self, 1.0, 0.0)


def _dsa_sample_select(scores, qib, kib, wi, n_sel):
    nb, past = scores.shape
    full2 = lambda a, b: pl.BlockSpec((a, b), lambda i: (0, 0))
    return pl.pallas_call(
        functools.partial(_dsa_sample_select_kernel, n_sel),
        grid=(1,),
        in_specs=[full2(nb, past), full2(nb, BRANCH_W), full2(nb, LANES), full2(nb, 8)],
        out_specs=[full2(nb, past), full2(nb, 1)],
        out_shape=[jax.ShapeDtypeStruct((nb, past), F32), jax.ShapeDtypeStruct((nb, 1), F32)],
        compiler_params=_cparams(1),
        name="dsa_sample_select",
    )(scores, qib, kib, wi)


def _dsa_sample_attend_kernel(layer, n_pages, pt_ref, q_ref, sel_ref, selself_ref, kself_ref, vself_ref,
                              ck_ref, cv_ref, o_ref, kbuf_ref, vbuf_ref, ksem_ref, vsem_ref):
    slot = _paged_prefetch(pt_ref, layer, [(ck_ref, kbuf_ref, ksem_ref), (cv_ref, vbuf_ref, vsem_ref)], n_pages)
    head_of_lane = lax.broadcasted_iota(I32, (N_HEADS, BRANCH_W), 1) // HEAD_DIM
    own = head_of_lane == lax.broadcasted_iota(I32, (N_HEADS, BRANCH_W), 0)
    qrows = jnp.where(own, jnp.broadcast_to(q_ref[0], (N_HEADS, BRANCH_W)), 0.0).astype(BF16)
    lg = jnp.concatenate([_dot(qrows, kbuf_ref[slot, pg].astype(BF16)) for pg in range(n_pages)], axis=1)
    lg = jnp.where(sel_ref[0] > 0.0, lg, NEG_BIG)
    kself = kself_ref[0].astype(BF16).astype(F32)
    lself = jnp.sum(qrows.astype(F32) * kself, axis=-1, keepdims=True)
    lself = jnp.where(selself_ref[0] > 0.0, lself, NEG_BIG)
    m = jnp.maximum(jnp.max(lg, axis=-1, keepdims=True), lself)
    p = jnp.exp(lg - m)
    pself = jnp.exp(lself - m)
    denom = jnp.sum(p, axis=-1, keepdims=True) + pself
    p = p.astype(BF16)
    page = kbuf_ref.shape[3]
    o4 = pself * vself_ref[0]
    for pg in range(n_pages):
        o4 = o4 + _dot_nt(p[:, pg * page:(pg + 1) * page], vbuf_ref[slot, pg].astype(BF16))
    o_ref[0] = jnp.sum(jnp.where(own, o4 / denom, 0.0), axis=0, keepdims=True)


def _dsa_sample_attend(page_table, q, sel, selself, kself, vself, cache_k_t, cache_v_t, layer):
    nb, n_pages = page_table.shape
    page = cache_k_t.shape[3]
    past = n_pages * page
    row3 = lambda w: pl.BlockSpec((1, 1, w), lambda b, pt: (b, 0, 0))
    return pl.pallas_call(
        functools.partial(_dsa_sample_attend_kernel, layer, n_pages),
        grid_spec=pltpu.PrefetchScalarGridSpec(
            num_scalar_prefetch=1,
            grid=(nb,),
            in_specs=[row3(BRANCH_W), row3(past), row3(1), row3(BRANCH_W), row3(BRANCH_W),
                      pl.BlockSpec(memory_space=pl.ANY), pl.BlockSpec(memory_space=pl.ANY)],
            out_specs=row3(BRANCH_W),
            scratch_shapes=[pltpu.VMEM((2, n_pages, BRANCH_W, page), F32),
                            pltpu.VMEM((2, n_pages, BRANCH_W, page), F32),
                            pltpu.SemaphoreType.DMA((2,)),
                            pltpu.SemaphoreType.DMA((2,))]),
        out_shape=jax.ShapeDtypeStruct((nb, 1, BRANCH_W), F32),
        compiler_params=_cparams(1),
        name="dsa_sample_attend",
    )(page_table, q.reshape(nb, 1, BRANCH_W), sel.reshape(nb, 1, past), selself.reshape(nb, 1, 1),
      kself.reshape(nb, 1, BRANCH_W), vself.reshape(nb, 1, BRANCH_W), cache_k_t, cache_v_t)


def _rope_tables(pos, rot_dim, theta):
    half = rot_dim // 2
    inv = theta ** (-jnp.arange(half, dtype=F32) / half)
    ang = pos.astype(F32)[:, None] * inv[None, :]
    cos, sin = jnp.cos(ang), jnp.sin(ang)
    n = pos.shape[0]
    zh = jnp.zeros((n, half), F32)
    zr = jnp.zeros((n, HEAD_DIM - rot_dim), F32)
    c = jnp.concatenate([cos, cos, jnp.ones((n, HEAD_DIM - rot_dim), F32)], axis=1)
    lo = jnp.concatenate([-sin, zh, zr], axis=1)
    hi = jnp.concatenate([zh, sin, zr], axis=1)
    return tuple(jnp.tile(t, (1, N_HEADS)) for t in (c, lo, hi))


def _ret_decay(chunk):
    log_g = jnp.log1p(-jnp.exp2(-5.0 - jnp.arange(N_HEADS, dtype=F32)))
    n = jnp.arange(chunk, dtype=F32)
    diff = n[:, None] - n[None, :]
    dmask = jnp.where(diff >= 0, jnp.exp(log_g[:, None, None] * jnp.maximum(diff, 0.0)), 0.0)
    q_dec = jnp.exp(log_g[:, None] * (n + 1.0)).T
    k_dec = jnp.exp(log_g[:, None] * (chunk - 1.0 - n))
    s_dec = jnp.exp(log_g * chunk)
    qdec = jnp.repeat(q_dec, HEAD_DIM, axis=1)
    kdec = jnp.repeat(k_dec.T, HEAD_DIM, axis=1)
    sdec_rows = jnp.repeat(s_dec, HEAD_DIM)
    return dmask, qdec, kdec, sdec_rows


def _block_diag(blocks):
    n = len(blocks)
    rows = []
    for i, blk in enumerate(blocks):
        rows.append(jnp.concatenate([blk if j == i else jnp.zeros((blk.shape[0], blocks[j].shape[1]), blk.dtype)
                                     for j in range(n)], axis=1))
    return jnp.concatenate(rows, axis=0)


def kernel(x_prompt, x_sample, cache_k, cache_v, cache_kidx, state_wkv, state_shift, state_ret, state_pool, page_table, norm1_w, w_in, rwkv_mu, rwkv_w0, rwkv_w_w2, rwkv_a0, rwkv_w_a2, rwkv_w_g2, rwkv_k_k, rwkv_k_a, rwkv_r_k, rwkv_ln_w, rwkv_ln_b, q_norm_w, k_norm_w, idx_k_norm_w, ret_gn_w, ret_gn_b, pool_w, pool_scale, w_branch, w_out, norm2_w, w_ffn_in, w_ffn_out):
    n_seq, t_len, _ = x_prompt.shape
    nb = x_sample.shape[0]
    assert x_sample.shape[1] == 1
    page = cache_k.shape[2]
    past = page_table.shape[1] * page
    n_sel_sample = min(TOPK_MAX, (past + 1) // 4)
    tm_p = min(512, t_len)
    tm_m = min(256, t_len)

    ones_bd = _block_diag([jnp.ones((HEAD_DIM, HEAD_DIM), BF16)] * N_HEADS)
    pos_p = jnp.arange(t_len)
    pos_s = jnp.full((nb,), past)
    att_tab_p, att_tab_s = _rope_tables(pos_p, ROT_DIM, ROPE_THETA), _rope_tables(pos_s, ROT_DIM, ROPE_THETA)
    ret_tab_p, ret_tab_s = _rope_tables(pos_p, HEAD_DIM, RET_THETA), _rope_tables(pos_s, HEAD_DIM, RET_THETA)
    dmask, qdec, kdec, sdec_rows = _ret_decay(math.gcd(t_len, RET_CHUNK))
    dec_p = (dmask, qdec, kdec, jnp.broadcast_to(sdec_rows[:, None], (BRANCH_W, HEAD_DIM)))
    gamma_s = _ret_decay(1)[3].reshape(1, BRANCH_W)

    n_pool = cache_k.shape[1]
    cki_t = cache_kidx.transpose(0, 1, 3, 2)
    ck_t = cache_k.transpose(0, 1, 3, 4, 2).reshape(-1, n_pool, BRANCH_W, page)
    cv_t = cache_v.transpose(0, 1, 3, 4, 2).reshape(-1, n_pool, BRANCH_W, page)

    xp = x_prompt.reshape(n_seq * t_len, D_MODEL)
    xs = x_sample.reshape(nb, D_MODEL)
    prompt_new = [[] for _ in range(7)]
    sample_new = [[] for _ in range(7)]
    o_att, o_ret, o_pool = RWKV_P, RWKV_P + ATT_P, RWKV_P + ATT_P + RET_P
    o_gate = o_pool + BRANCH_W
    row1 = lambda a: a.reshape(1, -1)
    w_in_t = w_in.transpose(2, 0, 1)

    for l in range(DEPTH):
        w = w_in_t[:, l, :]
        w_pad = jnp.concatenate(
            [w[o_gate:], w[:RWKV_P], w[o_ret:o_pool], w[o_pool:o_gate], w[o_att:o_ret],
             jnp.zeros((ATT_P_PAD - ATT_P, D_MODEL), F32)], axis=0).astype(BF16)
        lw = dict(mu=row1(rwkv_mu[l]), w0=row1(rwkv_w0[l]),
                  w_wa=_block_diag([rwkv_w_w2[l], rwkv_w_a2[l]]).astype(BF16),
                  a0=row1(rwkv_a0[l]), w_g2=rwkv_w_g2[l].astype(BF16), k_k=row1(rwkv_k_k[l]),
                  k_a=row1(rwkv_k_a[l]), r_k=row1(rwkv_r_k[l]), ln_w=row1(rwkv_ln_w[l]), ln_b=row1(rwkv_ln_b[l]))
        wp_bd = _block_diag([pool_w[l, g] for g in range(N_HEADS)]).astype(BF16)
        wb = w_branch[l].astype(BF16)
        wo = w_out[l].astype(BF16)
        wfi = w_ffn_in[l].astype(BF16)
        wfo = w_ffn_out[l].astype(BF16)

        p = _norm_matmul(xp, norm1_w[l], w_pad, min(1024, t_len), P_PAD // 5)
        o_a, wkv = _rwkv_prompt(p, lw, ones_bd, n_seq, t_len)
        q, k, v, ki, kb, vt, qib, kib, wit = _att_prep(p, att_tab_p, q_norm_w[l], k_norm_w[l], idx_k_norm_w[l],
                                                     ones_bd, n_seq, t_len, tm_p)
        o_b = _dsa_prompt(q, qib, wit, kb, vt, kib, n_seq, t_len)
        o_c, ret = _ret_prompt(p, ret_tab_p, dec_p, ret_gn_w[l], ret_gn_b[l], ones_bd, n_seq, t_len)
        o_d, pbuf = _pool_prompt(p, wp_bd, pool_scale[l], n_seq, t_len, tm_p)
        x1 = _merge(xp, (o_a, o_b, o_c, o_d), p, wb, wo, tm_m)
        xp = _ffn(x1, norm2_w[l], wfi, wfo, tm_m)
        st = (k.reshape(n_seq, t_len, N_HEADS, HEAD_DIM), v.reshape(n_seq, t_len, N_HEADS, HEAD_DIM),
              ki[:, :HEAD_DIM].reshape(n_seq, t_len, HEAD_DIM),
              wkv.reshape(n_seq, HEAD_DIM, N_HEADS, HEAD_DIM).transpose(0, 2, 3, 1),
              p.reshape(n_seq, t_len, P_PAD)[:, -1, COL_RWKV:COL_RWKV + RWKV_P],
              ret.reshape(n_seq, N_HEADS, HEAD_DIM, HEAD_DIM),
              pbuf[:, POOL_HIST - POOL_BUF:, :])
        for lst, s in zip(prompt_new, st):
            lst.append(s)

        ps = _norm_matmul(xs, norm1_w[l], w_pad, nb, P_PAD // 5)
        wkv0 = state_wkv[l].transpose(0, 3, 1, 2).reshape(nb, HEAD_DIM, BRANCH_W)
        o_a, wkv = _rwkv_sample(ps, state_shift[l], wkv0, lw, ones_bd)
        q, k, v, ki, kb, vt, qib, kib, wit = _att_prep(ps, att_tab_s, q_norm_w[l], k_norm_w[l], idx_k_norm_w[l],
                                                     ones_bd, 1, nb, nb)
        wi_rows = wit[0].T
        scores = _dsa_sample_scores(page_table, qib.reshape(nb, N_HEADS, HEAD_DIM),
                                    wi_rows[:, :N_HEADS].reshape(nb, N_HEADS, 1), cki_t, l)
        sel, selself = _dsa_sample_select(scores.reshape(nb, past), qib, kib, wi_rows, n_sel_sample)
        o_b = _dsa_sample_attend(page_table, q, sel, selself, k, v, ck_t, cv_t, l).reshape(nb, BRANCH_W)
        ret0 = state_ret[l].transpose(0, 2, 1, 3).reshape(nb, HEAD_DIM, BRANCH_W)
        o_c, ret = _ret_sample(ps, ret_tab_s, ret0, gamma_s, ret_gn_w[l], ret_gn_b[l], ones_bd)
        o_d, pbuf = _pool_sample(ps, state_pool[l].transpose(1, 0, 2), wp_bd, pool_scale[l], past)
        x1 = _merge(xs, (o_a, o_b, o_c, o_d), ps, wb, wo, nb)
        xs = _ffn(x1, norm2_w[l], wfi, wfo, nb)
        st = (k.reshape(nb, 1, N_HEADS, HEAD_DIM), v.reshape(nb, 1, N_HEADS, HEAD_DIM),
              ki[:, :HEAD_DIM].reshape(nb, 1, HEAD_DIM),
              wkv.reshape(nb, HEAD_DIM, N_HEADS, HEAD_DIM).transpose(0, 2, 3, 1),
              ps[:, COL_RWKV:COL_RWKV + RWKV_P],
              ret.reshape(nb, HEAD_DIM, N_HEADS, HEAD_DIM).transpose(0, 2, 1, 3),
              pbuf.transpose(1, 0, 2))
        for lst, s in zip(sample_new, st):
            lst.append(s)

    outs_p = [jnp.stack(a) for a in prompt_new]
    outs_s = [jnp.stack(a) for a in sample_new]
    return (xp.reshape(n_seq, t_len, D_MODEL), xs.reshape(nb, 1, D_MODEL), *outs_p, *outs_s)
```

```python
import functools
import math

import jax
import jax.numpy as jnp
from jax import lax
from jax.experimental import pallas as pl
from jax.experimental.pallas import tpu as pltpu

F32 = jnp.float32
BF16 = jnp.bfloat16
I32 = jnp.int32

D_MODEL = 1024
DEPTH = 4
HEAD_DIM = 64
N_HEADS = 4
BRANCH_W = N_HEADS * HEAD_DIM
N_BRANCH = 4
RWKV_DECAY_LORA = 64
RWKV_A_LORA = 64
RWKV_G_LORA = 128
RWKV_P = 3 * BRANCH_W + RWKV_DECAY_LORA + RWKV_A_LORA + RWKV_G_LORA
RWKV_LN_EPS = 64e-5
ATT_P = 3 * BRANCH_W + N_HEADS * HEAD_DIM + HEAD_DIM + N_HEADS
ATT_P_PAD = 1280
TOPK_MAX = 256
ROPE_THETA = 500000.0
ROT_DIM = HEAD_DIM // 4
ATT_SCALE = HEAD_DIM ** -0.5
IDX_SCALE = (N_HEADS * HEAD_DIM) ** -0.5
RET_P = 4 * BRANCH_W
RET_CHUNK = 128
RET_THETA = 10000.0
RET_KSCALE = HEAD_DIM ** -0.5
POOL_WINDOWS = (2, 4, 8, 16)
POOL_BUF = 15
GATE_P = N_BRANCH * D_MODEL
D_FF = -(-8 * D_MODEL // 768) * 256
NORM_EPS = 1e-6
GN_EPS = 1e-5

COL_GATE = 0
COL_RWKV = GATE_P
COL_RET = COL_RWKV + RWKV_P
COL_POOL = COL_RET + RET_P
COL_ATT = COL_POOL + BRANCH_W
P_PAD = COL_ATT + ATT_P_PAD

LANES = 128
INT_MIN = -2 ** 31
NEG_BIG = -1e30
LOG2_E = 1.4426950408889634
VMEM_LIMIT = 56 * 1024 * 1024


def _cparams(n_axes):
    return pltpu.CompilerParams(dimension_semantics=("arbitrary",) * n_axes,
                                vmem_limit_bytes=VMEM_LIMIT)


def _dot(a, b):
    return jnp.dot(a, b, preferred_element_type=F32)


def _dot_nt(a, b):
    return lax.dot_general(a, b, (((1,), (1,)), ((), ())), preferred_element_type=F32)


def _seg_sum(x, ones_bd):
    hi = x.astype(BF16)
    lo = (x - hi.astype(F32)).astype(BF16)
    return _dot(hi, ones_bd) + _dot(lo, ones_bd)


def _head_group_norm(x, ones_bd, w, b, eps):
    mean = _seg_sum(x, ones_bd) * (1.0 / HEAD_DIM)
    xc = x - mean
    var = _seg_sum(xc * xc, ones_bd) * (1.0 / HEAD_DIM)
    return xc * lax.rsqrt(var + eps) * w + b


def _rope(x, cos, sin_lo, sin_hi, half):
    n = x.shape[-1]
    return x * cos + pltpu.roll(x, n - half, 1) * sin_lo + pltpu.roll(x, half, 1) * sin_hi


def _norm_matmul_kernel(x_ref, nw_ref, w_ref, o_ref, h_ref):
    @pl.when(pl.program_id(1) == 0)
    def _():
        x = x_ref[...]
        y = x * lax.rsqrt(jnp.mean(x * x, axis=-1, keepdims=True) + NORM_EPS)
        h_ref[...] = (y * nw_ref[...]).astype(BF16)

    o_ref[...] = _dot_nt(h_ref[...], w_ref[...])


def _norm_matmul(x, norm_w, w_t_bf16, tm, tn):
    n, d = x.shape
    p = w_t_bf16.shape[0]
    return pl.pallas_call(
        _norm_matmul_kernel,
        grid=(n // tm, p // tn),
        in_specs=[pl.BlockSpec((tm, d), lambda i, j: (i, 0)),
                  pl.BlockSpec((1, d), lambda i, j: (0, 0)),
                  pl.BlockSpec((tn, d), lambda i, j: (j, 0))],
        out_specs=pl.BlockSpec((tm, tn), lambda i, j: (i, j)),
        out_shape=jax.ShapeDtypeStruct((n, p), F32),
        scratch_shapes=[pltpu.VMEM((tm, d), BF16)],
        compiler_params=_cparams(2),
        name="norm_in_proj",
    )(x, norm_w.reshape(1, d), w_t_bf16)


def _merge_kernel(x_ref, oa_ref, ob_ref, oc_ref, od_ref, gate_ref, wb_ref, wo_ref, out_ref):
    acc = None
    for b, o_ref in enumerate((oa_ref, ob_ref, oc_ref, od_ref)):
        up = _dot(o_ref[...].astype(BF16), wb_ref[b])
        g = jax.nn.sigmoid(gate_ref[:, b * D_MODEL:(b + 1) * D_MODEL])
        acc = g * up if acc is None else acc + g * up
    out_ref[...] = x_ref[...] + _dot(acc.astype(BF16), wo_ref[...])


def _merge(x, branches, p, w_branch_bf16, w_out_bf16, tm):
    n = x.shape[0]
    row = lambda w: pl.BlockSpec((tm, w), lambda i: (i, 0))
    return pl.pallas_call(
        _merge_kernel,
        grid=(n // tm,),
        in_specs=[row(D_MODEL), row(BRANCH_W), row(BRANCH_W), row(BRANCH_W), row(BRANCH_W),
                  pl.BlockSpec((tm, GATE_P), lambda i: (i, COL_GATE // GATE_P)),
                  pl.BlockSpec((N_BRANCH, BRANCH_W, D_MODEL), lambda i: (0, 0, 0)),
                  pl.BlockSpec((D_MODEL, D_MODEL), lambda i: (0, 0))],
        out_specs=row(D_MODEL),
        out_shape=jax.ShapeDtypeStruct((n, D_MODEL), F32),
        compiler_params=_cparams(1),
        name="gated_merge_out_proj",
    )(x, *branches, p, w_branch_bf16, w_out_bf16)


FF_CHUNK = D_FF // 2


def _ffn_kernel(x_ref, nw_ref, wi_ref, wo_ref, out_ref):
    x = x_ref[...]
    h = (x * lax.rsqrt(jnp.mean(x * x, axis=-1, keepdims=True) + NORM_EPS) * nw_ref[...]).astype(BF16)
    acc = x
    for c in range(D_FF // FF_CHUNK):
        g = _dot(h, wi_ref[:, c * FF_CHUNK:(c + 1) * FF_CHUNK])
        u = _dot(h, wi_ref[:, D_FF + c * FF_CHUNK:D_FF + (c + 1) * FF_CHUNK])
        a = (g * jax.nn.sigmoid(g) * u).astype(BF16)
        acc = acc + _dot(a, wo_ref[c * FF_CHUNK:(c + 1) * FF_CHUNK, :])
    out_ref[...] = acc


def _ffn(x, norm_w, w_in_bf16, w_out_bf16, tm):
    n = x.shape[0]
    return pl.pallas_call(
        _ffn_kernel,
        grid=(n // tm,),
        in_specs=[pl.BlockSpec((tm, D_MODEL), lambda i: (i, 0)),
                  pl.BlockSpec((1, D_MODEL), lambda i: (0, 0)),
                  pl.BlockSpec((D_MODEL, 2 * D_FF), lambda i: (0, 0)),
                  pl.BlockSpec((D_FF, D_MODEL), lambda i: (0, 0))],
        out_specs=pl.BlockSpec((tm, D_MODEL), lambda i: (i, 0)),
        out_shape=jax.ShapeDtypeStruct((n, D_MODEL), F32),
        compiler_params=_cparams(1),
        name="swiglu_ffn",
    )(x, norm_w.reshape(1, D_MODEL), w_in_bf16, w_out_bf16)


def _att_prep_kernel(p_ref, cos_ref, slo_ref, shi_ref, qnw_ref, knw_ref, inw_ref, ones_ref,
                     q_ref, k_ref, v_ref, ki_ref, kb_ref, vt_ref, qib_ref, kib_ref, wit_ref):
    cos, slo, shi = cos_ref[...], slo_ref[...], shi_ref[...]
    ones_bd = ones_ref[...]
    half = ROT_DIM // 2

    def head_rms(x, w):
        ms = _seg_sum(x * x, ones_bd) * (1.0 / HEAD_DIM)
        return x * lax.rsqrt(ms + NORM_EPS) * w

    q = _rope(head_rms(p_ref[:, 0:256], qnw_ref[...]), cos, slo, shi, half)
    k = _rope(head_rms(p_ref[:, 256:512], knw_ref[...]), cos, slo, shi, half)
    v = p_ref[:, 512:768]
    qi = _rope(p_ref[:, 768:1024], cos, slo, shi, half)
    tail = p_ref[:, 1024:1152]
    lane = lax.broadcasted_iota(I32, tail.shape, 1)
    is_ki = lane < HEAD_DIM
    kiraw = jnp.where(is_ki, tail, 0.0)
    ms = jnp.sum(kiraw * kiraw, axis=-1, keepdims=True) * (1.0 / HEAD_DIM)
    kin = kiraw * lax.rsqrt(ms + NORM_EPS) * inw_ref[...]
    ki = _rope(kin, cos[:, :LANES], slo[:, :LANES], shi[:, :LANES], half)
    ki = jnp.where(is_ki, ki, 0.0)

    q_ref[...] = q * ATT_SCALE
    k_ref[...] = k
    v_ref[...] = v
    ki_ref[...] = ki
    kb_ref[...] = k.astype(BF16)
    vt_ref[0] = v.T.astype(BF16)
    qib_ref[...] = qi.astype(BF16)
    kib_ref[...] = ki.astype(BF16)
    wi = jnp.where(is_ki, 0.0, tail) * IDX_SCALE
    wit_ref[0] = pltpu.roll(wi, LANES - HEAD_DIM, 1).T[0:8, :]


def _att_prep(p, tables, q_norm_w, k_norm_w, idx_k_norm_w, ones_bd, n_seq, t_len, tm):
    n = p.shape[0]
    tpb = t_len // tm
    cos, slo, shi = tables
    tile4 = lambda w: jnp.tile(w.reshape(1, HEAD_DIM), (1, N_HEADS))
    inw = jnp.concatenate([idx_k_norm_w.reshape(1, HEAD_DIM), jnp.ones((1, LANES - HEAD_DIM), F32)], axis=1)
    row = lambda w: pl.BlockSpec((tm, w), lambda i: (i, 0))
    tab = pl.BlockSpec((tm, BRANCH_W), lambda i: (i % tpb, 0))
    full = lambda a, b: pl.BlockSpec((a, b), lambda i: (0, 0))
    return pl.pallas_call(
        _att_prep_kernel,
        grid=(n // tm,),
        in_specs=[pl.BlockSpec((tm, ATT_P_PAD), lambda i: (i, COL_ATT // ATT_P_PAD)),
                  tab, tab, tab, full(1, BRANCH_W), full(1, BRANCH_W), full(1, LANES),
                  full(BRANCH_W, BRANCH_W)],
        out_specs=[row(BRANCH_W), row(BRANCH_W), row(BRANCH_W), row(LANES), row(BRANCH_W),
                   pl.BlockSpec((1, BRANCH_W, tm), lambda i: (i // tpb, 0, i % tpb)),
                   row(BRANCH_W), row(LANES),
                   pl.BlockSpec((1, 8, tm), lambda i: (i // tpb, 0, i % tpb))],
        out_shape=[jax.ShapeDtypeStruct((n, BRANCH_W), F32),
                   jax.ShapeDtypeStruct((n, BRANCH_W), F32),
                   jax.ShapeDtypeStruct((n, BRANCH_W), F32),
                   jax.ShapeDtypeStruct((n, LANES), F32),
                   jax.ShapeDtypeStruct((n, BRANCH_W), BF16),
                   jax.ShapeDtypeStruct((n_seq, BRANCH_W, t_len), BF16),
                   jax.ShapeDtypeStruct((n, BRANCH_W), BF16),
                   jax.ShapeDtypeStruct((n, LANES), BF16),
                   jax.ShapeDtypeStruct((n_seq, 8, t_len), F32)],
        compiler_params=_cparams(1),
        name="dsa_prep",
    )(p, cos, slo, shi, tile4(q_norm_w), tile4(k_norm_w), inw, ones_bd)


QB = 128


def _float_key(x):
    b = lax.bitcast_convert_type(x, I32)
    b = jnp.where(b == INT_MIN, 0, b)
    return jnp.where(b < 0, b ^ jnp.int32(0x7FFFFFFF), b)


KT = 512
KT_MM = 1024
PROBES_PER_CHECK = 4
MAX_PROBES = 48
NO_COUNT = 2 ** 30


def _dsa_prompt_kernel(n_sel, q_ref, qib_ref, wit_ref, kb_ref, vt_ref, kib_ref, o_ref,
                       key_ref, qbd_ref, acc_ref, m_ref, l_ref):
    j = pl.program_id(1)
    nkt = (j + KT // QB) // (KT // QB)
    nkt_mm = (j + KT_MM // QB) // (KT_MM // QB)
    q0 = j * QB

    def run_tiles(body, init):
        return lax.fori_loop(0, nkt_mm, lambda kt, c: body(KT_MM, pl.multiple_of(kt * KT_MM, KT_MM), c), init)

    @pl.when(jnp.logical_and(pl.program_id(0) == 0, j == 0))
    def _():
        qbd_ref[...] = jnp.zeros_like(qbd_ref)

    qib = qib_ref[...]
    qis = jnp.concatenate([qib[:, h * HEAD_DIM:(h + 1) * HEAD_DIM] for h in range(N_HEADS)], axis=0)
    qt = (q_ref[...] * LOG2_E).T.astype(BF16)
    for h in range(N_HEADS):
        qbd_ref[h * HEAD_DIM:(h + 1) * HEAD_DIM, h * QB:(h + 1) * QB] = qt[h * HEAD_DIM:(h + 1) * HEAD_DIM, :]
    wit = wit_ref[0]
    srow = lax.broadcasted_iota(I32, (QB, QB), 0)
    tcol = lax.broadcasted_iota(I32, (QB, QB), 1) + q0
    tq = lax.broadcasted_iota(I32, (1, QB), 1) + q0

    def score_tile(tile, first, carry):
        subs = range(tile // QB)
        k0 = [pl.multiple_of(first + u * QB, QB) for u in subs]
        s = [_dot_nt(kib_ref[0, pl.ds(k0[u], QB), :][:, :HEAD_DIM], qis) for u in subs]
        for u in subs:
            acc = jnp.zeros((QB, QB), F32)
            for h in range(N_HEADS):
                acc = acc + jnp.maximum(s[u][:, h * QB:(h + 1) * QB], 0.0) * wit[h:h + 1, :]
            key_ref[pl.ds(k0[u], QB), :] = jnp.where(srow + k0[u] <= tcol, _float_key(acc), jnp.int32(INT_MIN))
        return carry

    run_tiles(score_tile, 0)

    def count(pred):
        def body(kt, acc):
            k0 = pl.multiple_of(kt * KT, KT)
            m = pred(key_ref[pl.ds(k0, KT), :]).astype(I32)
            return acc + jnp.sum(m.reshape(KT // 8, 8, QB), axis=0)
        acc = lax.fori_loop(0, nkt, body, jnp.zeros((8, QB), I32))
        return jnp.sum(acc, axis=0, keepdims=True)

    def colmax_body(kt, acc):
        k0 = pl.multiple_of(kt * KT, KT)
        return jnp.maximum(acc, jnp.max(key_ref[pl.ds(k0, KT), :].reshape(KT // 8, 8, QB), axis=0))

    kmax = jnp.max(lax.fori_loop(0, nkt, colmax_body, jnp.full((8, QB), INT_MIN, I32)), axis=0, keepdims=True)
    kmax = jnp.minimum(kmax, jnp.int32(2 ** 31 - 2))
    hi0 = kmax + 1

    def unfinished(st):
        return jnp.logical_and(st[0] < MAX_PROBES, jnp.logical_not(st[6]))

    def probe_step(st):
        i, lo, hi, c_lo, c_hi, done, _ = st
        for _ in range(PROBES_PER_CHECK):
            e = jnp.maximum(i, 2) + 21
            step = jnp.int32(1) << jnp.minimum(e, 30)
            base = jnp.where(hi <= 1, hi, hi0)
            down = jnp.where(jnp.logical_and(e <= 30, base >= INT_MIN + step), base - step, jnp.int32(INT_MIN))
            mid = (lo >> 1) + (hi >> 1) + ((lo | hi) & 1)
            x = jnp.where(c_lo == NO_COUNT, jnp.where(hi == 1, 0, down), mid)
            x = jnp.where(i == 0, kmax, jnp.where(i == 1, 1, x))
            c = count(lambda blk: blk >= x)
            live = done == 0
            up = jnp.logical_and(jnp.logical_and(live, c >= n_sel), i != 1)
            dn = jnp.logical_and(jnp.logical_and(live, c < n_sel), x < hi)
            lo = jnp.where(up, x, lo)
            c_lo = jnp.where(up, c, c_lo)
            hi = jnp.where(dn, x, hi)
            c_hi = jnp.where(dn, c, c_hi)
            done = jnp.where(jnp.logical_or(c_lo == n_sel, hi - 1 == lo), 1, done)
            i = i + 1
        return i, lo, hi, c_lo, c_hi, done, jnp.min(done) > 0

    few = tq + 1 <= n_sel
    done0 = jnp.where(few, 1, 0)
    init = (jnp.int32(0), jnp.full((1, QB), INT_MIN, I32), hi0, jnp.where(few, n_sel, jnp.int32(NO_COUNT)),
            jnp.zeros((1, QB), I32), done0, jnp.min(done0) > 0)
    _, tau, _, c_lo, c_hi, _, _ = lax.while_loop(unfinished, probe_step, init)

    tied = c_lo != n_sel

    @pl.when(jnp.max(jnp.where(tied, 1, 0)) > 0)
    def _():
        need = jnp.where(tied, n_sel - c_hi, jnp.int32(NO_COUNT)).astype(F32)
        ltri = (lax.broadcasted_iota(I32, (QB, QB), 1) < srow).astype(BF16)

        def demote_tile(tile, first, seen):
            subs = range(tile // QB)
            k0 = [pl.multiple_of(first + u * QB, QB) for u in subs]
            blk = [key_ref[pl.ds(k0[u], QB), :] for u in subs]
            tief = [jnp.where(blk[u] == tau, 1.0, 0.0) for u in subs]
            before = [_dot(ltri, tief[u].astype(BF16)) for u in subs]
            for u in subs:
                drop = jnp.logical_and(blk[u] == tau, before[u] + seen >= need)
                key_ref[pl.ds(k0[u], QB), :] = jnp.where(drop, jnp.int32(INT_MIN), blk[u])
                seen = seen + jnp.sum(tief[u], axis=0, keepdims=True)
            return seen

        run_tiles(demote_tile, jnp.zeros((1, QB), F32))

    thr = jnp.maximum(tau, INT_MIN + 1)
    acc_ref[...] = jnp.zeros_like(acc_ref)
    m_ref[...] = jnp.full_like(m_ref, NEG_BIG)
    l_ref[...] = jnp.zeros_like(l_ref)

    def attend(tile, k0, carry):
        sel = key_ref[pl.ds(k0, tile), :] >= thr
        lg = _dot(kb_ref[0, pl.ds(k0, tile), :], qbd_ref[...])
        heads = range(N_HEADS)
        hs = [slice(h * HEAD_DIM, (h + 1) * HEAD_DIM) for h in heads]
        lh = [jnp.where(sel, lg[:, h * QB:(h + 1) * QB], NEG_BIG) for h in heads]
        m_old = [m_ref[h:h + 1, :] for h in heads]
        m_new = [jnp.maximum(m_old[h], jnp.max(lh[h], axis=0, keepdims=True)) for h in heads]
        p = [jnp.exp2(lh[h] - m_new[h]) for h in heads]
        pv = [_dot(vt_ref[0, hs[h], pl.ds(k0, tile)], p[h].astype(BF16)) for h in heads]
        alpha = [jnp.exp2(m_old[h] - m_new[h]) for h in heads]
        for h in heads:
            l_ref[h:h + 1, :] = alpha[h] * l_ref[h:h + 1, :] + jnp.sum(p[h], axis=0, keepdims=True)
            m_ref[h:h + 1, :] = m_new[h]
            acc_ref[hs[h], :] = acc_ref[hs[h], :] * alpha[h] + pv[h]
        return carry

    run_tiles(attend, 0)

    outs = [acc_ref[h * HEAD_DIM:(h + 1) * HEAD_DIM, :] / l_ref[h:h + 1, :] for h in range(N_HEADS)]
    o_ref[...] = jnp.concatenate(outs, axis=0).T


def _dsa_prompt(q, qib, wit, kb, vt, kib, n_seq, t_len):
    assert t_len % KT_MM == 0 and KT_MM % KT == 0
    n_sel = min(TOPK_MAX, t_len // 4)
    nblk = t_len // QB
    return pl.pallas_call(
        functools.partial(_dsa_prompt_kernel, n_sel),
        grid=(n_seq, nblk),
        in_specs=[pl.BlockSpec((QB, BRANCH_W), lambda b, j: (b * nblk + j, 0)),
                  pl.BlockSpec((QB, BRANCH_W), lambda b, j: (b * nblk + j, 0)),
                  pl.BlockSpec((1, 8, QB), lambda b, j: (b, 0, j)),
                  pl.BlockSpec((1, t_len, BRANCH_W), lambda b, j: (b, 0, 0)),
                  pl.BlockSpec((1, BRANCH_W, t_len), lambda b, j: (b, 0, 0)),
                  pl.BlockSpec((1, t_len, LANES), lambda b, j: (b, 0, 0))],
        out_specs=pl.BlockSpec((QB, BRANCH_W), lambda b, j: (b * nblk + j, 0)),
        out_shape=jax.ShapeDtypeStruct((n_seq * t_len, BRANCH_W), F32),
        scratch_shapes=[pltpu.VMEM((t_len, QB), I32),
                        pltpu.VMEM((BRANCH_W, N_HEADS * QB), BF16),
                        pltpu.VMEM((BRANCH_W, QB), F32),
                        pltpu.VMEM((8, QB), F32),
                        pltpu.VMEM((8, QB), F32)],
        compiler_params=_cparams(2),
        name="dsa_prompt",
    )(q, qib, wit, kb.reshape(n_seq, t_len, BRANCH_W), vt, kib.reshape(n_seq, t_len, LANES))


RW_CHUNK = 128
RW_GROUP = 8


def _col_bcast(xt, i):
    lane = lax.broadcasted_iota(I32, (HEAD_DIM, LANES), 1)
    halves = []
    for pair in range(2):
        r0 = 2 * pair * HEAD_DIM
        c0 = jnp.broadcast_to(xt[r0:r0 + HEAD_DIM, i:i + 1], (HEAD_DIM, LANES))
        c1 = jnp.broadcast_to(xt[r0 + HEAD_DIM:r0 + 2 * HEAD_DIM, i:i + 1], (HEAD_DIM, LANES))
        halves.append(jnp.where(lane < HEAD_DIM, c0, c1))
    return jnp.concatenate(halves, axis=1)


def _group_cols(xt_ref, n_arr, gi):
    sh = lax.rem(LANES - gi * RW_GROUP, LANES)
    return [pltpu.roll(xt_ref[a], sh, 1) for a in range(n_arr)]


def _rwkv_prep(p, prev, mu, w0, w_wa, a0, w_g2, k_k, k_a, r_k, ones_bd):
    xl = p + (prev - p) * mu
    r, k, v = xl[:, 0:256], xl[:, 256:512], xl[:, 512:768]
    wa = xl[:, 768:896]
    lane = lax.broadcasted_iota(I32, wa.shape, 1)
    wa = jnp.where(lane < RWKV_DECAY_LORA, jnp.tanh(wa), wa)
    z = _dot(wa.astype(BF16), w_wa)
    nz = -(w0 + z[:, 0:256])
    w_log = -(jnp.maximum(nz, 0.0) + jnp.log1p(jnp.exp(-jnp.abs(nz)))) - 0.5
    log_decay = -jnp.exp(w_log)
    a = jax.nn.sigmoid(a0 + z[:, 256:512])
    g = _dot(jax.nn.sigmoid(xl[:, 896:1024]).astype(BF16), w_g2)
    kk = k * k_k
    kk = kk / jnp.sqrt(_seg_sum(kk * kk, ones_bd) + 1e-12)
    kp = k * (1.0 + (a - 1.0) * k_a)
    bonus = _seg_sum(r * kp * r_k, ones_bd) * v
    return r, log_decay, kp, v, kk, kk * a, g, bonus


def _rwkv_sample_kernel(p_ref, prev_ref, st0_ref, mu_ref, w0_ref, wwa_ref, a0_ref, wg2_ref,
                        kk_ref, ka_ref, rk_ref, lnw_ref, lnb_ref, ones_ref,
                        o_ref, st_out_ref, xt_ref, v_ref, y_ref):
    ones_bd = ones_ref[...]
    r, log_decay, kp, v, kk, kka, g, bonus = _rwkv_prep(
        p_ref[...], prev_ref[...], mu_ref[...], w0_ref[...], wwa_ref[...], a0_ref[...], wg2_ref[...],
        kk_ref[...], ka_ref[...], rk_ref[...], ones_bd)
    for a, arr in enumerate((r, jnp.exp(log_decay), kp, kk, kka)):
        xt_ref[a] = arr.T
    v_ref[...] = v

    def group(gi, carry):
        rt, wt, kt, kkt, kkat = _group_cols(xt_ref, 5, gi)
        base = pl.multiple_of(gi * RW_GROUP, RW_GROUP)
        vrows = v_ref[pl.ds(base, RW_GROUP), :]
        ys = []
        for i in range(RW_GROUP):
            st = st0_ref[base + i]
            sa = jnp.sum(st * _col_bcast(kkt, i), axis=0, keepdims=True)
            st = st * _col_bcast(wt, i) - _col_bcast(kkat, i) * sa + _col_bcast(kt, i) * vrows[i:i + 1, :]
            ys.append(jnp.sum(st * _col_bcast(rt, i), axis=0, keepdims=True))
            st_out_ref[base + i] = st
        y_ref[pl.ds(base, RW_GROUP), :] = jnp.concatenate(ys, axis=0)
        return carry

    lax.fori_loop(0, RW_CHUNK // RW_GROUP, group, 0)
    o = _head_group_norm(y_ref[...], ones_bd, lnw_ref[...], lnb_ref[...], RWKV_LN_EPS)
    o_ref[...] = (o + bonus) * g


def _rwkv_sample(p, prev, st0, lw, ones_bd):
    n = st0.shape[0]
    assert n == RW_CHUNK
    full2 = lambda a, b: pl.BlockSpec((a, b), lambda i: (0, 0))
    st_spec = pl.BlockSpec((n, HEAD_DIM, BRANCH_W), lambda i: (0, 0, 0))
    return pl.pallas_call(
        _rwkv_sample_kernel,
        grid=(1,),
        in_specs=[pl.BlockSpec((n, RWKV_P), lambda i: (0, COL_RWKV // RWKV_P)),
                  full2(n, RWKV_P), st_spec,
                  full2(1, RWKV_P), full2(1, BRANCH_W), full2(LANES, 2 * BRANCH_W), full2(1, BRANCH_W),
                  full2(RWKV_G_LORA, BRANCH_W), full2(1, BRANCH_W), full2(1, BRANCH_W), full2(1, BRANCH_W),
                  full2(1, BRANCH_W), full2(1, BRANCH_W), full2(BRANCH_W, BRANCH_W)],
        out_specs=[full2(n, BRANCH_W), st_spec],
        out_shape=[jax.ShapeDtypeStruct((n, BRANCH_W), F32),
                   jax.ShapeDtypeStruct(st0.shape, F32)],
        scratch_shapes=[pltpu.VMEM((5, BRANCH_W, RW_CHUNK), F32),
                        pltpu.VMEM((RW_CHUNK, BRANCH_W), F32),
                        pltpu.VMEM((RW_CHUNK, BRANCH_W), F32)],
        compiler_params=_cparams(1),
        name="rwkv7_sample",
    )(p, prev, st0, lw["mu"], lw["w0"], lw["w_wa"], lw["a0"], lw["w_g2"], lw["k_k"], lw["k_a"],
      lw["r_k"], lw["ln_w"], lw["ln_b"], ones_bd)


RC = 64


def _mm(a, b):
    return _dot(a.astype(BF16), b.astype(BF16))


def _rwkv_chunk_kernel(p_ref, mu_ref, w0_ref, wwa_ref, a0_ref, wg2_ref, kk_ref, ka_ref, rk_ref,
                       lnw_ref, lnb_ref, ones_ref, o_ref, st_out_ref, carry_ref, z_ref):
    ns = p_ref.shape[0]
    seqs = range(ns)
    c = pl.program_id(0)

    @pl.when(c == 0)
    def _():
        carry_ref[...] = jnp.zeros_like(carry_ref)
        z_ref[...] = jnp.zeros_like(z_ref)

    rows = ns * RC
    p = p_ref[...].reshape(rows, RWKV_P)
    t_loc = lax.rem(lax.broadcasted_iota(I32, (rows, RWKV_P), 0), RC)
    seq_of_row = lax.broadcasted_iota(I32, (rows, RWKV_P), 0) // RC
    first = carry_ref[0]
    for s in range(1, ns):
        first = jnp.where(seq_of_row == s, carry_ref[s], first)
    prev = jnp.where(t_loc == 0, first, pltpu.roll(p, 1, 0))
    for s in seqs:
        carry_ref[s] = p[s * RC + RC - 1:(s + 1) * RC, :]
    ones_bd = ones_ref[...]
    r, log_decay, kp, v, kk, kka, g, bonus = _rwkv_prep(
        p, prev, mu_ref[...], w0_ref[...], wwa_ref[...], a0_ref[...], wg2_ref[...],
        kk_ref[...], ka_ref[...], rk_ref[...], ones_bd)

    trow = lax.rem(lax.broadcasted_iota(I32, (rows, BRANCH_W), 0), RC)
    cs = log_decay
    for sh in (1, 2, 4, 8, 16, 32):
        cs = cs + jnp.where(trow >= sh, pltpu.roll(cs, sh, 0), 0.0)
    pw = jnp.exp(cs)
    inv_pw = jnp.exp(-cs)
    a_m = -(kk * jnp.exp(cs - log_decay))
    r_m = r * pw
    b_m = kka * inv_pw
    k_m = kp * inv_pw
    sl = lambda x, s: x[s * RC:(s + 1) * RC]
    bkt = [jnp.concatenate([sl(b_m, s), sl(k_m, s)], axis=0).T for s in seqs]
    kbt = [jnp.concatenate([sl(k_m, s), sl(b_m, s)], axis=0).T for s in seqs]
    pwt = [jnp.concatenate([sl(pw, s), sl(pw, s)], axis=0).T for s in seqs]

    lane_head = lax.broadcasted_iota(I32, (RC, BRANCH_W), 1) // HEAD_DIM
    hm = [lane_head == h for h in range(N_HEADS)]
    xstack = [jnp.concatenate([jnp.where(hm[h], sl(a_m, s), 0.0) for h in range(N_HEADS)]
                              + [jnp.where(hm[h], sl(r_m, s), 0.0) for h in range(N_HEADS)], axis=0)
              for s in seqs]
    g1 = [_mm(xstack[s], bkt[s]) for s in seqs]
    g2 = [_mm(xstack[s], kbt[s]) for s in seqs]
    nst = N_HEADS * RC
    t_of_row = lax.rem(lax.broadcasted_iota(I32, (nst, LANES), 0), RC)
    s_of_lane = lax.broadcasted_iota(I32, (nst, LANES), 1)
    strict = s_of_lane < t_of_row
    incl = jnp.logical_and(s_of_lane <= t_of_row, s_of_lane < RC)
    l_st = [jnp.where(strict, g1[s][0:nst], 0.0) for s in seqs]
    m_st = [jnp.where(strict, g2[s][0:nst], 0.0) for s in seqs]
    n_st = [jnp.where(incl, g1[s][nst:2 * nst], 0.0) for s in seqs]
    q_st = [jnp.where(incl, g2[s][nst:2 * nst], 0.0) for s in seqs]

    eye = jnp.where(lax.broadcasted_iota(I32, (RC, LANES), 0) == lax.broadcasted_iota(I32, (RC, LANES), 1),
                    1.0, 0.0)
    lps = [l_st[s][h * RC:(h + 1) * RC] for s in seqs for h in range(N_HEADS)]
    ws = [eye + lp for lp in lps]
    for _ in range(5):
        lps = [_mm(lp[:, :RC], lp) for lp in lps]
        ws = [w_h + _mm(w_h[:, :RC], lp) for w_h, lp in zip(ws, lps)]

    def per_head(stacked):
        out = jnp.where(hm[0], stacked[0:RC], 0.0)
        for h in range(1, N_HEADS):
            out = out + jnp.where(hm[h], stacked[h * RC:(h + 1) * RC], 0.0)
        return out

    row_head = lax.broadcasted_iota(I32, (BRANCH_W, BRANCH_W), 0) // HEAD_DIM
    col_head = lax.broadcasted_iota(I32, (BRANCH_W, BRANCH_W), 1) // HEAD_DIM
    z0 = [z_ref[s] for s in seqs]
    zbd = [jnp.where(row_head == col_head, jnp.concatenate([z0[s]] * N_HEADS, axis=0), 0.0) for s in seqs]
    vs = [sl(v, s) for s in seqs]
    az = [_mm(sl(a_m, s), zbd[s]) for s in seqs]
    mv = [_mm(m_st[s][:, :RC], vs[s]) for s in seqs]
    rz = [_mm(sl(r_m, s), zbd[s]) for s in seqs]
    qv = [_mm(q_st[s][:, :RC], vs[s]) for s in seqs]
    rhs = [az[s] + per_head(mv[s]) for s in seqs]
    wr = [[_mm(ws[s * N_HEADS + h][:, :RC], rhs[s]) for h in range(N_HEADS)] for s in seqs]
    u = [per_head(jnp.concatenate(wr[s], axis=0)) for s in seqs]
    nu = [_mm(n_st[s][:, :RC], u[s]) for s in seqs]
    uv = [jnp.concatenate([u[s], vs[s]], axis=0) for s in seqs]
    upd = [[_mm(bkt[s][h * HEAD_DIM:(h + 1) * HEAD_DIM, :], uv[s]) for h in range(N_HEADS)] for s in seqs]
    y = jnp.concatenate([rz[s] + per_head(nu[s] + qv[s]) for s in seqs], axis=0)
    z_new = [_col_bcast(pwt[s], RC - 1) * (z0[s] + per_head(jnp.concatenate(upd[s], axis=0))) for s in seqs]
    for s in seqs:
        z_ref[s] = z_new[s]

    @pl.when(c == pl.num_programs(0) - 1)
    def _():
        for s in seqs:
            st_out_ref[s] = z_new[s]

    o = (_head_group_norm(y, ones_bd, lnw_ref[...], lnb_ref[...], RWKV_LN_EPS) + bonus) * g
    o_ref[...] = o.reshape(ns, RC, BRANCH_W)


def _rwkv_prompt(p, lw, ones_bd, n_seq, t_len):
    nchunk = t_len // RC
    full2 = lambda a, b: pl.BlockSpec((a, b), lambda c: (0, 0))
    st_spec = pl.BlockSpec((n_seq, HEAD_DIM, BRANCH_W), lambda c: (0, 0, 0))
    o, st = pl.pallas_call(
        _rwkv_chunk_kernel,
        grid=(nchunk,),
        in_specs=[pl.BlockSpec((n_seq, RC, RWKV_P), lambda c: (0, c, COL_RWKV // RWKV_P)),
                  full2(1, RWKV_P), full2(1, BRANCH_W), full2(LANES, 2 * BRANCH_W), full2(1, BRANCH_W),
                  full2(RWKV_G_LORA, BRANCH_W), full2(1, BRANCH_W), full2(1, BRANCH_W), full2(1, BRANCH_W),
                  full2(1, BRANCH_W), full2(1, BRANCH_W), full2(BRANCH_W, BRANCH_W)],
        out_specs=[pl.BlockSpec((n_seq, RC, BRANCH_W), lambda c: (0, c, 0)), st_spec],
        out_shape=[jax.ShapeDtypeStruct((n_seq, t_len, BRANCH_W), F32),
                   jax.ShapeDtypeStruct((n_seq, HEAD_DIM, BRANCH_W), F32)],
        scratch_shapes=[pltpu.VMEM((n_seq, 1, RWKV_P), F32),
                        pltpu.VMEM((n_seq, HEAD_DIM, BRANCH_W), F32)],
        compiler_params=_cparams(1),
        name="rwkv7_prompt",
    )(p.reshape(n_seq, t_len, P_PAD), lw["mu"], lw["w0"], lw["w_wa"], lw["a0"], lw["w_g2"], lw["k_k"], lw["k_a"],
      lw["r_k"], lw["ln_w"], lw["ln_b"], ones_bd)
    return o.reshape(n_seq * t_len, BRANCH_W), st


def _ret_prompt_kernel(p_ref, cos_ref, slo_ref, shi_ref, dmask_ref, qdec_ref, kdec_ref, sdec_ref,
                       gnw_ref, gnb_ref, ones_ref, o_ref, s_out_ref, s_ref):
    ns = p_ref.shape[0]
    c = pl.program_id(0)

    @pl.when(c == 0)
    def _():
        s_ref[...] = jnp.zeros_like(s_ref)

    p = p_ref[...].reshape(ns * RET_CHUNK, RET_P)
    rep = lambda t: jnp.concatenate([t] * ns, axis=0)
    cos, slo, shi = rep(cos_ref[...]), rep(slo_ref[...]), rep(shi_ref[...])
    half = HEAD_DIM // 2
    rq = _rope(p[:, 0:256], cos, slo, shi, half)
    rk = _rope(p[:, 256:512], cos, slo, shi, half) * RET_KSCALE
    rv = p[:, 512:768]
    rg = p[:, 768:1024]
    kd = rk * rep(kdec_ref[...])
    qdec = qdec_ref[...]
    rs = [slice(s * RET_CHUNK, (s + 1) * RET_CHUNK) for s in range(ns)]
    hs = [slice(h * HEAD_DIM, (h + 1) * HEAD_DIM) for h in range(N_HEADS)]
    kdt = [kd[rs[s]].T.astype(BF16) for s in range(ns)]
    sh = [(s, h) for s in range(ns) for h in range(N_HEADS)]
    qh = {k: rq[rs[k[0]], hs[k[1]]].astype(BF16) for k in sh}
    kh = {k: rk[rs[k[0]], hs[k[1]]].astype(BF16) for k in sh}
    vh = {k: rv[rs[k[0]], hs[k[1]]].astype(BF16) for k in sh}
    s_old = {k: s_ref[k[0], hs[k[1]], :] for k in sh}
    att = {k: _dot_nt(qh[k], kh[k]) for k in sh}
    inter = {k: _dot(qh[k], s_old[k].astype(BF16)) for k in sh}
    upd = {k: _dot(kdt[k[0]][hs[k[1]], :], vh[k]) for k in sh}
    intra = {k: _dot((att[k] * dmask_ref[k[1]]).astype(BF16), vh[k]) for k in sh}
    for k in sh:
        s_ref[k[0], hs[k[1]], :] = s_old[k] * sdec_ref[hs[k[1]], :] + upd[k]
    o = jnp.concatenate(
        [jnp.concatenate([intra[(s, h)] + inter[(s, h)] * qdec[:, hs[h]] for h in range(N_HEADS)], axis=1)
         for s in range(ns)], axis=0)
    gn = _head_group_norm(o, ones_ref[...], gnw_ref[...], gnb_ref[...], GN_EPS)
    o_ref[...] = (rg * jax.nn.sigmoid(rg) * gn).reshape(ns, RET_CHUNK, BRANCH_W)

    @pl.when(c == pl.num_programs(0) - 1)
    def _():
        s_out_ref[...] = s_ref[...]


def _ret_prompt(p, tables, dec, gn_w, gn_b, ones_bd, n_seq, t_len):
    nchunk = t_len // RET_CHUNK
    cos, slo, shi = tables
    dmask, qdec, kdec, sdec = dec
    tab = pl.BlockSpec((RET_CHUNK, BRANCH_W), lambda c: (c, 0))
    full2 = lambda a, b: pl.BlockSpec((a, b), lambda c: (0, 0))
    st_spec = pl.BlockSpec((n_seq, BRANCH_W, HEAD_DIM), lambda c: (0, 0, 0))
    o, st = pl.pallas_call(
        _ret_prompt_kernel,
        grid=(nchunk,),
        in_specs=[pl.BlockSpec((n_seq, RET_CHUNK, RET_P), lambda c: (0, c, COL_RET // RET_P)),
                  tab, tab, tab,
                  pl.BlockSpec((N_HEADS, RET_CHUNK, RET_CHUNK), lambda c: (0, 0, 0)),
                  full2(RET_CHUNK, BRANCH_W), full2(RET_CHUNK, BRANCH_W), full2(BRANCH_W, HEAD_DIM),
                  full2(1, BRANCH_W), full2(1, BRANCH_W), full2(BRANCH_W, BRANCH_W)],
        out_specs=[pl.BlockSpec((n_seq, RET_CHUNK, BRANCH_W), lambda c: (0, c, 0)), st_spec],
        out_shape=[jax.ShapeDtypeStruct((n_seq, t_len, BRANCH_W), F32),
                   jax.ShapeDtypeStruct((n_seq, BRANCH_W, HEAD_DIM), F32)],
        scratch_shapes=[pltpu.VMEM((n_seq, BRANCH_W, HEAD_DIM), F32)],
        compiler_params=_cparams(1),
        name="retention_prompt",
    )(p.reshape(n_seq, t_len, P_PAD), cos, slo, shi, dmask, qdec, kdec, sdec,
      gn_w.reshape(1, BRANCH_W), gn_b.reshape(1, BRANCH_W), ones_bd)
    return o.reshape(n_seq * t_len, BRANCH_W), st


def _ret_sample_kernel(p_ref, cos_ref, slo_ref, shi_ref, st0_ref, gamma_ref, gnw_ref, gnb_ref, ones_ref,
                       o_ref, st_out_ref, xt_ref, v_ref, y_ref):
    cos, slo, shi = cos_ref[...], slo_ref[...], shi_ref[...]
    half = HEAD_DIM // 2
    rq = _rope(p_ref[:, 0:256], cos, slo, shi, half)
    rk = _rope(p_ref[:, 256:512], cos, slo, shi, half) * RET_KSCALE
    rg = p_ref[:, 768:1024]
    xt_ref[0] = rq.T
    xt_ref[1] = rk.T
    v_ref[...] = p_ref[:, 512:768]
    gamma = gamma_ref[...]

    def group(gi, carry):
        qt, kt = _group_cols(xt_ref, 2, gi)
        base = pl.multiple_of(gi * RW_GROUP, RW_GROUP)
        vrows = v_ref[pl.ds(base, RW_GROUP), :]
        ys = []
        for i in range(RW_GROUP):
            st = st0_ref[base + i] * gamma + _col_bcast(kt, i) * vrows[i:i + 1, :]
            st_out_ref[base + i] = st
            ys.append(jnp.sum(st * _col_bcast(qt, i), axis=0, keepdims=True))
        y_ref[pl.ds(base, RW_GROUP), :] = jnp.concatenate(ys, axis=0)
        return carry

    lax.fori_loop(0, RW_CHUNK // RW_GROUP, group, 0)
    gn = _head_group_norm(y_ref[...], ones_ref[...], gnw_ref[...], gnb_ref[...], GN_EPS)
    o_ref[...] = rg * jax.nn.sigmoid(rg) * gn


def _ret_sample(p, tables, st0, gamma, gn_w, gn_b, ones_bd):
    n = st0.shape[0]
    assert n == RW_CHUNK
    cos, slo, shi = tables
    full2 = lambda a, b: pl.BlockSpec((a, b), lambda i: (0, 0))
    st_spec = pl.BlockSpec((n, HEAD_DIM, BRANCH_W), lambda i: (0, 0, 0))
    return pl.pallas_call(
        _ret_sample_kernel,
        grid=(1,),
        in_specs=[pl.BlockSpec((n, RET_P), lambda i: (0, COL_RET // RET_P)),
                  full2(n, BRANCH_W), full2(n, BRANCH_W), full2(n, BRANCH_W), st_spec,
                  full2(1, BRANCH_W), full2(1, BRANCH_W), full2(1, BRANCH_W), full2(BRANCH_W, BRANCH_W)],
        out_specs=[full2(n, BRANCH_W), st_spec],
        out_shape=[jax.ShapeDtypeStruct((n, BRANCH_W), F32), jax.ShapeDtypeStruct(st0.shape, F32)],
        scratch_shapes=[pltpu.VMEM((2, BRANCH_W, RW_CHUNK), F32),
                        pltpu.VMEM((RW_CHUNK, BRANCH_W), F32),
                        pltpu.VMEM((RW_CHUNK, BRANCH_W), F32)],
        compiler_params=_cparams(1),
        name="retention_sample",
    )(p, cos, slo, shi, st0, gamma, gn_w.reshape(1, BRANCH_W), gn_b.reshape(1, BRANCH_W), ones_bd)


POOL_HIST = 16


def _pool_windows(lane):
    win = jnp.full(lane.shape, float(POOL_WINDOWS[-1]), F32)
    for gi in range(len(POOL_WINDOWS) - 2, -1, -1):
        win = jnp.where(lane < (gi + 1) * HEAD_DIM, float(POOL_WINDOWS[gi]), win)
    return win


def _pool_select(sums, lane):
    out = sums[-1]
    for gi in range(len(POOL_WINDOWS) - 2, -1, -1):
        out = jnp.where(lane < (gi + 1) * HEAD_DIM, sums[gi], out)
    return out


def _pool_prompt_kernel(tm, u_ref, wp_ref, scale_ref, y_ref, nbuf_ref, ext_ref):
    t = pl.program_id(1)

    @pl.when(t == 0)
    def _():
        ext_ref[0:POOL_HIST, :] = jnp.zeros((POOL_HIST, BRANCH_W), F32)

    @pl.when(t > 0)
    def _():
        ext_ref[0:POOL_HIST, :] = ext_ref[tm:tm + POOL_HIST, :]

    u = u_ref[...]
    ext_ref[POOL_HIST:POOL_HIST + tm, :] = u
    sums, acc, i = [], u, 1
    for win in POOL_WINDOWS:
        while i < win:
            acc = acc + ext_ref[pl.ds(POOL_HIST - i, tm), :]
            i += 1
        sums.append(acc)
    lane = lax.broadcasted_iota(I32, u.shape, 1)
    pos = (lax.broadcasted_iota(I32, u.shape, 0) + t * tm).astype(F32)
    cnt = jnp.minimum(pos + 1.0, _pool_windows(lane))
    d = _pool_select(sums, lane) / cnt - u
    y_ref[...] = _dot(d.astype(BF16), wp_ref[...]) * scale_ref[...]

    @pl.when(t == pl.num_programs(1) - 1)
    def _():
        nbuf_ref[0] = ext_ref[tm:tm + POOL_HIST, :]


def _pool_prompt(p, wp_bd, scale, n_seq, t_len, tm):
    tpb = t_len // tm
    return pl.pallas_call(
        functools.partial(_pool_prompt_kernel, tm),
        grid=(n_seq, tpb),
        in_specs=[pl.BlockSpec((tm, BRANCH_W), lambda s, t: (s * tpb + t, COL_POOL // BRANCH_W)),
                  pl.BlockSpec((BRANCH_W, BRANCH_W), lambda s, t: (0, 0)),
                  pl.BlockSpec((1, BRANCH_W), lambda s, t: (0, 0))],
        out_specs=[pl.BlockSpec((tm, BRANCH_W), lambda s, t: (s * tpb + t, 0)),
                   pl.BlockSpec((1, POOL_HIST, BRANCH_W), lambda s, t: (s, 0, 0))],
        out_shape=[jax.ShapeDtypeStruct((n_seq * t_len, BRANCH_W), F32),
                   jax.ShapeDtypeStruct((n_seq, POOL_HIST, BRANCH_W), F32)],
        scratch_shapes=[pltpu.VMEM((tm + POOL_HIST, BRANCH_W), F32)],
        compiler_params=_cparams(2),
        name="pool_prompt",
    )(p, wp_bd, scale.reshape(1, BRANCH_W))


def _pool_sample_kernel(pos0, u_ref, buf_ref, wp_ref, scale_ref, y_ref, nbuf_ref):
    u = u_ref[...]
    sums, acc, i = [], u, 1
    for win in POOL_WINDOWS:
        while i < win:
            acc = acc + buf_ref[POOL_BUF - i]
            i += 1
        sums.append(acc)
    lane = lax.broadcasted_iota(I32, u.shape, 1)
    cnt = jnp.minimum(float(pos0) + 1.0, _pool_windows(lane))
    d = _pool_select(sums, lane) / cnt - u
    y_ref[...] = _dot(d.astype(BF16), wp_ref[...]) * scale_ref[...]
    for r in range(POOL_BUF - 1):
        nbuf_ref[r] = buf_ref[r + 1]
    nbuf_ref[POOL_BUF - 1] = u


def _pool_sample(p, buf_t, wp_bd, scale, pos0):
    n = buf_t.shape[1]
    full3 = pl.BlockSpec((POOL_BUF, n, BRANCH_W), lambda i: (0, 0, 0))
    return pl.pallas_call(
        functools.partial(_pool_sample_kernel, pos0),
        grid=(1,),
        in_specs=[pl.BlockSpec((n, BRANCH_W), lambda i: (0, COL_POOL // BRANCH_W)), full3,
                  pl.BlockSpec((BRANCH_W, BRANCH_W), lambda i: (0, 0)),
                  pl.BlockSpec((1, BRANCH_W), lambda i: (0, 0))],
        out_specs=[pl.BlockSpec((n, BRANCH_W), lambda i: (0, 0)), full3],
        out_shape=[jax.ShapeDtypeStruct((n, BRANCH_W), F32),
                   jax.ShapeDtypeStruct((POOL_BUF, n, BRANCH_W), F32)],
        compiler_params=_cparams(1),
        name="pool_sample",
    )(p, buf_t, wp_bd, scale.reshape(1, BRANCH_W))


def _page_copies(pt_ref, layer, seq, slot, hbm_ref, buf_ref, sem_ref, n_pages):
    return [pltpu.make_async_copy(hbm_ref.at[layer, pt_ref[seq, pg]], buf_ref.at[slot, pg], sem_ref.at[slot])
            for pg in range(n_pages)]


def _paged_prefetch(pt_ref, layer, pairs, n_pages):
    b = pl.program_id(0)
    nb = pl.num_programs(0)
    slot = lax.rem(b, 2)

    @pl.when(b == 0)
    def _():
        for hbm_ref, buf_ref, sem_ref in pairs:
            for cp in _page_copies(pt_ref, layer, 0, 0, hbm_ref, buf_ref, sem_ref, n_pages):
                cp.start()

    @pl.when(b + 1 < nb)
    def _():
        for hbm_ref, buf_ref, sem_ref in pairs:
            for cp in _page_copies(pt_ref, layer, b + 1, 1 - slot, hbm_ref, buf_ref, sem_ref, n_pages):
                cp.start()

    for hbm_ref, buf_ref, sem_ref in pairs:
        for cp in _page_copies(pt_ref, layer, b, slot, hbm_ref, buf_ref, sem_ref, n_pages):
            cp.wait()
    return slot


def _dsa_sample_score_kernel(layer, n_pages, pt_ref, qi_ref, wi_ref, cki_ref, out_ref, buf_ref, sem_ref):
    slot = _paged_prefetch(pt_ref, layer, [(cki_ref, buf_ref, sem_ref)], n_pages)
    qi4, wi4 = qi_ref[0], wi_ref[0]
    rows = []
    for pg in range(n_pages):
        s = jnp.maximum(_dot(qi4, buf_ref[slot, pg].astype(BF16)), 0.0)
        rows.append(jnp.sum(s * wi4, axis=0, keepdims=True))
    out_ref[0] = jnp.concatenate(rows, axis=1)


def _dsa_sample_scores(page_table, qi4, wi4, cache_kidx_t, layer):
    nb, n_pages = page_table.shape
    page = cache_kidx_t.shape[3]
    past = n_pages * page
    return pl.pallas_call(
        functools.partial(_dsa_sample_score_kernel, layer, n_pages),
        grid_spec=pltpu.PrefetchScalarGridSpec(
            num_scalar_prefetch=1,
            grid=(nb,),
            in_specs=[pl.BlockSpec((1, N_HEADS, HEAD_DIM), lambda b, pt: (b, 0, 0)),
                      pl.BlockSpec((1, N_HEADS, 1), lambda b, pt: (b, 0, 0)),
                      pl.BlockSpec(memory_space=pl.ANY)],
            out_specs=pl.BlockSpec((1, 1, past), lambda b, pt: (b, 0, 0)),
            scratch_shapes=[pltpu.VMEM((2, n_pages, HEAD_DIM, page), F32),
                            pltpu.SemaphoreType.DMA((2,))]),
        out_shape=jax.ShapeDtypeStruct((nb, 1, past), F32),
        compiler_params=_cparams(1),
        name="dsa_sample_scores",
    )(page_table, qi4, wi4, cache_kidx_t)


def _dsa_sample_select_kernel(n_sel, sc_ref, qib_ref, kib_ref, wi_ref, sel_ref, selself_ref):
    nb, past = sc_ref.shape
    qi = qib_ref[...].astype(F32)
    ki = kib_ref[...].astype(F32)[:, :HEAD_DIM]
    wi = wi_ref[...]
    s_self = jnp.zeros((nb, 1), F32)
    for h in range(N_HEADS):
        dot_h = jnp.sum(qi[:, h * HEAD_DIM:(h + 1) * HEAD_DIM] * ki, axis=-1, keepdims=True)
        s_self = s_self + jnp.maximum(dot_h, 0.0) * wi[:, h:h + 1]
    key = _float_key(sc_ref[...] + 0.0)
    key_self = _float_key(s_self)

    def count(pred):
        return (jnp.sum(pred(key).astype(I32), axis=-1, keepdims=True) + pred(key_self).astype(I32))

    def bit_step(i, tau):
        cand = tau + (jnp.int32(1) << (31 - i))
        return jnp.where(count(lambda x: x >= cand) >= n_sel, cand, tau)

    tau = lax.fori_loop(0, 32, bit_step, jnp.full((nb, 1), INT_MIN, I32))
    gt = key > tau
    tie = key == tau
    need = (n_sel - count(lambda x: x > tau)).astype(F32)
    utri = (lax.broadcasted_iota(I32, (LANES, LANES), 0)
            < lax.broadcasted_iota(I32, (LANES, LANES), 1)).astype(BF16)
    seen = jnp.zeros((nb, 1), F32)
    for c in range(past // LANES):
        cs = slice(c * LANES, (c + 1) * LANES)
        tief = jnp.where(tie[:, cs], 1.0, 0.0)
        rank = _dot(tief.astype(BF16), utri) + seen
        keep = jnp.logical_or(gt[:, cs], jnp.logical_and(tie[:, cs], rank < need))
        sel_ref[:, cs] = jnp.where(keep, 1.0, 0.0)
        seen = seen + jnp.sum(tief, axis=-1, keepdims=True)
    keep_self = jnp.logical_or(key_self > tau, jnp.logical_and(key_self == tau, seen < need))
    selself_ref[...] = jnp.where(keep_self, 1.0, 0.0)


def _dsa_sample_select(scores, qib, kib, wi, n_sel):
    nb, past = scores.shape
    full2 = lambda a, b: pl.BlockSpec((a, b), lambda i: (0, 0))
    return pl.pallas_call(
        functools.partial(_dsa_sample_select_kernel, n_sel),
        grid=(1,),
        in_specs=[full2(nb, past), full2(nb, BRANCH_W), full2(nb, LANES), full2(nb, 8)],
        out_specs=[full2(nb, past), full2(nb, 1)],
        out_shape=[jax.ShapeDtypeStruct((nb, past), F32), jax.ShapeDtypeStruct((nb, 1), F32)],
        compiler_params=_cparams(1),
        name="dsa_sample_select",
    )(scores, qib, kib, wi)


def _dsa_sample_attend_kernel(layer, n_pages, pt_ref, q_ref, sel_ref, selself_ref, kself_ref, vself_ref,
                              ck_ref, cv_ref, o_ref, kbuf_ref, vbuf_ref, ksem_ref, vsem_ref):
    slot = _paged_prefetch(pt_ref, layer, [(ck_ref, kbuf_ref, ksem_ref), (cv_ref, vbuf_ref, vsem_ref)], n_pages)
    head_of_lane = lax.broadcasted_iota(I32, (N_HEADS, BRANCH_W), 1) // HEAD_DIM
    own = head_of_lane == lax.broadcasted_iota(I32, (N_HEADS, BRANCH_W), 0)
    qrows = jnp.where(own, jnp.broadcast_to(q_ref[0], (N_HEADS, BRANCH_W)), 0.0).astype(BF16)
    lg = jnp.concatenate([_dot(qrows, kbuf_ref[slot, pg].astype(BF16)) for pg in range(n_pages)], axis=1)
    lg = jnp.where(sel_ref[0] > 0.0, lg, NEG_BIG)
    kself = kself_ref[0].astype(BF16).astype(F32)
    lself = jnp.sum(qrows.astype(F32) * kself, axis=-1, keepdims=True)
    lself = jnp.where(selself_ref[0] > 0.0, lself, NEG_BIG)
    m = jnp.maximum(jnp.max(lg, axis=-1, keepdims=True), lself)
    p = jnp.exp(lg - m)
    pself = jnp.exp(lself - m)
    denom = jnp.sum(p, axis=-1, keepdims=True) + pself
    p = p.astype(BF16)
    page = kbuf_ref.shape[3]
    o4 = pself * vself_ref[0]
    for pg in range(n_pages):
        o4 = o4 + _dot_nt(p[:, pg * page:(pg + 1) * page], vbuf_ref[slot, pg].astype(BF16))
    o_ref[0] = jnp.sum(jnp.where(own, o4 / denom, 0.0), axis=0, keepdims=True)


def _dsa_sample_attend(page_table, q, sel, selself, kself, vself, cache_k_t, cache_v_t, layer):
    nb, n_pages = page_table.shape
    page = cache_k_t.shape[3]
    past = n_pages * page
    row3 = lambda w: pl.BlockSpec((1, 1, w), lambda b, pt: (b, 0, 0))
    return pl.pallas_call(
        functools.partial(_dsa_sample_attend_kernel, layer, n_pages),
        grid_spec=pltpu.PrefetchScalarGridSpec(
            num_scalar_prefetch=1,
            grid=(nb,),
            in_specs=[row3(BRANCH_W), row3(past), row3(1), row3(BRANCH_W), row3(BRANCH_W),
                      pl.BlockSpec(memory_space=pl.ANY), pl.BlockSpec(memory_space=pl.ANY)],
            out_specs=row3(BRANCH_W),
            scratch_shapes=[pltpu.VMEM((2, n_pages, BRANCH_W, page), F32),
                            pltpu.VMEM((2, n_pages, BRANCH_W, page), F32),
                            pltpu.SemaphoreType.DMA((2,)),
                            pltpu.SemaphoreType.DMA((2,))]),
        out_shape=jax.ShapeDtypeStruct((nb, 1, BRANCH_W), F32),
        compiler_params=_cparams(1),
        name="dsa_sample_attend",
    )(page_table, q.reshape(nb, 1, BRANCH_W), sel.reshape(nb, 1, past), selself.reshape(nb, 1, 1),
      kself.reshape(nb, 1, BRANCH_W), vself.reshape(nb, 1, BRANCH_W), cache_k_t, cache_v_t)


def _rope_tables(pos, rot_dim, theta):
    half = rot_dim // 2
    inv = theta ** (-jnp.arange(half, dtype=F32) / half)
    ang = pos.astype(F32)[:, None] * inv[None, :]
    cos, sin = jnp.cos(ang), jnp.sin(ang)
    n = pos.shape[0]
    zh = jnp.zeros((n, half), F32)
    zr = jnp.zeros((n, HEAD_DIM - rot_dim), F32)
    c = jnp.concatenate([cos, cos, jnp.ones((n, HEAD_DIM - rot_dim), F32)], axis=1)
    lo = jnp.concatenate([-sin, zh, zr], axis=1)
    hi = jnp.concatenate([zh, sin, zr], axis=1)
    return tuple(jnp.tile(t, (1, N_HEADS)) for t in (c, lo, hi))


def _ret_decay(chunk):
    log_g = jnp.log1p(-jnp.exp2(-5.0 - jnp.arange(N_HEADS, dtype=F32)))
    n = jnp.arange(chunk, dtype=F32)
    diff = n[:, None] - n[None, :]
    dmask = jnp.where(diff >= 0, jnp.exp(log_g[:, None, None] * jnp.maximum(diff, 0.0)), 0.0)
    q_dec = jnp.exp(log_g[:, None] * (n + 1.0)).T
    k_dec = jnp.exp(log_g[:, None] * (chunk - 1.0 - n))
    s_dec = jnp.exp(log_g * chunk)
    qdec = jnp.repeat(q_dec, HEAD_DIM, axis=1)
    kdec = jnp.repeat(k_dec.T, HEAD_DIM, axis=1)
    sdec_rows = jnp.repeat(s_dec, HEAD_DIM)
    return dmask, qdec, kdec, sdec_rows


def _block_diag(blocks):
    n = len(blocks)
    rows = []
    for i, blk in enumerate(blocks):
        rows.append(jnp.concatenate([blk if j == i else jnp.zeros((blk.shape[0], blocks[j].shape[1]), blk.dtype)
                                     for j in range(n)], axis=1))
    return jnp.concatenate(rows, axis=0)


def kernel(x_prompt, x_sample, cache_k, cache_v, cache_kidx, state_wkv, state_shift, state_ret, state_pool, page_table, norm1_w, w_in, rwkv_mu, rwkv_w0, rwkv_w_w2, rwkv_a0, rwkv_w_a2, rwkv_w_g2, rwkv_k_k, rwkv_k_a, rwkv_r_k, rwkv_ln_w, rwkv_ln_b, q_norm_w, k_norm_w, idx_k_norm_w, ret_gn_w, ret_gn_b, pool_w, pool_scale, w_branch, w_out, norm2_w, w_ffn_in, w_ffn_out):
    n_seq, t_len, _ = x_prompt.shape
    nb = x_sample.shape[0]
    assert x_sample.shape[1] == 1
    page = cache_k.shape[2]
    past = page_table.shape[1] * page
    n_sel_sample = min(TOPK_MAX, (past + 1) // 4)
    tm_p = min(512, t_len)
    tm_m = min(256, t_len)

    ones_bd = _block_diag([jnp.ones((HEAD_DIM, HEAD_DIM), BF16)] * N_HEADS)
    pos_p = jnp.arange(t_len)
    pos_s = jnp.full((nb,), past)
    att_tab_p, att_tab_s = _rope_tables(pos_p, ROT_DIM, ROPE_THETA), _rope_tables(pos_s, ROT_DIM, ROPE_THETA)
    ret_tab_p, ret_tab_s = _rope_tables(pos_p, HEAD_DIM, RET_THETA), _rope_tables(pos_s, HEAD_DIM, RET_THETA)
    dmask, qdec, kdec, sdec_rows = _ret_decay(math.gcd(t_len, RET_CHUNK))
    dec_p = (dmask, qdec, kdec, jnp.broadcast_to(sdec_rows[:, None], (BRANCH_W, HEAD_DIM)))
    gamma_s = _ret_decay(1)[3].reshape(1, BRANCH_W)

    n_pool = cache_k.shape[1]
    cki_t = cache_kidx.transpose(0, 1, 3, 2)
    ck_t = cache_k.transpose(0, 1, 3, 4, 2).reshape(-1, n_pool, BRANCH_W, page)
    cv_t = cache_v.transpose(0, 1, 3, 4, 2).reshape(-1, n_pool, BRANCH_W, page)

    xp = x_prompt.reshape(n_seq * t_len, D_MODEL)
    xs = x_sample.reshape(nb, D_MODEL)
    prompt_new = [[] for _ in range(7)]
    sample_new = [[] for _ in range(7)]
    o_att, o_ret, o_pool = RWKV_P, RWKV_P + ATT_P, RWKV_P + ATT_P + RET_P
    o_gate = o_pool + BRANCH_W
    row1 = lambda a: a.reshape(1, -1)
    w_in_t = w_in.transpose(2, 0, 1)

    for l in range(DEPTH):
        w = w_in_t[:, l, :]
        w_pad = jnp.concatenate(
            [w[o_gate:], w[:RWKV_P], w[o_ret:o_pool], w[o_pool:o_gate], w[o_att:o_ret],
             jnp.zeros((ATT_P_PAD - ATT_P, D_MODEL), F32)], axis=0).astype(BF16)
        lw = dict(mu=row1(rwkv_mu[l]), w0=row1(rwkv_w0[l]),
                  w_wa=_block_diag([rwkv_w_w2[l], rwkv_w_a2[l]]).astype(BF16),
                  a0=row1(rwkv_a0[l]), w_g2=rwkv_w_g2[l].astype(BF16), k_k=row1(rwkv_k_k[l]),
                  k_a=row1(rwkv_k_a[l]), r_k=row1(rwkv_r_k[l]), ln_w=row1(rwkv_ln_w[l]), ln_b=row1(rwkv_ln_b[l]))
        wp_bd = _block_diag([pool_w[l, g] for g in range(N_HEADS)]).astype(BF16)
        wb = w_branch[l].astype(BF16)
        wo = w_out[l].astype(BF16)
        wfi = w_ffn_in[l].astype(BF16)
        wfo = w_ffn_out[l].astype(BF16)

        p = _norm_matmul(xp, norm1_w[l], w_pad, min(1024, t_len), P_PAD // 5)
        o_a, wkv = _rwkv_prompt(p, lw, ones_bd, n_seq, t_len)
        q, k, v, ki, kb, vt, qib, kib, wit = _att_prep(p, att_tab_p, q_norm_w[l], k_norm_w[l], idx_k_norm_w[l],
                                                     ones_bd, n_seq, t_len, tm_p)
        o_b = _dsa_prompt(q, qib, wit, kb, vt, kib, n_seq, t_len)
        o_c, ret = _ret_prompt(p, ret_tab_p, dec_p, ret_gn_w[l], ret_gn_b[l], ones_bd, n_seq, t_len)
        o_d, pbuf = _pool_prompt(p, wp_bd, pool_scale[l], n_seq, t_len, tm_p)
        x1 = _merge(xp, (o_a, o_b, o_c, o_d), p, wb, wo, tm_p)
        xp = _ffn(x1, norm2_w[l], wfi, wfo, tm_m)
        st = (k.reshape(n_seq, t_len, N_HEADS, HEAD_DIM), v.reshape(n_seq, t_len, N_HEADS, HEAD_DIM),
              ki[:, :HEAD_DIM].reshape(n_seq, t_len, HEAD_DIM),
              wkv.reshape(n_seq, HEAD_DIM, N_HEADS, HEAD_DIM).transpose(0, 2, 3, 1),
              p.reshape(n_seq, t_len, P_PAD)[:, -1, COL_RWKV:COL_RWKV + RWKV_P],
              ret.reshape(n_seq, N_HEADS, HEAD_DIM, HEAD_DIM),
              pbuf[:, POOL_HIST - POOL_BUF:, :])
        for lst, s in zip(prompt_new, st):
            lst.append(s)

        ps = _norm_matmul(xs, norm1_w[l], w_pad, nb, P_PAD // 5)
        wkv0 = state_wkv[l].transpose(0, 3, 1, 2).reshape(nb, HEAD_DIM, BRANCH_W)
        o_a, wkv = _rwkv_sample(ps, state_shift[l], wkv0, lw, ones_bd)
        q, k, v, ki, kb, vt, qib, kib, wit = _att_prep(ps, att_tab_s, q_norm_w[l], k_norm_w[l], idx_k_norm_w[l],
                                                     ones_bd, 1, nb, nb)
        wi_rows = wit[0].T
        scores = _dsa_sample_scores(page_table, qib.reshape(nb, N_HEADS, HEAD_DIM),
                                    wi_rows[:, :N_HEADS].reshape(nb, N_HEADS, 1), cki_t, l)
        sel, selself = _dsa_sample_select(scores.reshape(nb, past), qib, kib, wi_rows, n_sel_sample)
        o_b = _dsa_sample_attend(page_table, q, sel, selself, k, v, ck_t, cv_t, l).reshape(nb, BRANCH_W)
        ret0 = state_ret[l].transpose(0, 2, 1, 3).reshape(nb, HEAD_DIM, BRANCH_W)
        o_c, ret = _ret_sample(ps, ret_tab_s, ret0, gamma_s, ret_gn_w[l], ret_gn_b[l], ones_bd)
        o_d, pbuf = _pool_sample(ps, state_pool[l].transpose(1, 0, 2), wp_bd, pool_scale[l], past)
        x1 = _merge(xs, (o_a, o_b, o_c, o_d), ps, wb, wo, nb)
        xs = _ffn(x1, norm2_w[l], wfi, wfo, nb)
        st = (k.reshape(nb, 1, N_HEADS, HEAD_DIM), v.reshape(nb, 1, N_HEADS, HEAD_DIM),
              ki[:, :HEAD_DIM].reshape(nb, 1, HEAD_DIM),
              wkv.reshape(nb, HEAD_DIM, N_HEADS, HEAD_DIM).transpose(0, 2, 3, 1),
              ps[:, COL_RWKV:COL_RWKV + RWKV_P],
              ret.reshape(nb, HEAD_DIM, N_HEADS, HEAD_DIM).transpose(0, 2, 1, 3),
              pbuf.transpose(1, 0, 2))
        for lst, s in zip(sample_new, st):
            lst.append(s)

    outs_p = [jnp.stack(a) for a in prompt_new]
    outs_s = [jnp.stack(a) for a in sample_new]
    return (xp.reshape(n_seq, t_len, D_MODEL), xs.reshape(nb, 1, D_MODEL), *outs_p, *outs_s)
```
